```python
import math
import jax, jax.numpy as jnp
from jax import lax
import numpy as np

D_MODEL = 2048
BATCH = 8
SEQ = 4096
DEPTH = 2
DEC_BATCH = 16
DEC_SEQ = 32
PAST_LEN = 2048

CHUNK = 64
N_HEADS = 8
HEAD_DIM = 128
N_SUB = 2 * N_HEADS
D_ATT = N_HEADS * 2 * HEAD_DIM
Q_BLOCK = 128
D_SSM = D_MODEL
GROUP_CH = 16
N_GROUPS = D_SSM // GROUP_CH
STATE_P = 64
SCAN_BLOCK = CHUNK
D_FF = 5632
CONV_W = 3
LN_EPS = 1e-5
ALPHA = (2 * DEPTH) ** 0.25
BETA = (8 * DEPTH) ** -0.25
N_SSM_LAYERS = (DEPTH + 1) // 2
N_ATTN_LAYERS = DEPTH // 2

kernel_name = "hybrid_s5_diffattn_convffn_stream_step"


def layer_norm(x, g, b):
    xf = x.astype(jnp.float32)
    mu = jnp.mean(xf, axis=-1, keepdims=True)
    var = jnp.mean(jnp.square(xf - mu), axis=-1, keepdims=True)
    return ((xf - mu) * lax.rsqrt(var + LN_EPS) * g.astype(jnp.float32) + b.astype(jnp.float32)).astype(x.dtype)


def ada_mod(c, w, b):
    m = (jax.nn.silu(c) @ w + b)[:, None, :]
    return jnp.split(m, 6, axis=-1)


def conv_ffn(h, buf, w_up, w_conv, b_conv, w_down):
    g, v = jnp.split(h @ w_up, 2, axis=-1)
    gp = jnp.concatenate([buf.astype(g.dtype), g], axis=1)
    L = g.shape[1]
    conv = b_conv
    for kk in range(CONV_W):
        conv = conv + gp[:, kk:kk + L] * w_conv[kk]
    out = (jax.nn.gelu(conv, approximate=False) * v) @ w_down
    return out, gp[:, -(CONV_W - 1):]


def ssm_discretise(lam_re, lam_im, log_step, b_re, b_im):
    lr = jnp.minimum(lam_re.astype(jnp.float32), -1e-4)
    li = lam_im.astype(jnp.float32)
    dt = jnp.exp(log_step.astype(jnp.float32))[:, None]
    mag = jnp.exp(lr * dt)
    abar_re = mag * jnp.cos(li * dt)
    abar_im = mag * jnp.sin(li * dt)
    nr = abar_re - 1.0
    ni = abar_im
    den = lr * lr + li * li
    kr = (nr * lr + ni * li) / den
    ki = (ni * lr - nr * li) / den
    br = b_re.astype(jnp.float32)
    bi = b_im.astype(jnp.float32)
    bbar_re = kr[..., None] * br - ki[..., None] * bi
    bbar_im = kr[..., None] * bi + ki[..., None] * br
    return abar_re, abar_im, bbar_re, bbar_im


def _cmul_combine(e1, e2):
    a1r, a1i, b1r, b1i = e1
    a2r, a2i, b2r, b2i = e2
    return (a2r * a1r - a2i * a1i,
            a2r * a1i + a2i * a1r,
            a2r * b1r - a2i * b1i + b2r,
            a2r * b1i + a2i * b1r + b2i)


def ssm_scan(u, s_re, s_im, abar_re, abar_im, bbar_re, bbar_im, c_re, c_im):
    bsz, L = u.shape[0], u.shape[1]
    blk = L if L <= SCAN_BLOCK else SCAN_BLOCK
    nblk = L // blk
    ub = u.reshape(bsz, nblk, blk, N_GROUPS, GROUP_CH).swapaxes(0, 1)
    cr = c_re.astype(jnp.float32)
    ci = c_im.astype(jnp.float32)

    def step(carry, ublk):
        sr, si = carry
        br = jnp.einsum('btgc,gpc->btgp', ublk, bbar_re)
        bi = jnp.einsum('btgc,gpc->btgp', ublk, bbar_im)
        br = br.at[:, 0].add(abar_re * sr - abar_im * si)
        bi = bi.at[:, 0].add(abar_re * si + abar_im * sr)
        ar = jnp.broadcast_to(abar_re, br.shape)
        ai = jnp.broadcast_to(abar_im, bi.shape)
        _, _, hr, hi = lax.associative_scan(_cmul_combine, (ar, ai, br, bi), axis=1)
        y = jnp.einsum('btgp,gcp->btgc', hr, cr) - jnp.einsum('btgp,gcp->btgc', hi, ci)
        return (hr[:, -1], hi[:, -1]), y

    (sr, si), yb = lax.scan(step, (s_re.astype(jnp.float32), s_im.astype(jnp.float32)), ub)
    y = yb.swapaxes(0, 1).reshape(bsz, L, N_GROUPS, GROUP_CH)
    return y, sr, si


def ssm_mixer(h, s_re, s_im, w_in, disc, c_re, c_im, d_skip, w_glu):
    bsz, L, _ = h.shape
    u = (h @ w_in).astype(jnp.float32).reshape(bsz, L, N_GROUPS, GROUP_CH)
    y, sr, si = ssm_scan(u, s_re, s_im, *disc, c_re, c_im)
    y = y + d_skip.astype(jnp.float32).reshape(N_GROUPS, GROUP_CH) * u
    z = jax.nn.gelu(y.reshape(bsz, L, D_SSM), approximate=False).astype(h.dtype)
    a, g = jnp.split(z @ w_glu, 2, axis=-1)
    return a * jax.nn.sigmoid(g), sr, si


def diff_lambda(lq1, lk1, lq2, lk2, lam_init):
    f = jnp.float32
    return (jnp.exp(jnp.sum(lq1.astype(f) * lk1.astype(f)))
            - jnp.exp(jnp.sum(lq2.astype(f) * lk2.astype(f))) + lam_init)


def qkv_proj(h, w_qkv):
    bsz, L, _ = h.shape
    q, k, v = jnp.split(h @ w_qkv, 3, axis=-1)
    return (q.reshape(bsz, L, N_SUB, HEAD_DIM), k.reshape(bsz, L, N_SUB, HEAD_DIM),
            v.reshape(bsz, L, N_HEADS, 2 * HEAD_DIM))


def diff_attend(q, k, v, qpos, kpos, lam):
    s = jnp.einsum('bqhd,bkhd->bhqk', q, k, preferred_element_type=jnp.float32) * (HEAD_DIM ** -0.5)
    mask = (kpos[None, :] // CHUNK) <= (qpos[:, None] // CHUNK)
    s = jnp.where(mask, s, -1e30)
    p = jax.nn.softmax(s, axis=-1)
    bsz, _, tq, tk = p.shape
    p = p.reshape(bsz, N_HEADS, 2, tq, tk)
    a = p[:, :, 0] - lam * p[:, :, 1]
    return jnp.einsum('bhqk,bkhe->bqhe', a.astype(v.dtype), v)


def diff_out(o, subln_g, lam_init, w_o):
    of = o.astype(jnp.float32)
    of = of * lax.rsqrt(jnp.mean(jnp.square(of), axis=-1, keepdims=True) + LN_EPS)
    of = of * subln_g.astype(jnp.float32) * (1.0 - lam_init)
    bsz, L = o.shape[0], o.shape[1]
    return of.reshape(bsz, L, D_ATT).astype(o.dtype) @ w_o


def diff_attn_prompt(h, w_qkv, lam, lam_init, subln_g, w_o):
    bsz, L, _ = h.shape
    q, k, v = qkv_proj(h, w_qkv)
    nq = L // Q_BLOCK
    qb = q.reshape(bsz, nq, Q_BLOCK, N_SUB, HEAD_DIM).swapaxes(0, 1)
    kpos = jnp.arange(L)

    def blk(args):
        qblk, bi = args
        qpos = bi * Q_BLOCK + jnp.arange(Q_BLOCK)
        return diff_attend(qblk, k, v, qpos, kpos, lam)

    o = lax.map(blk, (qb, jnp.arange(nq)))
    o = o.swapaxes(0, 1).reshape(bsz, L, N_HEADS, 2 * HEAD_DIM)
    return diff_out(o, subln_g, lam_init, w_o), k, v


def diff_attn_sample(h, ck, cv, w_qkv, lam, lam_init, subln_g, w_o):
    T = h.shape[1]
    P = ck.shape[1]
    q, k, v = qkv_proj(h, w_qkv)
    kk = jnp.concatenate([ck.astype(k.dtype), k], axis=1)
    vv = jnp.concatenate([cv.astype(v.dtype), v], axis=1)
    qpos = P + jnp.arange(T)
    kpos = jnp.arange(P + T)
    o = diff_attend(q, kk, vv, qpos, kpos, lam)
    return diff_out(o, subln_g, lam_init, w_o), k, v


def setup_inputs(seed: int = 0) -> dict:
    key = jax.random.key(seed)
    ks = iter(jax.random.split(key, 48))
    f = jnp.float32

    def nrm(shape, scale):
        return jax.random.normal(next(ks), shape, f) * scale

    NS, NA = N_SSM_LAYERS, N_ATTN_LAYERS
    d = {}
    d['x_prompt'] = nrm((BATCH, SEQ, D_MODEL), 1.0)
    d['x_sample'] = nrm((DEC_BATCH, DEC_SEQ, D_MODEL), 1.0)
    d['c_prompt'] = nrm((BATCH, D_MODEL), 1.0)
    d['c_sample'] = nrm((DEC_BATCH, D_MODEL), 1.0)
    d['cache_k'] = nrm((NA, DEC_BATCH, PAST_LEN, N_SUB, HEAD_DIM), 1.0)
    d['cache_v'] = nrm((NA, DEC_BATCH, PAST_LEN, N_HEADS, 2 * HEAD_DIM), BETA)
    d['state_ssm_re'] = nrm((NS, DEC_BATCH, N_GROUPS, STATE_P), 0.5)
    d['state_ssm_im'] = nrm((NS, DEC_BATCH, N_GROUPS, STATE_P), 0.5)
    d['state_conv'] = nrm((DEPTH, DEC_BATCH, CONV_W - 1, D_FF), BETA)
    d['w_ada'] = nrm((DEPTH, D_MODEL, 6 * D_MODEL), 0.1 * D_MODEL ** -0.5)
    d['b_ada'] = nrm((DEPTH, 6 * D_MODEL), 0.01)
    d['ln_g'] = 1.0 + nrm((DEPTH, 2, D_MODEL), 0.02)
    d['ln_b'] = nrm((DEPTH, 2, D_MODEL), 0.02)
    d['w_up'] = nrm((DEPTH, D_MODEL, 2 * D_FF), BETA * D_MODEL ** -0.5)
    d['w_dconv'] = nrm((DEPTH, CONV_W, D_FF), CONV_W ** -0.5)
    d['b_dconv'] = nrm((DEPTH, D_FF), 0.02)
    d['w_down'] = nrm((DEPTH, D_FF, D_MODEL), BETA * D_FF ** -0.5)
    d['w_ssm_in'] = nrm((NS, D_MODEL, D_SSM), BETA * D_MODEL ** -0.5)
    d['ssm_lam_re'] = -0.5 + nrm((NS, N_GROUPS, STATE_P), 0.01)
    d['ssm_lam_im'] = math.pi * jnp.arange(STATE_P, dtype=f)[None, None, :] + nrm((NS, N_GROUPS, STATE_P), 0.01)
    d['ssm_log_step'] = jax.random.uniform(next(ks), (NS, N_GROUPS), f, math.log(1e-3), math.log(1e-1))
    d['ssm_b_re'] = nrm((NS, N_GROUPS, STATE_P, GROUP_CH), (2 * GROUP_CH) ** -0.5)
    d['ssm_b_im'] = nrm((NS, N_GROUPS, STATE_P, GROUP_CH), (2 * GROUP_CH) ** -0.5)
    d['ssm_c_re'] = nrm((NS, N_GROUPS, GROUP_CH, STATE_P), (2 * STATE_P) ** -0.5)
    d['ssm_c_im'] = nrm((NS, N_GROUPS, GROUP_CH, STATE_P), (2 * STATE_P) ** -0.5)
    d['ssm_d'] = nrm((NS, D_SSM), 1.0)
    d['w_glu'] = nrm((NS, D_SSM, 2 * D_MODEL), BETA * D_SSM ** -0.5)
    wq = nrm((NA, D_MODEL, D_ATT), D_MODEL ** -0.5)
    wk = nrm((NA, D_MODEL, D_ATT), D_MODEL ** -0.5)
    wv = nrm((NA, D_MODEL, D_ATT), BETA * D_MODEL ** -0.5)
    d['w_qkv'] = jnp.concatenate([wq, wk, wv], axis=-1)
    d['lam_q1'] = nrm((NA, HEAD_DIM), 0.1)
    d['lam_k1'] = nrm((NA, HEAD_DIM), 0.1)
    d['lam_q2'] = nrm((NA, HEAD_DIM), 0.1)
    d['lam_k2'] = nrm((NA, HEAD_DIM), 0.1)
    d['subln_g'] = 1.0 + nrm((NA, 2 * HEAD_DIM), 0.02)
    d['w_o'] = nrm((NA, D_ATT, D_MODEL), BETA * D_ATT ** -0.5)
    return d


def reference(x_prompt, x_sample, c_prompt, c_sample, cache_k, cache_v, state_ssm_re, state_ssm_im,
              state_conv, w_ada, b_ada, ln_g, ln_b, w_up, w_dconv, b_dconv, w_down, w_ssm_in,
              ssm_lam_re, ssm_lam_im, ssm_log_step, ssm_b_re, ssm_b_im, ssm_c_re, ssm_c_im, ssm_d,
              w_glu, w_qkv, lam_q1, lam_k1, lam_q2, lam_k2, subln_g, w_o):
    xp, xs = x_prompt, x_sample
    bp, bs = xp.shape[0], xs.shape[0]
    kp_l, vp_l, ks_l, vs_l = [], [], [], []
    srp_l, sip_l, srs_l, sis_l = [], [], [], []
    cvp_l, cvs_l = [], []
    for i in range(DEPTH):
        shp1, scp1, gtp1, shp2, scp2, gtp2 = ada_mod(c_prompt, w_ada[i], b_ada[i])
        shs1, scs1, gts1, shs2, scs2, gts2 = ada_mod(c_sample, w_ada[i], b_ada[i])
        hp = xp * (1.0 + scp1) + shp1
        hs = xs * (1.0 + scs1) + shs1
        j = i // 2
        if i % 2 == 0:
            disc = ssm_discretise(ssm_lam_re[j], ssm_lam_im[j], ssm_log_step[j], ssm_b_re[j], ssm_b_im[j])
            zp = jnp.zeros((bp, N_GROUPS, STATE_P), jnp.float32)
            mp, srp, sip = ssm_mixer(hp, zp, zp, w_ssm_in[j], disc, ssm_c_re[j], ssm_c_im[j], ssm_d[j], w_glu[j])
            ms, srs, sis = ssm_mixer(hs, state_ssm_re[j], state_ssm_im[j], w_ssm_in[j], disc,
                                     ssm_c_re[j], ssm_c_im[j], ssm_d[j], w_glu[j])
            srp_l.append(srp); sip_l.append(sip); srs_l.append(srs); sis_l.append(sis)
        else:
            lam_init = 0.8 - 0.6 * math.exp(-0.3 * i)
            lam = diff_lambda(lam_q1[j], lam_k1[j], lam_q2[j], lam_k2[j], lam_init)
            mp, kp, vp = diff_attn_prompt(hp, w_qkv[j], lam, lam_init, subln_g[j], w_o[j])
            ms, kn, vn = diff_attn_sample(hs, cache_k[j], cache_v[j], w_qkv[j], lam, lam_init, subln_g[j], w_o[j])
            kp_l.append(kp); vp_l.append(vp); ks_l.append(kn); vs_l.append(vn)
        xp = layer_norm(ALPHA * xp + (1.0 + gtp1) * mp, ln_g[i, 0], ln_b[i, 0])
        xs = layer_norm(ALPHA * xs + (1.0 + gts1) * ms, ln_g[i, 0], ln_b[i, 0])
        hp = xp * (1.0 + scp2) + shp2
        hs = xs * (1.0 + scs2) + shs2
        fp, cvp = conv_ffn(hp, jnp.zeros((bp, CONV_W - 1, D_FF), xp.dtype), w_up[i], w_dconv[i], b_dconv[i], w_down[i])
        fs, cvs = conv_ffn(hs, state_conv[i], w_up[i], w_dconv[i], b_dconv[i], w_down[i])
        cvp_l.append(cvp); cvs_l.append(cvs)
        xp = layer_norm(ALPHA * xp + (1.0 + gtp2) * fp, ln_g[i, 1], ln_b[i, 1])
        xs = layer_norm(ALPHA * xs + (1.0 + gts2) * fs, ln_g[i, 1], ln_b[i, 1])
    new_k_prompt = jnp.stack(kp_l)
    new_v_prompt = jnp.stack(vp_l)
    new_ssm_re_prompt = jnp.stack(srp_l)
    new_ssm_im_prompt = jnp.stack(sip_l)
    new_conv_prompt = jnp.stack(cvp_l)
    new_k_sample = jnp.stack(ks_l)
    new_v_sample = jnp.stack(vs_l)
    new_ssm_re_sample = jnp.stack(srs_l)
    new_ssm_im_sample = jnp.stack(sis_l)
    new_conv_sample = jnp.stack(cvs_l)
    return (xp, xs, new_k_prompt, new_v_prompt, new_ssm_re_prompt, new_ssm_im_prompt, new_conv_prompt,
            new_k_sample, new_v_sample, new_ssm_re_sample, new_ssm_im_sample, new_conv_sample)
```

```python
import functools
import math

import jax
import jax.numpy as jnp
from jax import lax
from jax.experimental import pallas as pl
from jax.experimental.pallas import tpu as pltpu

F32 = jnp.float32
BF16 = jnp.bfloat16

CHUNK = 64
HEAD_DIM = 128
GROUP_CH = 16
STATE_P = 64
LN_EPS = 1e-5
MASK_VALUE = -1e30

SUBLANES = 8
LANES = 128
VMEM_LIMIT = 56 * 1024 * 1024

SSM_GROUPS_PER_BLOCK = LANES // GROUP_CH
SSM_HS = SSM_GROUPS_PER_BLOCK * STATE_P


def _params(sem):
    return pltpu.CompilerParams(dimension_semantics=sem, vmem_limit_bytes=VMEM_LIMIT)


def _gelu(x):
    return 0.5 * x * (1.0 + lax.erf(x * math.sqrt(0.5)))


def _dot(a, b):
    return jnp.dot(a, b, preferred_element_type=F32)


def _dot_nt(a, b):
    return lax.dot_general(a, b, (((1,), (1,)), ((), ())), preferred_element_type=F32)


def _ada_kernel(c_ref, w_ref, b_ref, o_ref):
    c = c_ref[...]
    a = (c * jax.nn.sigmoid(c)).astype(BF16)
    o_ref[...] = _dot(a, w_ref[...].astype(BF16)) + b_ref[...]


def _ada(c_all, w_ada, b_ada, tn=1024):
    depth, d, n = w_ada.shape
    rows = c_all.shape[0]
    return pl.pallas_call(
        _ada_kernel,
        grid=(depth, n // tn),
        in_specs=[
            pl.BlockSpec((rows, d), lambda l, j: (0, 0)),
            pl.BlockSpec((None, d, tn), lambda l, j: (l, 0, j)),
            pl.BlockSpec((None, 1, tn), lambda l, j: (l, 0, j)),
        ],
        out_specs=pl.BlockSpec((None, rows, tn), lambda l, j: (l, 0, j)),
        out_shape=jax.ShapeDtypeStruct((depth, rows, n), F32),
        compiler_params=_params(("parallel", "parallel")),
        name="ada",
    )(c_all, w_ada, b_ada.reshape(depth, 1, n))


def _modulate(x_ref, sc_ref, sh_ref):
    return (x_ref[...] * (1.0 + sc_ref[...]) + sh_ref[...]).astype(BF16)


def _modmm_kernel(x_ref, sc_ref, sh_ref, w_ref, o_ref, h_scr):
    @pl.when(pl.program_id(2) == 0)
    def _():
        h_scr[...] = _modulate(x_ref, sc_ref, sh_ref)

    o_ref[...] = _dot(h_scr[...], w_ref[...])


def _mod_spec(mod, tm):
    if mod.shape[1] == 1:
        return pl.BlockSpec((None, 1, mod.shape[2]), lambda b, i, n: (b, 0, 0))
    return pl.BlockSpec((None, tm, mod.shape[2]), lambda b, i, n: (b, i, 0))


def _ssm_in(x, sc, sh, w, tm, tn, time_major):
    bsz, L, d = x.shape
    n = w.shape[1]
    nn = n // tn
    if time_major:
        out_shape = jax.ShapeDtypeStruct((L, bsz * n), F32)
        out_spec = pl.BlockSpec((tm, tn), lambda b, i, j: (i, b * nn + j))
    else:
        out_shape = jax.ShapeDtypeStruct((bsz, L, n), F32)
        out_spec = pl.BlockSpec((None, tm, tn), lambda b, i, j: (b, i, j))
    return pl.pallas_call(
        _modmm_kernel,
        grid=(bsz, L // tm, nn),
        in_specs=[
            pl.BlockSpec((None, tm, d), lambda b, i, j: (b, i, 0)),
            _mod_spec(sc, tm),
            _mod_spec(sh, tm),
            pl.BlockSpec((d, tn), lambda b, i, j: (0, j)),
        ],
        out_specs=out_spec,
        out_shape=out_shape,
        scratch_shapes=[pltpu.VMEM((tm, d), BF16)],
        compiler_params=_params(("parallel", "parallel", "arbitrary")),
        name="ssm_in",
    )(x, sc, sh, w)


def _ssm_kernel(u_ref, bdb_ref, are_ref, aim_ref, bdc_ref, sre_ref, sim_ref,
                y_ref, ore_ref, oim_ref, x_scr, h_scr, *, tt):
    i = pl.program_id(2)
    hs = SSM_HS

    @pl.when(i == 0)
    def _():
        h_scr[:, :hs] = sre_ref[...]
        h_scr[:, hs:] = sim_ref[...]

    u = u_ref[...].reshape(tt * SUBLANES, LANES).astype(BF16)
    x_scr[...] = _dot(u, bdb_ref[...]).reshape(tt, SUBLANES, 2 * hs)

    ar = jnp.broadcast_to(are_ref[...], (SUBLANES, hs))
    ai = jnp.broadcast_to(aim_ref[...], (SUBLANES, hs))

    def step(t, carry):
        hr, hi = carry
        nr = (ar * hr - ai * hi) + x_scr[t, :, :hs]
        ni = (ar * hi + ai * hr) + x_scr[t, :, hs:]
        x_scr[t, :, :hs] = nr
        x_scr[t, :, hs:] = ni
        return nr, ni

    hr, hi = lax.fori_loop(0, tt, step, (h_scr[:, :hs], h_scr[:, hs:]), unroll=8)
    h_scr[:, :hs] = hr
    h_scr[:, hs:] = hi

    hb = x_scr[...].reshape(tt * SUBLANES, 2 * hs).astype(BF16)
    y_ref[...] = _dot(hb, bdc_ref[...]).reshape(tt, SUBLANES, LANES)

    @pl.when(i == pl.num_programs(2) - 1)
    def _():
        ore_ref[...] = hr
        oim_ref[...] = hi


def _ssm_scan(u_tb, bd_b, a_re, a_im, bd_c, s_re, s_im, tt):
    L, bsz, d = u_tb.shape
    nblk = d // LANES
    hs = SSM_HS
    st_spec = pl.BlockSpec((SUBLANES, hs), lambda j, g, i: (g, j))
    return pl.pallas_call(
        functools.partial(_ssm_kernel, tt=tt),
        grid=(nblk, bsz // SUBLANES, L // tt),
        in_specs=[
            pl.BlockSpec((tt, SUBLANES, LANES), lambda j, g, i: (i, g, j)),
            pl.BlockSpec((None, LANES, 2 * hs), lambda j, g, i: (j, 0, 0)),
            pl.BlockSpec((None, 1, hs), lambda j, g, i: (j, 0, 0)),
            pl.BlockSpec((None, 1, hs), lambda j, g, i: (j, 0, 0)),
            pl.BlockSpec((None, 2 * hs, LANES), lambda j, g, i: (j, 0, 0)),
            st_spec,
            st_spec,
        ],
        out_specs=[
            pl.BlockSpec((tt, SUBLANES, LANES), lambda j, g, i: (i, g, j)),
            st_spec,
            st_spec,
        ],
        out_shape=[
            jax.ShapeDtypeStruct((L, bsz, d), F32),
            jax.ShapeDtypeStruct(s_re.shape, F32),
            jax.ShapeDtypeStruct(s_im.shape, F32),
        ],
        scratch_shapes=[
            pltpu.VMEM((tt, SUBLANES, 2 * hs), F32),
            pltpu.VMEM((SUBLANES, 2 * hs), F32),
        ],
        compiler_params=_params(("parallel", "parallel", "arbitrary")),
        name="ssm_scan",
    )(u_tb, bd_b, a_re, a_im, bd_c, s_re, s_im)


def _layer_norm_chunks(acc_scr, lng_ref, lnb_ref, o_ref, nchunk, tn):
    d = nchunk * tn
    parts = [acc_scr[c] for c in range(nchunk)]
    mu = sum(jnp.sum(p, axis=-1, keepdims=True) for p in parts) * (1.0 / d)
    cen = [p - mu for p in parts]
    var = sum(jnp.sum(q * q, axis=-1, keepdims=True) for q in cen) * (1.0 / d)
    inv = lax.rsqrt(var + LN_EPS)
    for c in range(nchunk):
        sl = slice(c * tn, (c + 1) * tn)
        o_ref[:, sl] = cen[c] * inv * lng_ref[:, sl] + lnb_ref[:, sl]


def _glu_ln_kernel(y_ref, u_ref, d_ref, wa_ref, wg_ref, x_ref, gt_ref, lng_ref, lnb_ref,
                   o_ref, z_scr, acc_scr, *, nchunk, tn, alpha):
    c = pl.program_id(2)

    @pl.when(c == 0)
    def _():
        z_scr[...] = _gelu(y_ref[...] + d_ref[...] * u_ref[...]).astype(BF16)

    z = z_scr[...]
    a = _dot(z, wa_ref[...])
    g = _dot(z, wg_ref[...])
    m = a * jax.nn.sigmoid(g)
    acc_scr[c] = alpha * x_ref[...] + (1.0 + gt_ref[...]) * m

    @pl.when(c == nchunk - 1)
    def _():
        _layer_norm_chunks(acc_scr, lng_ref, lnb_ref, o_ref, nchunk, tn)


def _glu_ln(y, u, d_skip, w_glu, x, gate, ln_g, ln_b, tm, tn, alpha, time_major):
    bsz, L, d = x.shape
    nchunk = d // tn
    if time_major:
        yu_spec = pl.BlockSpec((tm, d), lambda b, i, c: (i, b))
    else:
        yu_spec = pl.BlockSpec((None, tm, d), lambda b, i, c: (b, i, 0))
    if gate.shape[1] == 1:
        gt_spec = pl.BlockSpec((None, 1, tn), lambda b, i, c: (b, 0, c))
    else:
        gt_spec = pl.BlockSpec((None, tm, tn), lambda b, i, c: (b, i, c))
    vec = pl.BlockSpec((1, d), lambda b, i, c: (0, 0))
    return pl.pallas_call(
        functools.partial(_glu_ln_kernel, nchunk=nchunk, tn=tn, alpha=alpha),
        grid=(bsz, L // tm, nchunk),
        in_specs=[
            yu_spec,
            yu_spec,
            vec,
            pl.BlockSpec((d, tn), lambda b, i, c: (0, c)),
            pl.BlockSpec((d, tn), lambda b, i, c: (0, nchunk + c)),
            pl.BlockSpec((None, tm, tn), lambda b, i, c: (b, i, c)),
            gt_spec,
            vec,
            vec,
        ],
        out_specs=pl.BlockSpec((None, tm, d), lambda b, i, c: (b, i, 0)),
        out_shape=jax.ShapeDtypeStruct((bsz, L, d), F32),
        scratch_shapes=[pltpu.VMEM((tm, d), BF16), pltpu.VMEM((nchunk, tm, tn), F32)],
        compiler_params=_params(("parallel", "parallel", "arbitrary")),
        name="glu_ln",
    )(y, u, d_skip, w_glu, w_glu, x, gate, ln_g, ln_b)


def _oproj_ln_kernel(o_in_ref, w_ref, x_ref, gt_ref, lng_ref, lnb_ref, o_ref, acc_scr,
                     *, nchunk, tn, alpha):
    c = pl.program_id(2)
    m = _dot(o_in_ref[...], w_ref[...])
    acc_scr[c] = alpha * x_ref[...] + (1.0 + gt_ref[...]) * m

    @pl.when(c == nchunk - 1)
    def _():
        _layer_norm_chunks(acc_scr, lng_ref, lnb_ref, o_ref, nchunk, tn)


def _oproj_ln(o_in, w_o, x, gate, ln_g, ln_b, tm, tn, alpha):
    bsz, L, d = x.shape
    nchunk = d // tn
    if gate.shape[1] == 1:
        gt_spec = pl.BlockSpec((None, 1, tn), lambda b, i, c: (b, 0, c))
    else:
        gt_spec = pl.BlockSpec((None, tm, tn), lambda b, i, c: (b, i, c))
    vec = pl.BlockSpec((1, d), lambda b, i, c: (0, 0))
    return pl.pallas_call(
        functools.partial(_oproj_ln_kernel, nchunk=nchunk, tn=tn, alpha=alpha),
        grid=(bsz, L // tm, nchunk),
        in_specs=[
            pl.BlockSpec((None, tm, o_in.shape[2]), lambda b, i, c: (b, i, 0)),
            pl.BlockSpec((w_o.shape[0], tn), lambda b, i, c: (0, c)),
            pl.BlockSpec((None, tm, tn), lambda b, i, c: (b, i, c)),
            gt_spec,
            vec,
            vec,
        ],
        out_specs=pl.BlockSpec((None, tm, d), lambda b, i, c: (b, i, 0)),
        out_shape=jax.ShapeDtypeStruct((bsz, L, d), F32),
        scratch_shapes=[pltpu.VMEM((nchunk, tm, tn), F32)],
        compiler_params=_params(("parallel", "parallel", "arbitrary")),
        name="oproj_ln",
    )(o_in, w_o, x, gate, ln_g, ln_b)


def _ffn_kernel(x_ref, sc_ref, sh_ref, gt_ref, wg_ref, wv_ref, wc_ref, bc_ref, wd_ref,
                st_ref, lng_ref, lnb_ref, o_ref, cv_ref, h_scr, acc_scr, prev_scr,
                *, nb, T, nchunk, alpha):
    i = pl.program_id(1)
    c = pl.program_id(2)
    M = nb * T
    fc = wg_ref.shape[1]

    @pl.when(c == 0)
    def _():
        h_scr[...] = _modulate(x_ref, sc_ref, sh_ref)

    @pl.when(i == 0)
    def _():
        prev_scr[...] = st_ref[c]

    @pl.when(i > 0)
    def _():
        prev_scr[...] = cv_ref[c]

    h = h_scr[...]
    g = _dot(h, wg_ref[...])
    v = _dot(h, wv_ref[...])

    row = lax.broadcasted_iota(jnp.int32, (M, 1), 0)
    if nb == 1:
        tpos = row
        p0 = prev_scr[0, 0:1, :]
        p1 = prev_scr[0, 1:2, :]
    else:
        tpos = lax.rem(row, T)
        p0 = jnp.broadcast_to(prev_scr[:, 0:1, :], (nb, T, fc)).reshape(M, fc)
        p1 = jnp.broadcast_to(prev_scr[:, 1:2, :], (nb, T, fc)).reshape(M, fc)
    s1 = jnp.where(tpos == 0, p1, pltpu.roll(g, 1, 0))
    s2 = jnp.where(tpos == 0, p0, jnp.where(tpos == 1, p1, pltpu.roll(g, 2, 0)))
    conv = bc_ref[...] + s2 * wc_ref[0:1, :] + s1 * wc_ref[1:2, :] + g * wc_ref[2:3, :]
    act = (_gelu(conv) * v).astype(BF16)
    contrib = _dot(act, wd_ref[...])

    @pl.when(c == 0)
    def _():
        acc_scr[...] = contrib

    @pl.when(c > 0)
    def _():
        acc_scr[...] += contrib

    cv_ref[c] = g.reshape(nb, T, fc)[:, T - 2:, :]

    @pl.when(c == nchunk - 1)
    def _():
        r = alpha * x_ref[...] + (1.0 + gt_ref[...]) * acc_scr[...]
        mu = jnp.mean(r, axis=-1, keepdims=True)
        cen = r - mu
        var = jnp.mean(cen * cen, axis=-1, keepdims=True)
        o_ref[...] = cen * lax.rsqrt(var + LN_EPS) * lng_ref[...] + lnb_ref[...]


def _ffn(x, sc, sh, gate, w_up, w_conv, b_conv, w_down, conv_state, ln_g, ln_b,
         nb, T, fc, alpha):
    S, R, d = x.shape
    dff = w_down.shape[0]
    nchunk = dff // fc
    tm = nb * T
    ntile = R // tm
    assert nb == 1 or ntile == 1

    def mod_spec(mod):
        if mod.shape[1] == 1:
            return pl.BlockSpec((None, 1, d), lambda s, i, c: (s, 0, 0))
        return pl.BlockSpec((None, tm, d), lambda s, i, c: (s, i, 0))

    vec = pl.BlockSpec((1, d), lambda s, i, c: (0, 0))
    st_spec = pl.BlockSpec((nchunk, nb, 2, fc), lambda s, i, c: (0, s, 0, 0))
    return pl.pallas_call(
        functools.partial(_ffn_kernel, nb=nb, T=T, nchunk=nchunk, alpha=alpha),
        grid=(S, ntile, nchunk),
        in_specs=[
            pl.BlockSpec((None, tm, d), lambda s, i, c: (s, i, 0)),
            mod_spec(sc),
            mod_spec(sh),
            mod_spec(gate),
            pl.BlockSpec((d, fc), lambda s, i, c: (0, c)),
            pl.BlockSpec((d, fc), lambda s, i, c: (0, nchunk + c)),
            pl.BlockSpec((w_conv.shape[0], fc), lambda s, i, c: (0, c)),
            pl.BlockSpec((1, fc), lambda s, i, c: (0, c)),
            pl.BlockSpec((fc, d), lambda s, i, c: (c, 0)),
            st_spec,
            vec,
            vec,
        ],
        out_specs=[
            pl.BlockSpec((None, tm, d), lambda s, i, c: (s, i, 0)),
            st_spec,
        ],
        out_shape=[
            jax.ShapeDtypeStruct((S, R, d), F32),
            jax.ShapeDtypeStruct(conv_state.shape, F32),
        ],
        scratch_shapes=[
            pltpu.VMEM((tm, d), BF16),
            pltpu.VMEM((tm, d), F32),
            pltpu.VMEM((nb, 2, fc), F32),
        ],
        compiler_params=_params(("parallel", "arbitrary", "arbitrary")),
        name="conv_ffn",
    )(x, sc, sh, gate, w_up, w_up, w_conv, b_conv, w_down, conv_state, ln_g, ln_b)


def _qkv_kernel(x_ref, sc_ref, sh_ref, w_ref, q_ref, k32_ref, v32_ref, k16_ref, v16_ref,
                h_scr, *, nq):
    n = pl.program_id(2)

    @pl.when(n == 0)
    def _():
        h_scr[...] = _modulate(x_ref, sc_ref, sh_ref)

    r = _dot(h_scr[...], w_ref[...])

    @pl.when(n < nq)
    def _():
        q_ref[...] = (r * (HEAD_DIM ** -0.5)).astype(BF16)

    @pl.when((n >= nq) & (n < 2 * nq))
    def _():
        k32_ref[...] = r
        k16_ref[...] = r.astype(BF16)

    @pl.when(n >= 2 * nq)
    def _():
        v32_ref[...] = r
        v16_ref[...] = r.astype(BF16)


def _qkv(x, sc, sh, w_qkv, tm, tn):
    bsz, L, d = x.shape
    n3 = w_qkv.shape[1]
    da = n3 // 3
    nq = da // tn

    def out_spec(which):
        return pl.BlockSpec(
            (None, tm, tn),
            lambda b, i, n: (b, i, jnp.clip(n - which * nq, 0, nq - 1)))

    return pl.pallas_call(
        functools.partial(_qkv_kernel, nq=nq),
        grid=(bsz, L // tm, 3 * nq),
        in_specs=[
            pl.BlockSpec((None, tm, d), lambda b, i, n: (b, i, 0)),
            _mod_spec(sc, tm),
            _mod_spec(sh, tm),
            pl.BlockSpec((d, tn), lambda b, i, n: (0, n)),
        ],
        out_specs=[out_spec(0), out_spec(1), out_spec(2), out_spec(1), out_spec(2)],
        out_shape=[
            jax.ShapeDtypeStruct((bsz, L, da), BF16),
            jax.ShapeDtypeStruct((bsz, L, da), F32),
            jax.ShapeDtypeStruct((bsz, L, da), F32),
            jax.ShapeDtypeStruct((bsz, L, da), BF16),
            jax.ShapeDtypeStruct((bsz, L, da), BF16),
        ],
        scratch_shapes=[pltpu.VMEM((tm, d), BF16)],
        compiler_params=_params(("parallel", "parallel", "arbitrary")),
        name="qkv",
    )(x, sc, sh, w_qkv)


def _chunk_id(pos):
    assert CHUNK & (CHUNK - 1) == 0
    return lax.shift_right_logical(pos, CHUNK.bit_length() - 1)


def _sub_ln(o, g_ref, lam_init):
    o = o * lax.rsqrt(jnp.mean(o * o, axis=-1, keepdims=True) + LN_EPS)
    return o * g_ref[...] * (1.0 - lam_init)


def _online_update(s, v, m_scr, l_scr, acc_scr, idx):
    m_old = m_scr[idx]
    m_new = jnp.maximum(m_old, jnp.max(s, axis=-1, keepdims=True))
    corr = jnp.exp(m_old - m_new)
    p = jnp.exp(s - m_new)
    l_scr[idx] = corr * l_scr[idx] + jnp.sum(p, axis=-1, keepdims=True)
    acc_scr[idx] = corr * acc_scr[idx] + _dot(p.astype(BF16), v)
    m_scr[idx] = m_new


def _attn_prompt_kernel(lam_ref, q_ref, k_ref, v_ref, g_ref, o_ref, m_scr, l_scr, acc_scr,
                        *, tq, tk, lam_init):
    i = pl.program_id(2)
    m_scr[...] = jnp.full(m_scr.shape, MASK_VALUE, F32)
    l_scr[...] = jnp.zeros(l_scr.shape, F32)
    acc_scr[...] = jnp.zeros(acc_scr.shape, F32)
    q1 = q_ref[:, :HEAD_DIM]
    q2 = q_ref[:, HEAD_DIM:]

    def kv_tile(j, masked):
        start = pl.multiple_of(j * tk, tk)
        k = k_ref[pl.ds(start, tk), :]
        v = v_ref[pl.ds(start, tk), :]
        s1 = _dot_nt(q1, k[:, :HEAD_DIM])
        s2 = _dot_nt(q2, k[:, HEAD_DIM:])
        if masked:
            qc = _chunk_id(i * tq + lax.broadcasted_iota(jnp.int32, (tq, tk), 0))
            kc = _chunk_id(j * tk + lax.broadcasted_iota(jnp.int32, (tq, tk), 1))
            keep = kc <= qc
            s1 = jnp.where(keep, s1, MASK_VALUE)
            s2 = jnp.where(keep, s2, MASK_VALUE)
        _online_update(s1, v, m_scr, l_scr, acc_scr, 0)
        _online_update(s2, v, m_scr, l_scr, acc_scr, 1)

    nfull = (i * tq) // tk

    def body(j, carry):
        kv_tile(j, False)
        return carry

    lax.fori_loop(0, nfull, body, 0)
    for d in range(tq // tk):
        kv_tile(nfull + d, True)

    lam = lam_ref[0, 0]
    o = acc_scr[0] / l_scr[0] - lam * (acc_scr[1] / l_scr[1])
    o_ref[...] = _sub_ln(o, g_ref, lam_init).astype(o_ref.dtype)


def _attn_prompt(lam, q, k, v, subln_g, tq, tk, lam_init):
    bsz, L, da = q.shape
    hw = 2 * HEAD_DIM
    nh = da // hw
    assert tq % tk == 0 and tk % CHUNK == 0
    return pl.pallas_call(
        functools.partial(_attn_prompt_kernel, tq=tq, tk=tk, lam_init=lam_init),
        grid=(bsz, nh, L // tq),
        in_specs=[
            pl.BlockSpec(memory_space=pltpu.SMEM),
            pl.BlockSpec((None, tq, hw), lambda b, h, i: (b, i, h)),
            pl.BlockSpec((None, L, hw), lambda b, h, i: (b, 0, h)),
            pl.BlockSpec((None, L, hw), lambda b, h, i: (b, 0, h)),
            pl.BlockSpec((1, hw), lambda b, h, i: (0, 0)),
        ],
        out_specs=pl.BlockSpec((None, tq, hw), lambda b, h, i: (b, i, h)),
        out_shape=jax.ShapeDtypeStruct((bsz, L, da), BF16),
        scratch_shapes=[
            pltpu.VMEM((2, tq, 1), F32),
            pltpu.VMEM((2, tq, 1), F32),
            pltpu.VMEM((2, tq, hw), F32),
        ],
        compiler_params=_params(("parallel", "parallel", "arbitrary")),
        name="attn_prompt",
    )(lam, q, k, v, subln_g)


def _attn_sample_kernel(lam_ref, q_ref, ck_ref, cv_ref, kn_ref, vn_ref, g_ref, o_ref,
                        *, P, T, lam_init):
    ck = ck_ref[...].astype(BF16)
    cv = cv_ref[...].astype(BF16)
    kn = kn_ref[...]
    vn = vn_ref[...]
    qc = _chunk_id(P + lax.broadcasted_iota(jnp.int32, (T, 1), 0))
    keep_c = _chunk_id(lax.broadcasted_iota(jnp.int32, (1, P), 1)) <= qc
    keep_n = _chunk_id(P + lax.broadcasted_iota(jnp.int32, (1, T), 1)) <= qc

    def one_map(q, kc, kx):
        sc = jnp.where(keep_c, _dot_nt(q, kc), MASK_VALUE)
        sn = jnp.where(keep_n, _dot_nt(q, kx), MASK_VALUE)
        m = jnp.maximum(jnp.max(sc, axis=-1, keepdims=True),
                        jnp.max(sn, axis=-1, keepdims=True))
        pc = jnp.exp(sc - m)
        pn = jnp.exp(sn - m)
        l = jnp.sum(pc, axis=-1, keepdims=True) + jnp.sum(pn, axis=-1, keepdims=True)
        return (_dot(pc.astype(BF16), cv) + _dot(pn.astype(BF16), vn)) / l

    o1 = one_map(q_ref[:, :HEAD_DIM], ck[:, :HEAD_DIM], kn[:, :HEAD_DIM])
    o2 = one_map(q_ref[:, HEAD_DIM:], ck[:, HEAD_DIM:], kn[:, HEAD_DIM:])
    o = o1 - lam_ref[0, 0] * o2
    o_ref[...] = _sub_ln(o, g_ref, lam_init).astype(o_ref.dtype)


def _attn_sample(lam, q, cache_k, cache_v, k_new, v_new, subln_g, lam_init):
    bsz, T, da = q.shape
    P = cache_k.shape[1]
    hw = 2 * HEAD_DIM
    nh = da // hw
    new_spec = pl.BlockSpec((None, T, hw), lambda b, h: (b, 0, h))
    cache_spec = pl.BlockSpec((None, P, hw), lambda b, h: (b, 0, h))
    return pl.pallas_call(
        functools.partial(_attn_sample_kernel, P=P, T=T, lam_init=lam_init),
        grid=(bsz, nh),
        in_specs=[
            pl.BlockSpec(memory_space=pltpu.SMEM),
            new_spec, cache_spec, cache_spec, new_spec, new_spec,
            pl.BlockSpec((1, hw), lambda b, h: (0, 0)),
        ],
        out_specs=new_spec,
        out_shape=jax.ShapeDtypeStruct((bsz, T, da), BF16),
        compiler_params=_params(("parallel", "parallel")),
        name="attn_sample",
    )(lam, q, cache_k, cache_v, k_new, v_new, subln_g)


def _ssm_discretise(lam_re, lam_im, log_step, b_re, b_im):
    lr = jnp.minimum(lam_re, -1e-4)
    li = lam_im
    dt = jnp.exp(log_step)[:, None]
    mag = jnp.exp(lr * dt)
    abar_re = mag * jnp.cos(li * dt)
    abar_im = mag * jnp.sin(li * dt)
    nr = abar_re - 1.0
    ni = abar_im
    den = lr * lr + li * li
    kr = (nr * lr + ni * li) / den
    ki = (ni * lr - nr * li) / den
    bbar_re = kr[..., None] * b_re - ki[..., None] * b_im
    bbar_im = kr[..., None] * b_im + ki[..., None] * b_re
    return abar_re, abar_im, bbar_re, bbar_im


def _block_diag(m):
    nblk, gb, r, c = m.shape
    eye = jnp.eye(gb, dtype=m.dtype)
    return jnp.einsum("jgrc,gh->jgrhc", m, eye).reshape(nblk, gb * r, gb * c)


def _ssm_matrices(lam_re, lam_im, log_step, b_re, b_im, c_re, c_im):
    G = lam_re.shape[0]
    gb = SSM_GROUPS_PER_BLOCK
    nblk = G // gb
    abar_re, abar_im, bbar_re, bbar_im = _ssm_discretise(lam_re, lam_im, log_step, b_re, b_im)
    bt_re = bbar_re.reshape(nblk, gb, STATE_P, GROUP_CH).swapaxes(2, 3)
    bt_im = bbar_im.reshape(nblk, gb, STATE_P, GROUP_CH).swapaxes(2, 3)
    bd_b = jnp.concatenate([_block_diag(bt_re), _block_diag(bt_im)], axis=2).astype(BF16)
    ct_re = c_re.reshape(nblk, gb, GROUP_CH, STATE_P).swapaxes(2, 3)
    ct_im = c_im.reshape(nblk, gb, GROUP_CH, STATE_P).swapaxes(2, 3)
    bd_c = jnp.concatenate([_block_diag(ct_re), _block_diag(-ct_im)], axis=1).astype(BF16)
    a_re = abar_re.reshape(nblk, 1, gb * STATE_P)
    a_im = abar_im.reshape(nblk, 1, gb * STATE_P)
    return bd_b, a_re, a_im, bd_c


def _chunk_state(state, fc):
    nbt, w, dff = state.shape
    return state.reshape(nbt, w, dff // fc, fc).transpose(2, 0, 1, 3)


def _unchunk_state(state):
    nchunk, nbt, w, fc = state.shape
    return state.transpose(1, 2, 0, 3).reshape(nbt, w, nchunk * fc)


TM_PROMPT = 512
TN = 512
FFN_CHUNK = 512
SSM_TT = 256
TQ = 256
TK = 256


def kernel(x_prompt, x_sample, c_prompt, c_sample, cache_k, cache_v, state_ssm_re, state_ssm_im, state_conv, w_ada, b_ada, ln_g, ln_b, w_up, w_dconv, b_dconv, w_down, w_ssm_in, ssm_lam_re, ssm_lam_im, ssm_log_step, ssm_b_re, ssm_b_im, ssm_c_re, ssm_c_im, ssm_d, w_glu, w_qkv, lam_q1, lam_k1, lam_q2, lam_k2, subln_g, w_o):
    depth = w_ada.shape[0]
    bp, L, d = x_prompt.shape
    bs, T, _ = x_sample.shape
    dff = w_down.shape[1]
    alpha = (2 * depth) ** 0.25
    rows_s = bs * T

    mods = _ada(jnp.concatenate([c_prompt, c_sample], axis=0), w_ada, b_ada)

    xp = x_prompt
    xs = x_sample.reshape(1, rows_s, d)
    zeros_conv = jnp.zeros((bp, state_conv.shape[2], dff), F32)
    outs = {name: [] for name in ("kp", "vp", "srp", "sip", "cvp", "ks", "vs", "srs", "sis", "cvs")}

    for i in range(depth):
        mp = [m.reshape(bp, 1, d) for m in jnp.split(mods[i, :bp], 6, axis=-1)]
        ms = [jnp.repeat(m, T, axis=0).reshape(1, rows_s, d)
              for m in jnp.split(mods[i, bp:], 6, axis=-1)]
        shp1, scp1, gtp1, shp2, scp2, gtp2 = mp
        shs1, scs1, gts1, shs2, scs2, gts2 = ms
        lng = ln_g[i].reshape(2, 1, d)
        lnb = ln_b[i].reshape(2, 1, d)
        j = i // 2
        if i % 2 == 0:
            bd_b, a_re, a_im, bd_c = _ssm_matrices(
                ssm_lam_re[j], ssm_lam_im[j], ssm_log_step[j], ssm_b_re[j], ssm_b_im[j],
                ssm_c_re[j], ssm_c_im[j])
            w_in = w_ssm_in[j].astype(BF16)
            wg = w_glu[j].astype(BF16)
            d_skip = ssm_d[j].reshape(1, d)
            gp = ssm_lam_re.shape[1] * ssm_lam_re.shape[2]
            up = _ssm_in(xp, scp1, shp1, w_in, TM_PROMPT, d, True).reshape(L, bp, d)
            zp = jnp.zeros((bp, gp), F32)
            yp, srp, sip = _ssm_scan(up, bd_b, a_re, a_im, bd_c, zp, zp, SSM_TT)
            xp = _glu_ln(yp.reshape(L, bp * d), up.reshape(L, bp * d), d_skip, wg, xp, gtp1,
                         lng[0], lnb[0], TM_PROMPT, TN, alpha, True)
            us = _ssm_in(xs, scs1, shs1, w_in, rows_s, d, False)
            us_tb = us.reshape(bs, T, d).transpose(1, 0, 2)
            ys_tb, srs, sis = _ssm_scan(us_tb, bd_b, a_re, a_im, bd_c,
                                        state_ssm_re[j].reshape(bs, gp),
                                        state_ssm_im[j].reshape(bs, gp), T)
            ys = ys_tb.transpose(1, 0, 2).reshape(1, rows_s, d)
            xs = _glu_ln(ys, us, d_skip, wg, xs, gts1, lng[0], lnb[0], rows_s, TN, alpha, False)
            st_shape = ssm_lam_re.shape[1:]
            outs["srp"].append(srp.reshape((bp,) + st_shape))
            outs["sip"].append(sip.reshape((bp,) + st_shape))
            outs["srs"].append(srs.reshape((bs,) + st_shape))
            outs["sis"].append(sis.reshape((bs,) + st_shape))
        else:
            lam_init = 0.8 - 0.6 * math.exp(-0.3 * i)
            lam = (jnp.exp(jnp.sum(lam_q1[j] * lam_k1[j])) - jnp.exp(jnp.sum(lam_q2[j] * lam_k2[j]))
                   + lam_init).reshape(1, 1)
            wqkv = w_qkv[j].astype(BF16)
            wo = w_o[j].astype(BF16)
            sg = subln_g[j].reshape(1, 2 * HEAD_DIM)
            nsub = cache_k.shape[3]
            nhead = cache_v.shape[3]
            qp, kp, vp, kp16, vp16 = _qkv(xp, scp1, shp1, wqkv, TM_PROMPT, TN)
            op = _attn_prompt(lam, qp, kp16, vp16, sg, TQ, TK, lam_init)
            xp = _oproj_ln(op, wo, xp, gtp1, lng[0], lnb[0], TM_PROMPT, TN, alpha)
            qs, ksn, vsn, ks16, vs16 = _qkv(xs, scs1, shs1, wqkv, rows_s, TN)
            P = cache_k.shape[2]
            da = wo.shape[0]
            osm = _attn_sample(lam, qs.reshape(bs, T, da),
                               cache_k[j].reshape(bs, P, da), cache_v[j].reshape(bs, P, da),
                               ks16.reshape(bs, T, da), vs16.reshape(bs, T, da), sg, lam_init)
            xs = _oproj_ln(osm.reshape(1, rows_s, da), wo, xs, gts1, lng[0], lnb[0],
                           rows_s, TN, alpha)
            outs["kp"].append(kp.reshape(bp, L, nsub, HEAD_DIM))
            outs["vp"].append(vp.reshape(bp, L, nhead, 2 * HEAD_DIM))
            outs["ks"].append(ksn.reshape(bs, T, nsub, HEAD_DIM))
            outs["vs"].append(vsn.reshape(bs, T, nhead, 2 * HEAD_DIM))

        wup = w_up[i].astype(BF16)
        wdn = w_down[i].astype(BF16)
        bconv = b_dconv[i].reshape(1, dff)
        xp, cvp = _ffn(xp, scp2, shp2, gtp2, wup, w_dconv[i], bconv, wdn,
                       _chunk_state(zeros_conv, FFN_CHUNK), lng[1], lnb[1],
                       1, TM_PROMPT, FFN_CHUNK, alpha)
        xs, cvs = _ffn(xs, scs2, shs2, gts2, wup, w_dconv[i], bconv, wdn,
                       _chunk_state(state_conv[i], FFN_CHUNK), lng[1], lnb[1],
                       bs, T, FFN_CHUNK, alpha)
        outs["cvp"].append(_unchunk_state(cvp))
        outs["cvs"].append(_unchunk_state(cvs))

    return (xp, xs.reshape(bs, T, d),
            jnp.stack(outs["kp"]), jnp.stack(outs["vp"]),
            jnp.stack(outs["srp"]), jnp.stack(outs["sip"]), jnp.stack(outs["cvp"]),
            jnp.stack(outs["ks"]), jnp.stack(outs["vs"]),
            jnp.stack(outs["srs"]), jnp.stack(outs["sis"]), jnp.stack(outs["cvs"]))
```

```python
import functools
import math

import jax
import jax.numpy as jnp
from jax import lax
from jax.experimental import pallas as pl
from jax.experimental.pallas import tpu as pltpu

F32 = jnp.float32
BF16 = jnp.bfloat16

CHUNK = 64
HEAD_DIM = 128
GROUP_CH = 16
STATE_P = 64
LN_EPS = 1e-5
MASK_VALUE = -1e30
Q_SCALE = math.log2(math.e) * HEAD_DIM ** -0.5

SUBLANES = 8
LANES = 128
VMEM_LIMIT = 56 * 1024 * 1024

SSM_GROUPS_PER_BLOCK = LANES // GROUP_CH
SSM_HS = SSM_GROUPS_PER_BLOCK * STATE_P


def _params(sem):
    return pltpu.CompilerParams(dimension_semantics=sem, vmem_limit_bytes=VMEM_LIMIT)


def _gelu(x):
    return 0.5 * x * (1.0 + lax.erf(x * math.sqrt(0.5)))


def _dot(a, b):
    return jnp.dot(a, b, preferred_element_type=F32)


def _dot_nt(a, b):
    return lax.dot_general(a, b, (((1,), (1,)), ((), ())), preferred_element_type=F32)


def _ada_kernel(c_ref, w_ref, b_ref, o_ref):
    c = c_ref[...]
    a = (c * jax.nn.sigmoid(c)).astype(BF16)
    o_ref[...] = _dot(a, w_ref[...].astype(BF16)) + b_ref[...]


def _ada(c_all, w_ada, b_ada, tn=1024):
    depth, d, n = w_ada.shape
    rows = c_all.shape[0]
    return pl.pallas_call(
        _ada_kernel,
        grid=(depth, n // tn),
        in_specs=[
            pl.BlockSpec((rows, d), lambda l, j: (0, 0)),
            pl.BlockSpec((None, d, tn), lambda l, j: (l, 0, j)),
            pl.BlockSpec((None, 1, tn), lambda l, j: (l, 0, j)),
        ],
        out_specs=pl.BlockSpec((None, rows, tn), lambda l, j: (l, 0, j)),
        out_shape=jax.ShapeDtypeStruct((depth, rows, n), F32),
        compiler_params=_params(("parallel", "parallel")),
        name="ada",
    )(c_all, w_ada, b_ada.reshape(depth, 1, n))


def _modulate(x_ref, sc_ref, sh_ref):
    return (x_ref[...] * (1.0 + sc_ref[...]) + sh_ref[...]).astype(BF16)


def _modmm_kernel(x_ref, sc_ref, sh_ref, w_ref, o_ref, h_scr):
    @pl.when(pl.program_id(2) == 0)
    def _():
        h_scr[...] = _modulate(x_ref, sc_ref, sh_ref)

    o_ref[...] = _dot(h_scr[...], w_ref[...])


def _mod_spec(mod, tm):
    if mod.shape[1] == 1:
        return pl.BlockSpec((None, 1, mod.shape[2]), lambda b, i, n: (b, 0, 0))
    return pl.BlockSpec((None, tm, mod.shape[2]), lambda b, i, n: (b, i, 0))


def _ssm_in(x, sc, sh, w, layer, tm, tn, time_major):
    bsz, L, d = x.shape
    n = w.shape[2]
    nn = n // tn
    if time_major:
        out_shape = jax.ShapeDtypeStruct((L, bsz * n), F32)
        out_spec = pl.BlockSpec((tm, tn), lambda b, i, j: (i, b * nn + j))
    else:
        out_shape = jax.ShapeDtypeStruct((bsz, L, n), F32)
        out_spec = pl.BlockSpec((None, tm, tn), lambda b, i, j: (b, i, j))
    return pl.pallas_call(
        _modmm_kernel,
        grid=(bsz, L // tm, nn),
        in_specs=[
            pl.BlockSpec((None, tm, d), lambda b, i, j: (b, i, 0)),
            _mod_spec(sc, tm),
            _mod_spec(sh, tm),
            pl.BlockSpec((None, d, tn), lambda b, i, j: (layer, 0, j)),
        ],
        out_specs=out_spec,
        out_shape=out_shape,
        scratch_shapes=[pltpu.VMEM((tm, d), BF16)],
        compiler_params=_params(("parallel", "parallel", "arbitrary")),
        name="ssm_in",
    )(x, sc, sh, w)


def _ssm_kernel(u_ref, bdb_ref, are_ref, aim_ref, bdc_ref, sre_ref, sim_ref,
                y_ref, ore_ref, oim_ref, x_scr, h_scr, *, tt):
    i = pl.program_id(2)
    hs = SSM_HS

    @pl.when(i == 0)
    def _():
        h_scr[:, :hs] = sre_ref[...]
        h_scr[:, hs:] = sim_ref[...]

    u = u_ref[...].reshape(tt * SUBLANES, LANES).astype(BF16)
    x_scr[...] = _dot(u, bdb_ref[...]).reshape(tt, SUBLANES, 2 * hs)

    ar = jnp.broadcast_to(are_ref[...], (SUBLANES, hs))
    ai = jnp.broadcast_to(aim_ref[...], (SUBLANES, hs))

    def step(t, carry):
        hr, hi = carry
        nr = (ar * hr - ai * hi) + x_scr[t, :, :hs]
        ni = (ar * hi + ai * hr) + x_scr[t, :, hs:]
        x_scr[t, :, :hs] = nr
        x_scr[t, :, hs:] = ni
        return nr, ni

    hr, hi = lax.fori_loop(0, tt, step, (h_scr[:, :hs], h_scr[:, hs:]), unroll=8)
    h_scr[:, :hs] = hr
    h_scr[:, hs:] = hi

    hb = x_scr[...].reshape(tt * SUBLANES, 2 * hs).astype(BF16)
    y_ref[...] = _dot(hb, bdc_ref[...]).reshape(tt, SUBLANES, LANES)

    @pl.when(i == pl.num_programs(2) - 1)
    def _():
        ore_ref[...] = hr
        oim_ref[...] = hi


def _ssm_scan(u_tb, bd_b, a_re, a_im, bd_c, s_re, s_im, tt):
    L, bsz, d = u_tb.shape
    nblk = d // LANES
    hs = SSM_HS
    st_spec = pl.BlockSpec((SUBLANES, hs), lambda j, g, i: (g, j))
    return pl.pallas_call(
        functools.partial(_ssm_kernel, tt=tt),
        grid=(nblk, bsz // SUBLANES, L // tt),
        in_specs=[
            pl.BlockSpec((tt, SUBLANES, LANES), lambda j, g, i: (i, g, j)),
            pl.BlockSpec((None, LANES, 2 * hs), lambda j, g, i: (j, 0, 0)),
            pl.BlockSpec((None, 1, hs), lambda j, g, i: (j, 0, 0)),
            pl.BlockSpec((None, 1, hs), lambda j, g, i: (j, 0, 0)),
            pl.BlockSpec((None, 2 * hs, LANES), lambda j, g, i: (j, 0, 0)),
            st_spec,
            st_spec,
        ],
        out_specs=[
            pl.BlockSpec((tt, SUBLANES, LANES), lambda j, g, i: (i, g, j)),
            st_spec,
            st_spec,
        ],
        out_shape=[
            jax.ShapeDtypeStruct((L, bsz, d), F32),
            jax.ShapeDtypeStruct(s_re.shape, F32),
            jax.ShapeDtypeStruct(s_im.shape, F32),
        ],
        scratch_shapes=[
            pltpu.VMEM((tt, SUBLANES, 2 * hs), F32),
            pltpu.VMEM((SUBLANES, 2 * hs), F32),
        ],
        compiler_params=_params(("parallel", "parallel", "arbitrary")),
        name="ssm_scan",
    )(u_tb, bd_b, a_re, a_im, bd_c, s_re, s_im)


def _layer_norm_chunks(acc_scr, lng_ref, lnb_ref, o_ref, nchunk, tn):
    d = nchunk * tn
    parts = [acc_scr[c] for c in range(nchunk)]
    mu = sum(jnp.sum(p, axis=-1, keepdims=True) for p in parts) * (1.0 / d)
    cen = [p - mu for p in parts]
    var = sum(jnp.sum(q * q, axis=-1, keepdims=True) for q in cen) * (1.0 / d)
    inv = lax.rsqrt(var + LN_EPS)
    for c in range(nchunk):
        sl = slice(c * tn, (c + 1) * tn)
        o_ref[:, sl] = cen[c] * inv * lng_ref[:, sl] + lnb_ref[:, sl]


def _glu_ln_kernel(y_ref, u_ref, d_ref, wa_ref, wg_ref, x_ref, gt_ref, lng_ref, lnb_ref,
                   o_ref, z_scr, acc_scr, *, nchunk, tn, alpha):
    c = pl.program_id(2)

    @pl.when(c == 0)
    def _():
        z_scr[...] = _gelu(y_ref[...] + d_ref[...] * u_ref[...]).astype(BF16)

    z = z_scr[...]
    a = _dot(z, wa_ref[...])
    g = _dot(z, wg_ref[...])
    m = a * jax.nn.sigmoid(g)
    acc_scr[c] = alpha * x_ref[...] + (1.0 + gt_ref[...]) * m

    @pl.when(c == nchunk - 1)
    def _():
        _layer_norm_chunks(acc_scr, lng_ref, lnb_ref, o_ref, nchunk, tn)


def _glu_ln(y, u, d_skip, w_glu, layer, x, gate, ln_g, ln_b, tm, tn, alpha, time_major):
    bsz, L, d = x.shape
    nchunk = d // tn
    if time_major:
        yu_spec = pl.BlockSpec((tm, d), lambda b, i, c: (i, b))
    else:
        yu_spec = pl.BlockSpec((None, tm, d), lambda b, i, c: (b, i, 0))
    if gate.shape[1] == 1:
        gt_spec = pl.BlockSpec((None, 1, tn), lambda b, i, c: (b, 0, c))
    else:
        gt_spec = pl.BlockSpec((None, tm, tn), lambda b, i, c: (b, i, c))
    vec = pl.BlockSpec((1, d), lambda b, i, c: (0, 0))
    return pl.pallas_call(
        functools.partial(_glu_ln_kernel, nchunk=nchunk, tn=tn, alpha=alpha),
        grid=(bsz, L // tm, nchunk),
        in_specs=[
            yu_spec,
            yu_spec,
            vec,
            pl.BlockSpec((None, d, tn), lambda b, i, c: (layer, 0, c)),
            pl.BlockSpec((None, d, tn), lambda b, i, c: (layer, 0, nchunk + c)),
            pl.BlockSpec((None, tm, tn), lambda b, i, c: (b, i, c)),
            gt_spec,
            vec,
            vec,
        ],
        out_specs=pl.BlockSpec((None, tm, d), lambda b, i, c: (b, i, 0)),
        out_shape=jax.ShapeDtypeStruct((bsz, L, d), F32),
        scratch_shapes=[pltpu.VMEM((tm, d), BF16), pltpu.VMEM((nchunk, tm, tn), F32)],
        compiler_params=_params(("parallel", "parallel", "arbitrary")),
        name="glu_ln",
    )(y, u, d_skip, w_glu, w_glu, x, gate, ln_g, ln_b)


def _oproj_ln_kernel(o_in_ref, w_ref, x_ref, gt_ref, lng_ref, lnb_ref, o_ref, acc_scr,
                     *, nchunk, tn, alpha):
    c = pl.program_id(2)
    m = _dot(o_in_ref[...], w_ref[...])
    acc_scr[c] = alpha * x_ref[...] + (1.0 + gt_ref[...]) * m

    @pl.when(c == nchunk - 1)
    def _():
        _layer_norm_chunks(acc_scr, lng_ref, lnb_ref, o_ref, nchunk, tn)


def _oproj_ln(o_in, w_o, layer, x, gate, ln_g, ln_b, tm, tn, alpha):
    bsz, L, d = x.shape
    nchunk = d // tn
    if gate.shape[1] == 1:
        gt_spec = pl.BlockSpec((None, 1, tn), lambda b, i, c: (b, 0, c))
    else:
        gt_spec = pl.BlockSpec((None, tm, tn), lambda b, i, c: (b, i, c))
    vec = pl.BlockSpec((1, d), lambda b, i, c: (0, 0))
    return pl.pallas_call(
        functools.partial(_oproj_ln_kernel, nchunk=nchunk, tn=tn, alpha=alpha),
        grid=(bsz, L // tm, nchunk),
        in_specs=[
            pl.BlockSpec((None, tm, o_in.shape[2]), lambda b, i, c: (b, i, 0)),
            pl.BlockSpec((None, w_o.shape[1], tn), lambda b, i, c: (layer, 0, c)),
            pl.BlockSpec((None, tm, tn), lambda b, i, c: (b, i, c)),
            gt_spec,
            vec,
            vec,
        ],
        out_specs=pl.BlockSpec((None, tm, d), lambda b, i, c: (b, i, 0)),
        out_shape=jax.ShapeDtypeStruct((bsz, L, d), F32),
        scratch_shapes=[pltpu.VMEM((nchunk, tm, tn), F32)],
        compiler_params=_params(("parallel", "parallel", "arbitrary")),
        name="oproj_ln",
    )(o_in, w_o, x, gate, ln_g, ln_b)


def _ffn_kernel(x_ref, sc_ref, sh_ref, gt_ref, wg_ref, wv_ref, wc_ref, bc_ref, wd_ref,
                st_ref, lng_ref, lnb_ref, o_ref, cv_ref, h_scr, acc_scr, prev_scr,
                *, nb, T, nchunk, alpha):
    i = pl.program_id(1)
    c = pl.program_id(2)
    M = nb * T
    fc = wg_ref.shape[1]

    @pl.when(c == 0)
    def _():
        h_scr[...] = _modulate(x_ref, sc_ref, sh_ref)
        acc_scr[...] = jnp.zeros(acc_scr.shape, F32)

    @pl.when(i == 0)
    def _():
        prev_scr[...] = st_ref[c]

    @pl.when(i > 0)
    def _():
        prev_scr[...] = cv_ref[c]

    h = h_scr[...]
    g = _dot(h, wg_ref[...])
    v = _dot(h, wv_ref[...])

    row = lax.broadcasted_iota(jnp.int32, (M, 1), 0)
    if nb == 1:
        tpos = row
        p0 = prev_scr[0, 0:1, :]
        p1 = prev_scr[0, 1:2, :]
    else:
        tpos = lax.rem(row, T)
        p0 = jnp.broadcast_to(prev_scr[:, 0:1, :], (nb, T, fc)).reshape(M, fc)
        p1 = jnp.broadcast_to(prev_scr[:, 1:2, :], (nb, T, fc)).reshape(M, fc)
    s1 = jnp.where(tpos == 0, p1, pltpu.roll(g, 1, 0))
    s2 = jnp.where(tpos == 0, p0, jnp.where(tpos == 1, p1, pltpu.roll(g, 2, 0)))
    conv = bc_ref[...] + s2 * wc_ref[0:1, :] + s1 * wc_ref[1:2, :] + g * wc_ref[2:3, :]
    act = (_gelu(conv) * v).astype(BF16)
    acc_scr[...] += _dot(act, wd_ref[...])
    cv_ref[c] = g.reshape(nb, T, fc)[:, T - 2:, :]

    @pl.when(c == nchunk - 1)
    def _():
        r = alpha * x_ref[...] + (1.0 + gt_ref[...]) * acc_scr[...]
        mu = jnp.mean(r, axis=-1, keepdims=True)
        cen = r - mu
        var = jnp.mean(cen * cen, axis=-1, keepdims=True)
        o_ref[...] = cen * lax.rsqrt(var + LN_EPS) * lng_ref[...] + lnb_ref[...]


def _ffn(x, sc, sh, gate, w_up, w_conv, b_conv, w_down, layer, conv_state, ln_g, ln_b,
         nb, T, fc, alpha):
    S, R, d = x.shape
    dff = w_down.shape[1]
    nchunk = dff // fc
    tm = nb * T
    ntile = R // tm
    assert nb == 1 or ntile == 1

    def mod_spec(mod):
        if mod.shape[1] == 1:
            return pl.BlockSpec((None, 1, d), lambda s, i, c: (s, 0, 0))
        return pl.BlockSpec((None, tm, d), lambda s, i, c: (s, i, 0))

    vec = pl.BlockSpec((1, d), lambda s, i, c: (0, 0))
    st_spec = pl.BlockSpec((nchunk, nb, 2, fc), lambda s, i, c: (0, s, 0, 0))
    return pl.pallas_call(
        functools.partial(_ffn_kernel, nb=nb, T=T, nchunk=nchunk, alpha=alpha),
        grid=(S, ntile, nchunk),
        in_specs=[
            pl.BlockSpec((None, tm, d), lambda s, i, c: (s, i, 0)),
            mod_spec(sc),
            mod_spec(sh),
            mod_spec(gate),
            pl.BlockSpec((None, d, fc), lambda s, i, c: (layer, 0, c)),
            pl.BlockSpec((None, d, fc), lambda s, i, c: (layer, 0, nchunk + c)),
            pl.BlockSpec((w_conv.shape[0], fc), lambda s, i, c: (0, c)),
            pl.BlockSpec((1, fc), lambda s, i, c: (0, c)),
            pl.BlockSpec((None, fc, d), lambda s, i, c: (layer, c, 0)),
            st_spec,
            vec,
            vec,
        ],
        out_specs=[
            pl.BlockSpec((None, tm, d), lambda s, i, c: (s, i, 0)),
            st_spec,
        ],
        out_shape=[
            jax.ShapeDtypeStruct((S, R, d), F32),
            jax.ShapeDtypeStruct(conv_state.shape, F32),
        ],
        scratch_shapes=[
            pltpu.VMEM((tm, d), BF16),
            pltpu.VMEM((tm, d), F32),
            pltpu.VMEM((nb, 2, fc), F32),
        ],
        compiler_params=_params(("parallel", "arbitrary", "arbitrary")),
        name="conv_ffn",
    )(x, sc, sh, gate, w_up, w_up, w_conv, b_conv, w_down, conv_state, ln_g, ln_b)


def _qkv_kernel(x_ref, sc_ref, sh_ref, w_ref, q_ref, k32_ref, v32_ref, k16_ref, v16_ref,
                h_scr, *, nq):
    n = pl.program_id(2)

    @pl.when(n == 0)
    def _():
        h_scr[...] = _modulate(x_ref, sc_ref, sh_ref)

    r = _dot(h_scr[...], w_ref[...])

    @pl.when(n < nq)
    def _():
        q_ref[...] = (r * Q_SCALE).astype(BF16)

    @pl.when((n >= nq) & (n < 2 * nq))
    def _():
        k32_ref[...] = r
        k16_ref[...] = r.astype(BF16)

    @pl.when(n >= 2 * nq)
    def _():
        v32_ref[...] = r
        v16_ref[...] = r.astype(BF16)


def _qkv(x, sc, sh, w_qkv, layer, tm, tn):
    bsz, L, d = x.shape
    n3 = w_qkv.shape[2]
    da = n3 // 3
    nq = da // tn

    def out_spec(which):
        return pl.BlockSpec(
            (None, tm, tn),
            lambda b, i, n: (b, i, jnp.clip(n - which * nq, 0, nq - 1)))

    return pl.pallas_call(
        functools.partial(_qkv_kernel, nq=nq),
        grid=(bsz, L // tm, 3 * nq),
        in_specs=[
            pl.BlockSpec((None, tm, d), lambda b, i, n: (b, i, 0)),
            _mod_spec(sc, tm),
            _mod_spec(sh, tm),
            pl.BlockSpec((None, d, tn), lambda b, i, n: (layer, 0, n)),
        ],
        out_specs=[out_spec(0), out_spec(1), out_spec(2), out_spec(1), out_spec(2)],
        out_shape=[
            jax.ShapeDtypeStruct((bsz, L, da), BF16),
            jax.ShapeDtypeStruct((bsz, L, da), F32),
            jax.ShapeDtypeStruct((bsz, L, da), F32),
            jax.ShapeDtypeStruct((bsz, L, da), BF16),
            jax.ShapeDtypeStruct((bsz, L, da), BF16),
        ],
        scratch_shapes=[pltpu.VMEM((tm, d), BF16)],
        compiler_params=_params(("parallel", "parallel", "arbitrary")),
        name="qkv",
    )(x, sc, sh, w_qkv)


def _chunk_id(pos):
    assert CHUNK & (CHUNK - 1) == 0
    return lax.shift_right_logical(pos, CHUNK.bit_length() - 1)


def _sub_ln(o, g_ref, lam_init):
    o = o * lax.rsqrt(jnp.mean(o * o, axis=-1, keepdims=True) + LN_EPS)
    return o * g_ref[...] * (1.0 - lam_init)


def _lane_tile(x, width):
    return jnp.tile(x, (1, width // LANES))


def _online_update(s, v, m_scr, l_scr, acc_scr, idx):
    m_old = m_scr[idx]
    m_new = jnp.maximum(m_old, jnp.max(s, axis=-1, keepdims=True))
    corr = jnp.exp2(m_old - m_new)
    p = jnp.exp2(s - _lane_tile(m_new, s.shape[1]))
    l_scr[idx] = corr * l_scr[idx] + jnp.sum(p, axis=-1, keepdims=True)
    acc_scr[idx] = _lane_tile(corr, v.shape[1]) * acc_scr[idx] + _dot(p.astype(BF16), v)
    m_scr[idx] = m_new


def _attn_prompt_kernel(lam_ref, q_ref, k_ref, v_ref, g_ref, o_ref, m_scr, l_scr, acc_scr,
                        *, tq, sb, lam_init):
    i = pl.program_id(2)
    maps = (slice(0, HEAD_DIM), slice(HEAD_DIM, 2 * HEAD_DIM))
    base = pl.multiple_of(i * tq, tq)

    row_chunk = _chunk_id(lax.broadcasted_iota(jnp.int32, (sb, sb), 0))
    col_chunk = _chunk_id(lax.broadcasted_iota(jnp.int32, (sb, sb), 1))
    diag_keep = col_chunk <= row_chunk
    for r in range(tq // sb):
        rows = slice(r * sb, (r + 1) * sb)
        nk = (r + 1) * sb
        v = v_ref[pl.ds(base, nk), :]
        for idx, cols in enumerate(maps):
            s = _dot_nt(q_ref[rows, cols], k_ref[pl.ds(base, nk), cols])
            s_diag = jnp.where(diag_keep, s[:, r * sb:], MASK_VALUE)
            s = s_diag if r == 0 else jnp.concatenate([s[:, :r * sb], s_diag], axis=1)
            m = jnp.max(s, axis=-1, keepdims=True)
            p = jnp.exp2(s - m)
            m_scr[idx, rows] = jnp.broadcast_to(m, (sb, LANES))
            l_scr[idx, rows] = jnp.broadcast_to(jnp.sum(p, axis=-1, keepdims=True), (sb, LANES))
            acc_scr[idx, rows] = _dot(p.astype(BF16), v)

    def body(j, carry):
        start = pl.multiple_of(j * tq, tq)
        v = v_ref[pl.ds(start, tq), :]
        s = [_dot_nt(q_ref[:, cols], k_ref[pl.ds(start, tq), cols]) for cols in maps]
        for idx in range(len(maps)):
            _online_update(s[idx], v, m_scr, l_scr, acc_scr, idx)
        return carry

    lax.fori_loop(0, i, body, 0)

    lam = lam_ref[0, 0]
    hw = acc_scr.shape[2]
    o = (acc_scr[0] * _lane_tile(1.0 / l_scr[0], hw)
         - acc_scr[1] * _lane_tile(lam / l_scr[1], hw))
    o_ref[...] = _sub_ln(o, g_ref, lam_init).astype(o_ref.dtype)


def _attn_prompt(lam, q, k, v, subln_g, tq, sb, lam_init):
    bsz, L, da = q.shape
    hw = 2 * HEAD_DIM
    nh = da // hw
    assert tq % sb == 0 and sb % CHUNK == 0
    return pl.pallas_call(
        functools.partial(_attn_prompt_kernel, tq=tq, sb=sb, lam_init=lam_init),
        grid=(bsz, nh, L // tq),
        in_specs=[
            pl.BlockSpec(memory_space=pltpu.SMEM),
            pl.BlockSpec((None, tq, hw), lambda b, h, i: (b, i, h)),
            pl.BlockSpec((None, L, hw), lambda b, h, i: (b, 0, h)),
            pl.BlockSpec((None, L, hw), lambda b, h, i: (b, 0, h)),
            pl.BlockSpec((1, hw), lambda b, h, i: (0, 0)),
        ],
        out_specs=pl.BlockSpec((None, tq, hw), lambda b, h, i: (b, i, h)),
        out_shape=jax.ShapeDtypeStruct((bsz, L, da), BF16),
        scratch_shapes=[
            pltpu.VMEM((2, tq, LANES), F32),
            pltpu.VMEM((2, tq, LANES), F32),
            pltpu.VMEM((2, tq, hw), F32),
        ],
        compiler_params=_params(("parallel", "parallel", "arbitrary")),
        name="attn_prompt",
    )(lam, q, k, v, subln_g)


def _attn_sample_kernel(lam_ref, q_ref, ck_ref, cv_ref, kn_ref, vn_ref, g_ref, o_ref,
                        *, P, T, lam_init):
    ck = ck_ref[...].astype(BF16)
    cv = cv_ref[...].astype(BF16)
    kn = kn_ref[...]
    vn = vn_ref[...]
    qc = _chunk_id(P + lax.broadcasted_iota(jnp.int32, (T, 1), 0))
    keep_c = _chunk_id(lax.broadcasted_iota(jnp.int32, (1, P), 1)) <= qc
    keep_n = _chunk_id(P + lax.broadcasted_iota(jnp.int32, (1, T), 1)) <= qc

    def one_map(q, kc, kx):
        sc = jnp.where(keep_c, _dot_nt(q, kc), MASK_VALUE)
        sn = jnp.where(keep_n, _dot_nt(q, kx), MASK_VALUE)
        m = jnp.maximum(jnp.max(sc, axis=-1, keepdims=True),
                        jnp.max(sn, axis=-1, keepdims=True))
        pc = jnp.exp2(sc - m)
        pn = jnp.exp2(sn - m)
        l = jnp.sum(pc, axis=-1, keepdims=True) + jnp.sum(pn, axis=-1, keepdims=True)
        return (_dot(pc.astype(BF16), cv) + _dot(pn.astype(BF16), vn)) / l

    o1 = one_map(q_ref[:, :HEAD_DIM], ck[:, :HEAD_DIM], kn[:, :HEAD_DIM])
    o2 = one_map(q_ref[:, HEAD_DIM:], ck[:, HEAD_DIM:], kn[:, HEAD_DIM:])
    o = o1 - lam_ref[0, 0] * o2
    o_ref[...] = _sub_ln(o, g_ref, lam_init).astype(o_ref.dtype)


def _attn_sample(lam, q, cache_k, cache_v, layer, k_new, v_new, subln_g, lam_init):
    bsz, T, da = q.shape
    P = cache_k.shape[2]
    hw = 2 * HEAD_DIM
    nh = da // hw
    new_spec = pl.BlockSpec((None, T, hw), lambda b, h: (b, 0, h))
    cache_spec = pl.BlockSpec((None, None, P, hw), lambda b, h: (layer, b, 0, h))
    return pl.pallas_call(
        functools.partial(_attn_sample_kernel, P=P, T=T, lam_init=lam_init),
        grid=(bsz, nh),
        in_specs=[
            pl.BlockSpec(memory_space=pltpu.SMEM),
            new_spec, cache_spec, cache_spec, new_spec, new_spec,
            pl.BlockSpec((1, hw), lambda b, h: (0, 0)),
        ],
        out_specs=new_spec,
        out_shape=jax.ShapeDtypeStruct((bsz, T, da), BF16),
        compiler_params=_params(("parallel", "parallel")),
        name="attn_sample",
    )(lam, q, cache_k, cache_v, k_new, v_new, subln_g)


def _ssm_discretise(lam_re, lam_im, log_step, b_re, b_im):
    lr = jnp.minimum(lam_re, -1e-4)
    li = lam_im
    dt = jnp.exp(log_step)[:, None]
    mag = jnp.exp(lr * dt)
    abar_re = mag * jnp.cos(li * dt)
    abar_im = mag * jnp.sin(li * dt)
    nr = abar_re - 1.0
    ni = abar_im
    den = lr * lr + li * li
    kr = (nr * lr + ni * li) / den
    ki = (ni * lr - nr * li) / den
    bbar_re = kr[..., None] * b_re - ki[..., None] * b_im
    bbar_im = kr[..., None] * b_im + ki[..., None] * b_re
    return abar_re, abar_im, bbar_re, bbar_im


def _block_diag(m):
    nblk, gb, r, c = m.shape
    eye = jnp.eye(gb, dtype=m.dtype)
    return jnp.einsum("jgrc,gh->jgrhc", m, eye).reshape(nblk, gb * r, gb * c)


def _ssm_matrices(lam_re, lam_im, log_step, b_re, b_im, c_re, c_im):
    G = lam_re.shape[0]
    gb = SSM_GROUPS_PER_BLOCK
    nblk = G // gb
    abar_re, abar_im, bbar_re, bbar_im = _ssm_discretise(lam_re, lam_im, log_step, b_re, b_im)
    bt_re = bbar_re.reshape(nblk, gb, STATE_P, GROUP_CH).swapaxes(2, 3)
    bt_im = bbar_im.reshape(nblk, gb, STATE_P, GROUP_CH).swapaxes(2, 3)
    bd_b = jnp.concatenate([_block_diag(bt_re), _block_diag(bt_im)], axis=2).astype(BF16)
    ct_re = c_re.reshape(nblk, gb, GROUP_CH, STATE_P).swapaxes(2, 3)
    ct_im = c_im.reshape(nblk, gb, GROUP_CH, STATE_P).swapaxes(2, 3)
    bd_c = jnp.concatenate([_block_diag(ct_re), _block_diag(-ct_im)], axis=1).astype(BF16)
    a_re = abar_re.reshape(nblk, 1, gb * STATE_P)
    a_im = abar_im.reshape(nblk, 1, gb * STATE_P)
    return bd_b, a_re, a_im, bd_c


def _chunk_state(state, fc):
    nbt, w, dff = state.shape
    return state.reshape(nbt, w, dff // fc, fc).transpose(2, 0, 1, 3)


def _unchunk_state(state):
    nchunk, nbt, w, fc = state.shape
    return state.transpose(1, 2, 0, 3).reshape(nbt, w, nchunk * fc)


TM_PROMPT = 512
TN_GLU = 512
TN_QKV = 1024
FFN_CHUNK = 512
SSM_TT = 256
TQ = 512
TQ_DIAG = 256


def kernel(x_prompt, x_sample, c_prompt, c_sample, cache_k, cache_v, state_ssm_re, state_ssm_im, state_conv, w_ada, b_ada, ln_g, ln_b, w_up, w_dconv, b_dconv, w_down, w_ssm_in, ssm_lam_re, ssm_lam_im, ssm_log_step, ssm_b_re, ssm_b_im, ssm_c_re, ssm_c_im, ssm_d, w_glu, w_qkv, lam_q1, lam_k1, lam_q2, lam_k2, subln_g, w_o):
    depth = w_ada.shape[0]
    bp, L, d = x_prompt.shape
    bs, T, _ = x_sample.shape
    dff = w_down.shape[1]
    alpha = (2 * depth) ** 0.25
    rows_s = bs * T

    mods = _ada(jnp.concatenate([c_prompt, c_sample], axis=0), w_ada, b_ada)
    w_up16, w_down16 = w_up.astype(BF16), w_down.astype(BF16)
    w_in16, w_glu16 = w_ssm_in.astype(BF16), w_glu.astype(BF16)
    w_qkv16, w_o16 = w_qkv.astype(BF16), w_o.astype(BF16)

    xp = x_prompt
    xs = x_sample.reshape(1, rows_s, d)
    zeros_conv = jnp.zeros((bp, state_conv.shape[2], dff), F32)
    outs = {name: [] for name in ("kp", "vp", "srp", "sip", "cvp", "ks", "vs", "srs", "sis", "cvs")}

    for i in range(depth):
        mp = [m.reshape(bp, 1, d) for m in jnp.split(mods[i, :bp], 6, axis=-1)]
        ms = [jnp.repeat(m, T, axis=0).reshape(1, rows_s, d)
              for m in jnp.split(mods[i, bp:], 6, axis=-1)]
        shp1, scp1, gtp1, shp2, scp2, gtp2 = mp
        shs1, scs1, gts1, shs2, scs2, gts2 = ms
        lng = ln_g[i].reshape(2, 1, d)
        lnb = ln_b[i].reshape(2, 1, d)
        j = i // 2
        if i % 2 == 0:
            bd_b, a_re, a_im, bd_c = _ssm_matrices(
                ssm_lam_re[j], ssm_lam_im[j], ssm_log_step[j], ssm_b_re[j], ssm_b_im[j],
                ssm_c_re[j], ssm_c_im[j])
            d_skip = ssm_d[j].reshape(1, d)
            gp = ssm_lam_re.shape[1] * ssm_lam_re.shape[2]
            up = _ssm_in(xp, scp1, shp1, w_in16, j, TM_PROMPT, d, True).reshape(L, bp, d)
            zp = jnp.zeros((bp, gp), F32)
            yp, srp, sip = _ssm_scan(up, bd_b, a_re, a_im, bd_c, zp, zp, SSM_TT)
            xp = _glu_ln(yp.reshape(L, bp * d), up.reshape(L, bp * d), d_skip, w_glu16, j, xp,
                         gtp1, lng[0], lnb[0], TM_PROMPT, TN_GLU, alpha, True)
            us = _ssm_in(xs, scs1, shs1, w_in16, j, rows_s, d, False)
            us_tb = us.reshape(bs, T, d).transpose(1, 0, 2)
            ys_tb, srs, sis = _ssm_scan(us_tb, bd_b, a_re, a_im, bd_c,
                                        state_ssm_re[j].reshape(bs, gp),
                                        state_ssm_im[j].reshape(bs, gp), T)
            ys = ys_tb.transpose(1, 0, 2).reshape(1, rows_s, d)
            xs = _glu_ln(ys, us, d_skip, w_glu16, j, xs, gts1, lng[0], lnb[0], rows_s, TN_GLU,
                         alpha, False)
            st_shape = ssm_lam_re.shape[1:]
            outs["srp"].append(srp.reshape((bp,) + st_shape))
            outs["sip"].append(sip.reshape((bp,) + st_shape))
            outs["srs"].append(srs.reshape((bs,) + st_shape))
            outs["sis"].append(sis.reshape((bs,) + st_shape))
        else:
            lam_init = 0.8 - 0.6 * math.exp(-0.3 * i)
            lam = (jnp.exp(jnp.sum(lam_q1[j] * lam_k1[j])) - jnp.exp(jnp.sum(lam_q2[j] * lam_k2[j]))
                   + lam_init).reshape(1, 1)
            sg = subln_g[j].reshape(1, 2 * HEAD_DIM)
            nsub = cache_k.shape[3]
            nhead = cache_v.shape[3]
            qp, kp, vp, kp16, vp16 = _qkv(xp, scp1, shp1, w_qkv16, j, TM_PROMPT, TN_QKV)
            op = _attn_prompt(lam, qp, kp16, vp16, sg, TQ, TQ_DIAG, lam_init)
            xp = _oproj_ln(op, w_o16, j, xp, gtp1, lng[0], lnb[0], TM_PROMPT, d, alpha)
            qs, ksn, vsn, ks16, vs16 = _qkv(xs, scs1, shs1, w_qkv16, j, rows_s, TN_QKV)
            P = cache_k.shape[2]
            da = w_o.shape[1]
            na = cache_k.shape[0]
            osm = _attn_sample(lam, qs.reshape(bs, T, da),
                               cache_k.reshape(na, bs, P, da), cache_v.reshape(na, bs, P, da), j,
                               ks16.reshape(bs, T, da), vs16.reshape(bs, T, da), sg, lam_init)
            xs = _oproj_ln(osm.reshape(1, rows_s, da), w_o16, j, xs, gts1, lng[0], lnb[0],
                           rows_s, d, alpha)
            outs["kp"].append(kp.reshape(bp, L, nsub, HEAD_DIM))
            outs["vp"].append(vp.reshape(bp, L, nhead, 2 * HEAD_DIM))
            outs["ks"].append(ksn.reshape(bs, T, nsub, HEAD_DIM))
            outs["vs"].append(vsn.reshape(bs, T, nhead, 2 * HEAD_DIM))

        bconv = b_dconv[i].reshape(1, dff)
        xp, cvp = _ffn(xp, scp2, shp2, gtp2, w_up16, w_dconv[i], bconv, w_down16, i,
                       _chunk_state(zeros_conv, FFN_CHUNK), lng[1], lnb[1],
                       1, TM_PROMPT, FFN_CHUNK, alpha)
        xs, cvs = _ffn(xs, scs2, shs2, gts2, w_up16, w_dconv[i], bconv, w_down16, i,
                       _chunk_state(state_conv[i], FFN_CHUNK), lng[1], lnb[1],
                       bs, T, FFN_CHUNK, alpha)
        outs["cvp"].append(_unchunk_state(cvp))
        outs["cvs"].append(_unchunk_state(cvs))

    return (xp, xs.reshape(bs, T, d),
            jnp.stack(outs["kp"]), jnp.stack(outs["vp"]),
            jnp.stack(outs["srp"]), jnp.stack(outs["sip"]), jnp.stack(outs["cvp"]),
            jnp.stack(outs["ks"]), jnp.stack(outs["vs"]),
            jnp.stack(outs["srs"]), jnp.stack(outs["sis"]), jnp.stack(outs["cvs"]))
```

```python
import functools
import math

import jax
import jax.numpy as jnp
from jax import lax
from jax.experimental import pallas as pl
from jax.experimental.pallas import tpu as pltpu

F32 = jnp.float32
BF16 = jnp.bfloat16

CHUNK = 64
HEAD_DIM = 128
GROUP_CH = 16
STATE_P = 64
LN_EPS = 1e-5
MASK_VALUE = -1e30
Q_SCALE = math.log2(math.e) * HEAD_DIM ** -0.5

SUBLANES = 8
LANES = 128
VMEM_LIMIT = 56 * 1024 * 1024

SSM_GROUPS_PER_BLOCK = LANES // GROUP_CH
SSM_HS = SSM_GROUPS_PER_BLOCK * STATE_P


def _params(sem):
    return pltpu.CompilerParams(dimension_semantics=sem, vmem_limit_bytes=VMEM_LIMIT)


def _gelu(x):
    return 0.5 * x * (1.0 + lax.erf(x * math.sqrt(0.5)))


def _dot(a, b):
    return jnp.dot(a, b, preferred_element_type=F32)


def _dot_nt(a, b):
    return lax.dot_general(a, b, (((1,), (1,)), ((), ())), preferred_element_type=F32)


def _ada_kernel(c_ref, w_ref, b_ref, o_ref):
    c = c_ref[...]
    a = (c * jax.nn.sigmoid(c)).astype(BF16)
    o_ref[...] = _dot(a, w_ref[...].astype(BF16)) + b_ref[...]


def _ada(c_all, w_ada, b_ada, tn=1024):
    depth, d, n = w_ada.shape
    rows = c_all.shape[0]
    return pl.pallas_call(
        _ada_kernel,
        grid=(depth, n // tn),
        in_specs=[
            pl.BlockSpec((rows, d), lambda l, j: (0, 0)),
            pl.BlockSpec((None, d, tn), lambda l, j: (l, 0, j)),
            pl.BlockSpec((None, 1, tn), lambda l, j: (l, 0, j)),
        ],
        out_specs=pl.BlockSpec((None, rows, tn), lambda l, j: (l, 0, j)),
        out_shape=jax.ShapeDtypeStruct((depth, rows, n), F32),
        compiler_params=_params(("parallel", "parallel")),
        name="ada",
    )(c_all, w_ada, b_ada.reshape(depth, 1, n))


def _modulate(x_ref, sc_ref, sh_ref):
    return (x_ref[...] * (1.0 + sc_ref[...]) + sh_ref[...]).astype(BF16)


def _modmm_kernel(x_ref, sc_ref, sh_ref, w_ref, *rest, bf16_scale):
    *out_refs, h_scr = rest

    @pl.when(pl.program_id(2) == 0)
    def _():
        h_scr[...] = _modulate(x_ref, sc_ref, sh_ref)

    r = _dot(h_scr[...], w_ref[...])
    for o_ref in out_refs:
        if o_ref.dtype == F32:
            o_ref[...] = r
        else:
            o_ref[...] = (r if bf16_scale == 1.0 else r * bf16_scale).astype(o_ref.dtype)


def _mod_spec(mod, tm):
    if mod.shape[1] == 1:
        return pl.BlockSpec((None, 1, mod.shape[2]), lambda b, i, n: (b, 0, 0))
    return pl.BlockSpec((None, tm, mod.shape[2]), lambda b, i, n: (b, i, 0))


def _mod_matmul(x, sc, sh, w, layer, col0, n, out_dtypes, tm, tn, name, bf16_scale=1.0):
    bsz, L, d = x.shape
    assert col0 % tn == 0 and n % tn == 0
    jb = col0 // tn
    out_spec = pl.BlockSpec((None, tm, tn), lambda b, i, j: (b, i, j))
    return pl.pallas_call(
        functools.partial(_modmm_kernel, bf16_scale=bf16_scale),
        grid=(bsz, L // tm, n // tn),
        in_specs=[
            pl.BlockSpec((None, tm, d), lambda b, i, j: (b, i, 0)),
            _mod_spec(sc, tm),
            _mod_spec(sh, tm),
            pl.BlockSpec((None, d, tn), lambda b, i, j: (layer, 0, jb + j)),
        ],
        out_specs=[out_spec] * len(out_dtypes),
        out_shape=[jax.ShapeDtypeStruct((bsz, L, n), dt) for dt in out_dtypes],
        scratch_shapes=[pltpu.VMEM((tm, d), BF16)],
        compiler_params=_params(("parallel", "parallel", "arbitrary")),
        name=name,
    )(x, sc, sh, w)


def _ssm_kernel(u_ref, d_ref, bdb_ref, are_ref, aim_ref, bdc_ref, sre_ref, sim_ref,
                z_ref, ore_ref, oim_ref, tb_scr, x_scr, h_scr, *, tt):
    i = pl.program_id(2)
    hs = SSM_HS

    @pl.when(i == 0)
    def _():
        h_scr[:, :hs] = sre_ref[...]
        h_scr[:, hs:] = sim_ref[...]

    for b in range(SUBLANES):
        tb_scr[pl.ds(b, tt, stride=SUBLANES), :] = u_ref[b]
    u = tb_scr[...]
    x_scr[...] = _dot(u.astype(BF16), bdb_ref[...]).reshape(tt, SUBLANES, 2 * hs)

    ar = jnp.broadcast_to(are_ref[...], (SUBLANES, hs))
    ai = jnp.broadcast_to(aim_ref[...], (SUBLANES, hs))

    def step(t, carry):
        hr, hi = carry
        nr = (ar * hr - ai * hi) + x_scr[t, :, :hs]
        ni = (ar * hi + ai * hr) + x_scr[t, :, hs:]
        x_scr[t, :, :hs] = nr
        x_scr[t, :, hs:] = ni
        return nr, ni

    hr, hi = lax.fori_loop(0, tt, step, (h_scr[:, :hs], h_scr[:, hs:]), unroll=8)
    h_scr[:, :hs] = hr
    h_scr[:, hs:] = hi

    hb = x_scr[...].reshape(tt * SUBLANES, 2 * hs).astype(BF16)
    y = _dot(hb, bdc_ref[...])
    tb_scr[...] = _gelu(y + d_ref[...] * u)
    for b in range(SUBLANES):
        z_ref[b] = tb_scr[pl.ds(b, tt, stride=SUBLANES), :].astype(z_ref.dtype)

    @pl.when(i == pl.num_programs(2) - 1)
    def _():
        ore_ref[...] = hr
        oim_ref[...] = hi


def _ssm_scan(u, d_skip, bd_b, a_re, a_im, bd_c, s_re, s_im, tt):
    bsz, L, d = u.shape
    nblk = d // LANES
    hs = SSM_HS
    st_spec = pl.BlockSpec((SUBLANES, hs), lambda j, g, i: (g, j))
    seq_spec = pl.BlockSpec((SUBLANES, tt, LANES), lambda j, g, i: (g, i, j))
    return pl.pallas_call(
        functools.partial(_ssm_kernel, tt=tt),
        grid=(nblk, bsz // SUBLANES, L // tt),
        in_specs=[
            seq_spec,
            pl.BlockSpec((1, LANES), lambda j, g, i: (0, j)),
            pl.BlockSpec((None, LANES, 2 * hs), lambda j, g, i: (j, 0, 0)),
            pl.BlockSpec((None, 1, hs), lambda j, g, i: (j, 0, 0)),
            pl.BlockSpec((None, 1, hs), lambda j, g, i: (j, 0, 0)),
            pl.BlockSpec((None, 2 * hs, LANES), lambda j, g, i: (j, 0, 0)),
            st_spec,
            st_spec,
        ],
        out_specs=[seq_spec, st_spec, st_spec],
        out_shape=[
            jax.ShapeDtypeStruct((bsz, L, d), BF16),
            jax.ShapeDtypeStruct(s_re.shape, F32),
            jax.ShapeDtypeStruct(s_im.shape, F32),
        ],
        scratch_shapes=[
            pltpu.VMEM((tt * SUBLANES, LANES), F32),
            pltpu.VMEM((tt, SUBLANES, 2 * hs), F32),
            pltpu.VMEM((SUBLANES, 2 * hs), F32),
        ],
        compiler_params=_params(("parallel", "parallel", "arbitrary")),
        name="ssm_scan",
    )(u, d_skip, bd_b, a_re, a_im, bd_c, s_re, s_im)


def _layer_norm_chunks(acc_scr, lng_ref, lnb_ref, o_ref, nchunk, tn):
    d = nchunk * tn
    parts = [acc_scr[c] for c in range(nchunk)]
    mu = sum(jnp.sum(p, axis=-1, keepdims=True) for p in parts) * (1.0 / d)
    cen = [p - mu for p in parts]
    var = sum(jnp.sum(q * q, axis=-1, keepdims=True) for q in cen) * (1.0 / d)
    inv = lax.rsqrt(var + LN_EPS)
    for c in range(nchunk):
        sl = slice(c * tn, (c + 1) * tn)
        o_ref[:, sl] = cen[c] * inv * lng_ref[:, sl] + lnb_ref[:, sl]


def _glu_ln_kernel(z_ref, wa_ref, wg_ref, x_ref, gt_ref, lng_ref, lnb_ref,
                   o_ref, acc_scr, *, nchunk, tn, alpha):
    c = pl.program_id(2)
    z = z_ref[...]
    a = _dot(z, wa_ref[...])
    g = _dot(z, wg_ref[...])
    m = a * jax.nn.sigmoid(g)
    acc_scr[c] = alpha * x_ref[...] + (1.0 + gt_ref[...]) * m

    @pl.when(c == nchunk - 1)
    def _():
        _layer_norm_chunks(acc_scr, lng_ref, lnb_ref, o_ref, nchunk, tn)


def _glu_ln(z, w_glu, layer, x, gate, ln_g, ln_b, tm, tn, alpha):
    bsz, L, d = x.shape
    nchunk = d // tn
    if gate.shape[1] == 1:
        gt_spec = pl.BlockSpec((None, 1, tn), lambda b, i, c: (b, 0, c))
    else:
        gt_spec = pl.BlockSpec((None, tm, tn), lambda b, i, c: (b, i, c))
    vec = pl.BlockSpec((1, d), lambda b, i, c: (0, 0))
    return pl.pallas_call(
        functools.partial(_glu_ln_kernel, nchunk=nchunk, tn=tn, alpha=alpha),
        grid=(bsz, L // tm, nchunk),
        in_specs=[
            pl.BlockSpec((None, tm, d), lambda b, i, c: (b, i, 0)),
            pl.BlockSpec((None, d, tn), lambda b, i, c: (layer, 0, c)),
            pl.BlockSpec((None, d, tn), lambda b, i, c: (layer, 0, nchunk + c)),
            pl.BlockSpec((None, tm, tn), lambda b, i, c: (b, i, c)),
            gt_spec,
            vec,
            vec,
        ],
        out_specs=pl.BlockSpec((None, tm, d), lambda b, i, c: (b, i, 0)),
        out_shape=jax.ShapeDtypeStruct((bsz, L, d), F32),
        scratch_shapes=[pltpu.VMEM((nchunk, tm, tn), F32)],
        compiler_params=_params(("parallel", "parallel", "arbitrary")),
        name="glu_ln",
    )(z, w_glu, w_glu, x, gate, ln_g, ln_b)


def _oproj_ln_kernel(o_in_ref, w_ref, x_ref, gt_ref, lng_ref, lnb_ref, o_ref, acc_scr,
                     *, nchunk, tn, alpha):
    c = pl.program_id(2)
    m = _dot(o_in_ref[...], w_ref[...])
    acc_scr[c] = alpha * x_ref[...] + (1.0 + gt_ref[...]) * m

    @pl.when(c == nchunk - 1)
    def _():
        _layer_norm_chunks(acc_scr, lng_ref, lnb_ref, o_ref, nchunk, tn)


def _oproj_ln(o_in, w_o, layer, x, gate, ln_g, ln_b, tm, tn, alpha):
    bsz, L, d = x.shape
    nchunk = d // tn
    if gate.shape[1] == 1:
        gt_spec = pl.BlockSpec((None, 1, tn), lambda b, i, c: (b, 0, c))
    else:
        gt_spec = pl.BlockSpec((None, tm, tn), lambda b, i, c: (b, i, c))
    vec = pl.BlockSpec((1, d), lambda b, i, c: (0, 0))
    return pl.pallas_call(
        functools.partial(_oproj_ln_kernel, nchunk=nchunk, tn=tn, alpha=alpha),
        grid=(bsz, L // tm, nchunk),
        in_specs=[
            pl.BlockSpec((None, tm, o_in.shape[2]), lambda b, i, c: (b, i, 0)),
            pl.BlockSpec((None, w_o.shape[1], tn), lambda b, i, c: (layer, 0, c)),
            pl.BlockSpec((None, tm, tn), lambda b, i, c: (b, i, c)),
            gt_spec,
            vec,
            vec,
        ],
        out_specs=pl.BlockSpec((None, tm, d), lambda b, i, c: (b, i, 0)),
        out_shape=jax.ShapeDtypeStruct((bsz, L, d), F32),
        scratch_shapes=[pltpu.VMEM((nchunk, tm, tn), F32)],
        compiler_params=_params(("parallel", "parallel", "arbitrary")),
        name="oproj_ln",
    )(o_in, w_o, x, gate, ln_g, ln_b)


def _ffn_kernel(x_ref, sc_ref, sh_ref, gt_ref, wg_ref, wv_ref, wc_ref, bc_ref, wd_ref,
                st_ref, lng_ref, lnb_ref, o_ref, cv_ref, h_scr, acc_scr, prev_scr,
                *, nb, T, nchunk, alpha):
    i = pl.program_id(1)
    c = pl.program_id(2)
    M = nb * T
    fc = wg_ref.shape[1]

    @pl.when(c == 0)
    def _():
        h_scr[...] = _modulate(x_ref, sc_ref, sh_ref)
        acc_scr[...] = jnp.zeros(acc_scr.shape, F32)

    @pl.when(i == 0)
    def _():
        prev_scr[...] = st_ref[c]

    @pl.when(i > 0)
    def _():
        prev_scr[...] = cv_ref[c]

    h = h_scr[...]
    g = _dot(h, wg_ref[...])
    v = _dot(h, wv_ref[...])

    row = lax.broadcasted_iota(jnp.int32, (M, 1), 0)
    if nb == 1:
        tpos = row
        p0 = prev_scr[0, 0:1, :]
        p1 = prev_scr[0, 1:2, :]
    else:
        tpos = lax.rem(row, T)
        p0 = jnp.broadcast_to(prev_scr[:, 0:1, :], (nb, T, fc)).reshape(M, fc)
        p1 = jnp.broadcast_to(prev_scr[:, 1:2, :], (nb, T, fc)).reshape(M, fc)
    s1 = jnp.where(tpos == 0, p1, pltpu.roll(g, 1, 0))
    s2 = jnp.where(tpos == 0, p0, jnp.where(tpos == 1, p1, pltpu.roll(g, 2, 0)))
    conv = bc_ref[...] + s2 * wc_ref[0:1, :] + s1 * wc_ref[1:2, :] + g * wc_ref[2:3, :]
    act = (_gelu(conv) * v).astype(BF16)
    acc_scr[...] += _dot(act, wd_ref[...])
    cv_ref[c] = g.reshape(nb, T, fc)[:, T - 2:, :]

    @pl.when(c == nchunk - 1)
    def _():
        r = alpha * x_ref[...] + (1.0 + gt_ref[...]) * acc_scr[...]
        mu = jnp.mean(r, axis=-1, keepdims=True)
        cen = r - mu
        var = jnp.mean(cen * cen, axis=-1, keepdims=True)
        o_ref[...] = cen * lax.rsqrt(var + LN_EPS) * lng_ref[...] + lnb_ref[...]


def _ffn(x, sc, sh, gate, w_up, w_conv, b_conv, w_down, layer, conv_state, ln_g, ln_b,
         nb, T, fc, alpha):
    S, R, d = x.shape
    dff = w_down.shape[1]
    nchunk = dff // fc
    tm = nb * T
    ntile = R // tm
    assert nb == 1 or ntile == 1

    def mod_spec(mod):
        if mod.shape[1] == 1:
            return pl.BlockSpec((None, 1, d), lambda s, i, c: (s, 0, 0))
        return pl.BlockSpec((None, tm, d), lambda s, i, c: (s, i, 0))

    vec = pl.BlockSpec((1, d), lambda s, i, c: (0, 0))
    st_spec = pl.BlockSpec((nchunk, nb, 2, fc), lambda s, i, c: (0, s, 0, 0))
    return pl.pallas_call(
        functools.partial(_ffn_kernel, nb=nb, T=T, nchunk=nchunk, alpha=alpha),
        grid=(S, ntile, nchunk),
        in_specs=[
            pl.BlockSpec((None, tm, d), lambda s, i, c: (s, i, 0)),
            mod_spec(sc),
            mod_spec(sh),
            mod_spec(gate),
            pl.BlockSpec((None, d, fc), lambda s, i, c: (layer, 0, c)),
            pl.BlockSpec((None, d, fc), lambda s, i, c: (layer, 0, nchunk + c)),
            pl.BlockSpec((w_conv.shape[0], fc), lambda s, i, c: (0, c)),
            pl.BlockSpec((1, fc), lambda s, i, c: (0, c)),
            pl.BlockSpec((None, fc, d), lambda s, i, c: (layer, c, 0)),
            st_spec,
            vec,
            vec,
        ],
        out_specs=[
            pl.BlockSpec((None, tm, d), lambda s, i, c: (s, i, 0)),
            st_spec,
        ],
        out_shape=[
            jax.ShapeDtypeStruct((S, R, d), F32),
            jax.ShapeDtypeStruct(conv_state.shape, F32),
        ],
        scratch_shapes=[
            pltpu.VMEM((tm, d), BF16),
            pltpu.VMEM((tm, d), F32),
            pltpu.VMEM((nb, 2, fc), F32),
        ],
        compiler_params=_params(("parallel", "arbitrary", "arbitrary")),
        name="conv_ffn",
    )(x, sc, sh, gate, w_up, w_up, w_conv, b_conv, w_down, conv_state, ln_g, ln_b)


def _qkv(x, sc, sh, w_qkv, layer, tm):
    da = w_qkv.shape[2] // 3
    (q,) = _mod_matmul(x, sc, sh, w_qkv, layer, 0, da, (BF16,), tm, da, "q_proj", Q_SCALE)
    k32, k16 = _mod_matmul(x, sc, sh, w_qkv, layer, da, da, (F32, BF16), tm, da, "k_proj")
    v32, v16 = _mod_matmul(x, sc, sh, w_qkv, layer, 2 * da, da, (F32, BF16), tm, da, "v_proj")
    return q, k32, v32, k16, v16


def _chunk_id(pos):
    assert CHUNK & (CHUNK - 1) == 0
    return lax.shift_right_logical(pos, CHUNK.bit_length() - 1)


def _sub_ln(o, g_ref, lam_init):
    o = o * lax.rsqrt(jnp.mean(o * o, axis=-1, keepdims=True) + LN_EPS)
    return o * g_ref[...] * (1.0 - lam_init)


def _lane_tile(x, width):
    return jnp.tile(x, (1, width // LANES))


def _online_update(s, v, m_scr, l_scr, acc_scr, idx):
    m_old = m_scr[idx]
    m_new = jnp.maximum(m_old, jnp.max(s, axis=-1, keepdims=True))
    corr = jnp.exp2(m_old - m_new)
    p = jnp.exp2(s - _lane_tile(m_new, s.shape[1]))
    l_scr[idx] = corr * l_scr[idx] + jnp.sum(p, axis=-1, keepdims=True)
    acc_scr[idx] = _lane_tile(corr, v.shape[1]) * acc_scr[idx] + _dot(p.astype(BF16), v)
    m_scr[idx] = m_new


def _attn_prompt_kernel(lam_ref, q_ref, k_ref, v_ref, g_ref, o_ref, m_scr, l_scr, acc_scr,
                        *, tq, sb, lam_init):
    i = pl.program_id(2)
    maps = (slice(0, HEAD_DIM), slice(HEAD_DIM, 2 * HEAD_DIM))
    base = pl.multiple_of(i * tq, tq)

    row_chunk = _chunk_id(lax.broadcasted_iota(jnp.int32, (sb, sb), 0))
    col_chunk = _chunk_id(lax.broadcasted_iota(jnp.int32, (sb, sb), 1))
    diag_keep = col_chunk <= row_chunk
    for r in range(tq // sb):
        rows = slice(r * sb, (r + 1) * sb)
        nk = (r + 1) * sb
        v = v_ref[pl.ds(base, nk), :]
        for idx, cols in enumerate(maps):
            s = _dot_nt(q_ref[rows, cols], k_ref[pl.ds(base, nk), cols])
            s_diag = jnp.where(diag_keep, s[:, r * sb:], MASK_VALUE)
            s = s_diag if r == 0 else jnp.concatenate([s[:, :r * sb], s_diag], axis=1)
            m = jnp.max(s, axis=-1, keepdims=True)
            p = jnp.exp2(s - m)
            m_scr[idx, rows] = jnp.broadcast_to(m, (sb, LANES))
            l_scr[idx, rows] = jnp.broadcast_to(jnp.sum(p, axis=-1, keepdims=True), (sb, LANES))
            acc_scr[idx, rows] = _dot(p.astype(BF16), v)

    def full_tiles(j0, count):
        starts = [pl.multiple_of((j0 + t) * tq, tq) for t in range(count)]
        s = [[_dot_nt(q_ref[:, cols], k_ref[pl.ds(st, tq), cols]) for cols in maps]
             for st in starts]
        for t, st in enumerate(starts):
            v = v_ref[pl.ds(st, tq), :]
            for idx in range(len(maps)):
                _online_update(s[t][idx], v, m_scr, l_scr, acc_scr, idx)

    def pair(jj, carry):
        full_tiles(2 * jj, 2)
        return carry

    lax.fori_loop(0, i // 2, pair, 0)

    @pl.when(i % 2 == 1)
    def _():
        full_tiles(i - 1, 1)

    lam = lam_ref[0, 0]
    hw = acc_scr.shape[2]
    o = (acc_scr[0] * _lane_tile(1.0 / l_scr[0], hw)
         - acc_scr[1] * _lane_tile(lam / l_scr[1], hw))
    o_ref[...] = _sub_ln(o, g_ref, lam_init).astype(o_ref.dtype)


def _attn_prompt(lam, q, k, v, subln_g, tq, sb, lam_init):
    bsz, L, da = q.shape
    hw = 2 * HEAD_DIM
    nh = da // hw
    assert tq % sb == 0 and sb % CHUNK == 0
    return pl.pallas_call(
        functools.partial(_attn_prompt_kernel, tq=tq, sb=sb, lam_init=lam_init),
        grid=(bsz, nh, L // tq),
        in_specs=[
            pl.BlockSpec(memory_space=pltpu.SMEM),
            pl.BlockSpec((None, tq, hw), lambda b, h, i: (b, i, h)),
            pl.BlockSpec((None, L, hw), lambda b, h, i: (b, 0, h)),
            pl.BlockSpec((None, L, hw), lambda b, h, i: (b, 0, h)),
            pl.BlockSpec((1, hw), lambda b, h, i: (0, 0)),
        ],
        out_specs=pl.BlockSpec((None, tq, hw), lambda b, h, i: (b, i, h)),
        out_shape=jax.ShapeDtypeStruct((bsz, L, da), BF16),
        scratch_shapes=[
            pltpu.VMEM((2, tq, LANES), F32),
            pltpu.VMEM((2, tq, LANES), F32),
            pltpu.VMEM((2, tq, hw), F32),
        ],
        compiler_params=_params(("parallel", "parallel", "arbitrary")),
        name="attn_prompt",
    )(lam, q, k, v, subln_g)


def _attn_sample_kernel(lam_ref, q_ref, ck_ref, cv_ref, kn_ref, vn_ref, g_ref, o_ref,
                        *, P, T, lam_init):
    ck = ck_ref[...].astype(BF16)
    cv = cv_ref[...].astype(BF16)
    kn = kn_ref[...]
    vn = vn_ref[...]
    qc = _chunk_id(P + lax.broadcasted_iota(jnp.int32, (T, 1), 0))
    keep_c = _chunk_id(lax.broadcasted_iota(jnp.int32, (1, P), 1)) <= qc
    keep_n = _chunk_id(P + lax.broadcasted_iota(jnp.int32, (1, T), 1)) <= qc

    def one_map(q, kc, kx):
        sc = jnp.where(keep_c, _dot_nt(q, kc), MASK_VALUE)
        sn = jnp.where(keep_n, _dot_nt(q, kx), MASK_VALUE)
        m = jnp.maximum(jnp.max(sc, axis=-1, keepdims=True),
                        jnp.max(sn, axis=-1, keepdims=True))
        pc = jnp.exp2(sc - m)
        pn = jnp.exp2(sn - m)
        l = jnp.sum(pc, axis=-1, keepdims=True) + jnp.sum(pn, axis=-1, keepdims=True)
        return (_dot(pc.astype(BF16), cv) + _dot(pn.astype(BF16), vn)) / l

    o1 = one_map(q_ref[:, :HEAD_DIM], ck[:, :HEAD_DIM], kn[:, :HEAD_DIM])
    o2 = one_map(q_ref[:, HEAD_DIM:], ck[:, HEAD_DIM:], kn[:, HEAD_DIM:])
    o = o1 - lam_ref[0, 0] * o2
    o_ref[...] = _sub_ln(o, g_ref, lam_init).astype(o_ref.dtype)


def _attn_sample(lam, q, cache_k, cache_v, layer, k_new, v_new, subln_g, lam_init):
    bsz, T, da = q.shape
    P = cache_k.shape[2]
    hw = 2 * HEAD_DIM
    nh = da // hw
    new_spec = pl.BlockSpec((None, T, hw), lambda b, h: (b, 0, h))
    cache_spec = pl.BlockSpec((None, None, P, hw), lambda b, h: (layer, b, 0, h))
    return pl.pallas_call(
        functools.partial(_attn_sample_kernel, P=P, T=T, lam_init=lam_init),
        grid=(bsz, nh),
        in_specs=[
            pl.BlockSpec(memory_space=pltpu.SMEM),
            new_spec, cache_spec, cache_spec, new_spec, new_spec,
            pl.BlockSpec((1, hw), lambda b, h: (0, 0)),
        ],
        out_specs=new_spec,
        out_shape=jax.ShapeDtypeStruct((bsz, T, da), BF16),
        compiler_params=_params(("parallel", "parallel")),
        name="attn_sample",
    )(lam, q, cache_k, cache_v, k_new, v_new, subln_g)


def _ssm_discretise(lam_re, lam_im, log_step, b_re, b_im):
    lr = jnp.minimum(lam_re, -1e-4)
    li = lam_im
    dt = jnp.exp(log_step)[:, None]
    mag = jnp.exp(lr * dt)
    abar_re = mag * jnp.cos(li * dt)
    abar_im = mag * jnp.sin(li * dt)
    nr = abar_re - 1.0
    ni = abar_im
    den = lr * lr + li * li
    kr = (nr * lr + ni * li) / den
    ki = (ni * lr - nr * li) / den
    bbar_re = kr[..., None] * b_re - ki[..., None] * b_im
    bbar_im = kr[..., None] * b_im + ki[..., None] * b_re
    return abar_re, abar_im, bbar_re, bbar_im


def _block_diag(m):
    nblk, gb, r, c = m.shape
    eye = jnp.eye(gb, dtype=m.dtype)
    return jnp.einsum("jgrc,gh->jgrhc", m, eye).reshape(nblk, gb * r, gb * c)


def _ssm_matrices(lam_re, lam_im, log_step, b_re, b_im, c_re, c_im):
    G = lam_re.shape[0]
    gb = SSM_GROUPS_PER_BLOCK
    nblk = G // gb
    abar_re, abar_im, bbar_re, bbar_im = _ssm_discretise(lam_re, lam_im, log_step, b_re, b_im)
    bt_re = bbar_re.reshape(nblk, gb, STATE_P, GROUP_CH).swapaxes(2, 3)
    bt_im = bbar_im.reshape(nblk, gb, STATE_P, GROUP_CH).swapaxes(2, 3)
    bd_b = jnp.concatenate([_block_diag(bt_re), _block_diag(bt_im)], axis=2).astype(BF16)
    ct_re = c_re.reshape(nblk, gb, GROUP_CH, STATE_P).swapaxes(2, 3)
    ct_im = c_im.reshape(nblk, gb, GROUP_CH, STATE_P).swapaxes(2, 3)
    bd_c = jnp.concatenate([_block_diag(ct_re), _block_diag(-ct_im)], axis=1).astype(BF16)
    a_re = abar_re.reshape(nblk, 1, gb * STATE_P)
    a_im = abar_im.reshape(nblk, 1, gb * STATE_P)
    return bd_b, a_re, a_im, bd_c


def _chunk_state(state, fc):
    nbt, w, dff = state.shape
    return state.reshape(nbt, w, dff // fc, fc).transpose(2, 0, 1, 3)


def _unchunk_state(state):
    nchunk, nbt, w, fc = state.shape
    return state.transpose(1, 2, 0, 3).reshape(nbt, w, nchunk * fc)


TM_PROMPT = 512
TN_GLU = 1024
FFN_CHUNK = 512
SSM_TT = 256
TQ = 512
TQ_DIAG = 256


def kernel(x_prompt, x_sample, c_prompt, c_sample, cache_k, cache_v, state_ssm_re, state_ssm_im, state_conv, w_ada, b_ada, ln_g, ln_b, w_up, w_dconv, b_dconv, w_down, w_ssm_in, ssm_lam_re, ssm_lam_im, ssm_log_step, ssm_b_re, ssm_b_im, ssm_c_re, ssm_c_im, ssm_d, w_glu, w_qkv, lam_q1, lam_k1, lam_q2, lam_k2, subln_g, w_o):
    depth = w_ada.shape[0]
    bp, L, d = x_prompt.shape
    bs, T, _ = x_sample.shape
    dff = w_down.shape[1]
    alpha = (2 * depth) ** 0.25
    rows_s = bs * T

    mods = _ada(jnp.concatenate([c_prompt, c_sample], axis=0), w_ada, b_ada)
    w_up16, w_down16 = w_up.astype(BF16), w_down.astype(BF16)
    w_in16, w_glu16 = w_ssm_in.astype(BF16), w_glu.astype(BF16)
    w_qkv16, w_o16 = w_qkv.astype(BF16), w_o.astype(BF16)

    xp = x_prompt
    xs = x_sample.reshape(1, rows_s, d)
    zeros_conv = jnp.zeros((bp, state_conv.shape[2], dff), F32)
    outs = {name: [] for name in ("kp", "vp", "srp", "sip", "cvp", "ks", "vs", "srs", "sis", "cvs")}

    for i in range(depth):
        mp = [m.reshape(bp, 1, d) for m in jnp.split(mods[i, :bp], 6, axis=-1)]
        ms = [jnp.repeat(m, T, axis=0).reshape(1, rows_s, d)
              for m in jnp.split(mods[i, bp:], 6, axis=-1)]
        shp1, scp1, gtp1, shp2, scp2, gtp2 = mp
        shs1, scs1, gts1, shs2, scs2, gts2 = ms
        lng = ln_g[i].reshape(2, 1, d)
        lnb = ln_b[i].reshape(2, 1, d)
        j = i // 2
        if i % 2 == 0:
            bd_b, a_re, a_im, bd_c = _ssm_matrices(
                ssm_lam_re[j], ssm_lam_im[j], ssm_log_step[j], ssm_b_re[j], ssm_b_im[j],
                ssm_c_re[j], ssm_c_im[j])
            d_skip = ssm_d[j].reshape(1, d)
            gp = ssm_lam_re.shape[1] * ssm_lam_re.shape[2]
            (up,) = _mod_matmul(xp, scp1, shp1, w_in16, j, 0, d, (F32,), TM_PROMPT, d, "ssm_in")
            zero_state = jnp.zeros((bp, gp), F32)
            zp, srp, sip = _ssm_scan(up, d_skip, bd_b, a_re, a_im, bd_c, zero_state, zero_state,
                                     SSM_TT)
            xp = _glu_ln(zp, w_glu16, j, xp, gtp1, lng[0], lnb[0], TM_PROMPT, TN_GLU, alpha)
            (us,) = _mod_matmul(xs, scs1, shs1, w_in16, j, 0, d, (F32,), rows_s, d, "ssm_in")
            zs, srs, sis = _ssm_scan(us.reshape(bs, T, d), d_skip, bd_b, a_re, a_im, bd_c,
                                     state_ssm_re[j].reshape(bs, gp),
                                     state_ssm_im[j].reshape(bs, gp), T)
            xs = _glu_ln(zs.reshape(1, rows_s, d), w_glu16, j, xs, gts1, lng[0], lnb[0],
                         rows_s, TN_GLU, alpha)
            st_shape = ssm_lam_re.shape[1:]
            outs["srp"].append(srp.reshape((bp,) + st_shape))
            outs["sip"].append(sip.reshape((bp,) + st_shape))
            outs["srs"].append(srs.reshape((bs,) + st_shape))
            outs["sis"].append(sis.reshape((bs,) + st_shape))
        else:
            lam_init = 0.8 - 0.6 * math.exp(-0.3 * i)
            lam = (jnp.exp(jnp.sum(lam_q1[j] * lam_k1[j])) - jnp.exp(jnp.sum(lam_q2[j] * lam_k2[j]))
                   + lam_init).reshape(1, 1)
            sg = subln_g[j].reshape(1, 2 * HEAD_DIM)
            nsub = cache_k.shape[3]
            nhead = cache_v.shape[3]
            qp, kp, vp, kp16, vp16 = _qkv(xp, scp1, shp1, w_qkv16, j, TM_PROMPT)
            op = _attn_prompt(lam, qp, kp16, vp16, sg, TQ, TQ_DIAG, lam_init)
            xp = _oproj_ln(op, w_o16, j, xp, gtp1, lng[0], lnb[0], TM_PROMPT, d, alpha)
            qs, ksn, vsn, ks16, vs16 = _qkv(xs, scs1, shs1, w_qkv16, j, rows_s)
            P = cache_k.shape[2]
            da = w_o.shape[1]
            na = cache_k.shape[0]
            osm = _attn_sample(lam, qs.reshape(bs, T, da),
                               cache_k.reshape(na, bs, P, da), cache_v.reshape(na, bs, P, da), j,
                               ks16.reshape(bs, T, da), vs16.reshape(bs, T, da), sg, lam_init)
            xs = _oproj_ln(osm.reshape(1, rows_s, da), w_o16, j, xs, gts1, lng[0], lnb[0],
                           rows_s, d, alpha)
            outs["kp"].append(kp.reshape(bp, L, nsub, HEAD_DIM))
            outs["vp"].append(vp.reshape(bp, L, nhead, 2 * HEAD_DIM))
            outs["ks"].append(ksn.reshape(bs, T, nsub, HEAD_DIM))
            outs["vs"].append(vsn.reshape(bs, T, nhead, 2 * HEAD_DIM))

        bconv = b_dconv[i].reshape(1, dff)
        xp, cvp = _ffn(xp, scp2, shp2, gtp2, w_up16, w_dconv[i], bconv, w_down16, i,
                       _chunk_state(zeros_conv, FFN_CHUNK), lng[1], lnb[1],
                       1, TM_PROMPT, FFN_CHUNK, alpha)
        xs, cvs = _ffn(xs, scs2, shs2, gts2, w_up16, w_dconv[i], bconv, w_down16, i,
                       _chunk_state(state_conv[i], FFN_CHUNK), lng[1], lnb[1],
                       bs, T, FFN_CHUNK, alpha)
        outs["cvp"].append(_unchunk_state(cvp))
        outs["cvs"].append(_unchunk_state(cvs))

    return (xp, xs.reshape(bs, T, d),
            jnp.stack(outs["kp"]), jnp.stack(outs["vp"]),
            jnp.stack(outs["srp"]), jnp.stack(outs["sip"]), jnp.stack(outs["cvp"]),
            jnp.stack(outs["ks"]), jnp.stack(outs["vs"]),
            jnp.stack(outs["srs"]), jnp.stack(outs["sis"]), jnp.stack(outs["cvs"]))
```

```python
import functools
import math

import jax
import jax.numpy as jnp
from jax import lax
from jax.experimental import pallas as pl
from jax.experimental.pallas import tpu as pltpu

F32 = jnp.float32
BF16 = jnp.bfloat16

CHUNK = 64
HEAD_DIM = 128
GROUP_CH = 16
STATE_P = 64
LN_EPS = 1e-5
MASK_VALUE = -1e30
Q_SCALE = math.log2(math.e) * HEAD_DIM ** -0.5

SUBLANES = 8
LANES = 128
VMEM_LIMIT = 56 * 1024 * 1024

SSM_GROUPS_PER_BLOCK = LANES // GROUP_CH
SSM_HS = SSM_GROUPS_PER_BLOCK * STATE_P


def _params(sem):
    return pltpu.CompilerParams(dimension_semantics=sem, vmem_limit_bytes=VMEM_LIMIT)


def _gelu(x):
    return 0.5 * x * (1.0 + lax.erf(x * math.sqrt(0.5)))


def _dot(a, b):
    return jnp.dot(a, b, preferred_element_type=F32)


def _dot_nt(a, b):
    return lax.dot_general(a, b, (((1,), (1,)), ((), ())), preferred_element_type=F32)


def _ada_kernel(c_ref, w_ref, b_ref, o_ref):
    c = c_ref[...]
    a = (c * jax.nn.sigmoid(c)).astype(BF16)
    o_ref[...] = _dot(a, w_ref[...].astype(BF16)) + b_ref[...]


def _ada(c_all, w_ada, b_ada, tn=1024):
    depth, d, n = w_ada.shape
    rows = c_all.shape[0]
    return pl.pallas_call(
        _ada_kernel,
        grid=(depth, n // tn),
        in_specs=[
            pl.BlockSpec((rows, d), lambda l, j: (0, 0)),
            pl.BlockSpec((None, d, tn), lambda l, j: (l, 0, j)),
            pl.BlockSpec((None, 1, tn), lambda l, j: (l, 0, j)),
        ],
        out_specs=pl.BlockSpec((None, rows, tn), lambda l, j: (l, 0, j)),
        out_shape=jax.ShapeDtypeStruct((depth, rows, n), F32),
        compiler_params=_params(("parallel", "parallel")),
        name="ada",
    )(c_all, w_ada, b_ada.reshape(depth, 1, n))


def _modulate(x_ref, sc_ref, sh_ref):
    return (x_ref[...] * (1.0 + sc_ref[...]) + sh_ref[...]).astype(BF16)


def _store_heads(o_ref, r, heads):
    nh, hd = heads
    rows = r.shape[0]
    for h in range(nh):
        part = r[:, h * hd:(h + 1) * hd]
        if hd == LANES:
            o_ref[pl.ds(h, rows, stride=nh), :] = part
        else:
            o_ref[:, h, :] = part


def _modmm_kernel(x_ref, sc_ref, sh_ref, w_ref, *rest, bf16_scale, f32_heads):
    *out_refs, h_scr = rest

    @pl.when(pl.program_id(2) == 0)
    def _():
        h_scr[...] = _modulate(x_ref, sc_ref, sh_ref)

    r = _dot(h_scr[...], w_ref[...])
    for o_ref in out_refs:
        if o_ref.dtype != F32:
            o_ref[...] = (r if bf16_scale == 1.0 else r * bf16_scale).astype(o_ref.dtype)
        elif f32_heads is None:
            o_ref[...] = r
        else:
            _store_heads(o_ref, r, f32_heads)


def _mod_spec(mod, tm):
    if mod.shape[1] == 1:
        return pl.BlockSpec((None, 1, mod.shape[2]), lambda b, i, n: (b, 0, 0))
    return pl.BlockSpec((None, tm, mod.shape[2]), lambda b, i, n: (b, i, 0))


def _mod_matmul(x, sc, sh, w, layer, col0, n, out_dtypes, tm, tn, name, bf16_scale=1.0,
                f32_heads=None):
    bsz, L, d = x.shape
    assert col0 % tn == 0 and n % tn == 0
    jb = col0 // tn
    nt = L // tm
    flat_spec = pl.BlockSpec((None, tm, tn), lambda b, i, j: (b, i, j))
    out_specs, out_shape = [], []
    for dt in out_dtypes:
        if dt != F32 or f32_heads is None:
            out_specs.append(flat_spec)
            out_shape.append(jax.ShapeDtypeStruct((bsz, L, n), dt))
            continue
        nh, hd = f32_heads
        assert tn == n == nh * hd
        if hd == LANES:
            out_specs.append(pl.BlockSpec((tm * nh, hd), lambda b, i, j: (b * nt + i, 0)))
            out_shape.append(jax.ShapeDtypeStruct((bsz * L * nh, hd), dt))
        else:
            out_specs.append(pl.BlockSpec((None, tm, nh, hd), lambda b, i, j: (b, i, 0, 0)))
            out_shape.append(jax.ShapeDtypeStruct((bsz, L, nh, hd), dt))
    return pl.pallas_call(
        functools.partial(_modmm_kernel, bf16_scale=bf16_scale, f32_heads=f32_heads),
        grid=(bsz, nt, n // tn),
        in_specs=[
            pl.BlockSpec((None, tm, d), lambda b, i, j: (b, i, 0)),
            _mod_spec(sc, tm),
            _mod_spec(sh, tm),
            pl.BlockSpec((None, d, tn), lambda b, i, j: (layer, 0, jb + j)),
        ],
        out_specs=out_specs,
        out_shape=out_shape,
        scratch_shapes=[pltpu.VMEM((tm, d), BF16)],
        compiler_params=_params(("parallel", "parallel", "arbitrary")),
        name=name,
    )(x, sc, sh, w)


def _ssm_kernel(u_ref, d_ref, bdb_ref, are_ref, aim_ref, bdc_ref, sre_ref, sim_ref,
                z_ref, ore_ref, oim_ref, tb_scr, x_scr, h_scr, *, tt):
    i = pl.program_id(2)
    hs = SSM_HS

    @pl.when(i == 0)
    def _():
        h_scr[:, :hs] = sre_ref[...]
        h_scr[:, hs:] = sim_ref[...]

    for b in range(SUBLANES):
        tb_scr[pl.ds(b, tt, stride=SUBLANES), :] = u_ref[b]
    u = tb_scr[...]
    x_scr[...] = _dot(u.astype(BF16), bdb_ref[...]).reshape(tt, SUBLANES, 2 * hs)

    ar = jnp.broadcast_to(are_ref[...], (SUBLANES, hs))
    ai = jnp.broadcast_to(aim_ref[...], (SUBLANES, hs))

    def step(t, carry):
        hr, hi = carry
        nr = (ar * hr - ai * hi) + x_scr[t, :, :hs]
        ni = (ar * hi + ai * hr) + x_scr[t, :, hs:]
        x_scr[t, :, :hs] = nr
        x_scr[t, :, hs:] = ni
        return nr, ni

    hr, hi = lax.fori_loop(0, tt, step, (h_scr[:, :hs], h_scr[:, hs:]), unroll=8)
    h_scr[:, :hs] = hr
    h_scr[:, hs:] = hi

    hb = x_scr[...].reshape(tt * SUBLANES, 2 * hs).astype(BF16)
    y = _dot(hb, bdc_ref[...])
    tb_scr[...] = _gelu(y + d_ref[...] * u)
    for b in range(SUBLANES):
        z_ref[b] = tb_scr[pl.ds(b, tt, stride=SUBLANES), :].astype(z_ref.dtype)

    @pl.when(i == pl.num_programs(2) - 1)
    def _():
        ore_ref[...] = hr
        oim_ref[...] = hi


def _ssm_scan(u, d_skip, bd_b, a_re, a_im, bd_c, s_re, s_im, tt):
    bsz, L, d = u.shape
    nblk = d // LANES
    hs = SSM_HS
    st_spec = pl.BlockSpec((SUBLANES, hs), lambda j, g, i: (g, j))
    seq_spec = pl.BlockSpec((SUBLANES, tt, LANES), lambda j, g, i: (g, i, j))
    return pl.pallas_call(
        functools.partial(_ssm_kernel, tt=tt),
        grid=(nblk, bsz // SUBLANES, L // tt),
        in_specs=[
            seq_spec,
            pl.BlockSpec((1, LANES), lambda j, g, i: (0, j)),
            pl.BlockSpec((None, LANES, 2 * hs), lambda j, g, i: (j, 0, 0)),
            pl.BlockSpec((None, 1, hs), lambda j, g, i: (j, 0, 0)),
            pl.BlockSpec((None, 1, hs), lambda j, g, i: (j, 0, 0)),
            pl.BlockSpec((None, 2 * hs, LANES), lambda j, g, i: (j, 0, 0)),
            st_spec,
            st_spec,
        ],
        out_specs=[seq_spec, st_spec, st_spec],
        out_shape=[
            jax.ShapeDtypeStruct((bsz, L, d), BF16),
            jax.ShapeDtypeStruct(s_re.shape, F32),
            jax.ShapeDtypeStruct(s_im.shape, F32),
        ],
        scratch_shapes=[
            pltpu.VMEM((tt * SUBLANES, LANES), F32),
            pltpu.VMEM((tt, SUBLANES, 2 * hs), F32),
            pltpu.VMEM((SUBLANES, 2 * hs), F32),
        ],
        compiler_params=_params(("parallel", "parallel", "arbitrary")),
        name="ssm_scan",
    )(u, d_skip, bd_b, a_re, a_im, bd_c, s_re, s_im)


def _layer_norm_chunks(acc_scr, lng_ref, lnb_ref, o_ref, nchunk, tn):
    d = nchunk * tn
    parts = [acc_scr[c] for c in range(nchunk)]
    mu = sum(jnp.sum(p, axis=-1, keepdims=True) for p in parts) * (1.0 / d)
    cen = [p - mu for p in parts]
    var = sum(jnp.sum(q * q, axis=-1, keepdims=True) for q in cen) * (1.0 / d)
    inv = lax.rsqrt(var + LN_EPS)
    for c in range(nchunk):
        sl = slice(c * tn, (c + 1) * tn)
        o_ref[:, sl] = cen[c] * inv * lng_ref[:, sl] + lnb_ref[:, sl]


def _glu_ln_kernel(z_ref, wa_ref, wg_ref, x_ref, gt_ref, lng_ref, lnb_ref,
                   o_ref, acc_scr, *, nchunk, tn, alpha):
    c = pl.program_id(2)
    z = z_ref[...]
    a = _dot(z, wa_ref[...])
    g = _dot(z, wg_ref[...])
    m = a * jax.nn.sigmoid(g)
    acc_scr[c] = alpha * x_ref[...] + (1.0 + gt_ref[...]) * m

    @pl.when(c == nchunk - 1)
    def _():
        _layer_norm_chunks(acc_scr, lng_ref, lnb_ref, o_ref, nchunk, tn)


def _glu_ln(z, w_glu, layer, x, gate, ln_g, ln_b, tm, tn, alpha):
    bsz, L, d = x.shape
    nchunk = d // tn
    if gate.shape[1] == 1:
        gt_spec = pl.BlockSpec((None, 1, tn), lambda b, i, c: (b, 0, c))
    else:
        gt_spec = pl.BlockSpec((None, tm, tn), lambda b, i, c: (b, i, c))
    vec = pl.BlockSpec((1, d), lambda b, i, c: (0, 0))
    return pl.pallas_call(
        functools.partial(_glu_ln_kernel, nchunk=nchunk, tn=tn, alpha=alpha),
        grid=(bsz, L // tm, nchunk),
        in_specs=[
            pl.BlockSpec((None, tm, d), lambda b, i, c: (b, i, 0)),
            pl.BlockSpec((None, d, tn), lambda b, i, c: (layer, 0, c)),
            pl.BlockSpec((None, d, tn), lambda b, i, c: (layer, 0, nchunk + c)),
            pl.BlockSpec((None, tm, tn), lambda b, i, c: (b, i, c)),
            gt_spec,
            vec,
            vec,
        ],
        out_specs=pl.BlockSpec((None, tm, d), lambda b, i, c: (b, i, 0)),
        out_shape=jax.ShapeDtypeStruct((bsz, L, d), F32),
        scratch_shapes=[pltpu.VMEM((nchunk, tm, tn), F32)],
        compiler_params=_params(("parallel", "parallel", "arbitrary")),
        name="glu_ln",
    )(z, w_glu, w_glu, x, gate, ln_g, ln_b)


def _oproj_ln_kernel(o_in_ref, w_ref, x_ref, gt_ref, lng_ref, lnb_ref, o_ref, acc_scr,
                     *, nchunk, tn, alpha):
    c = pl.program_id(2)
    m = _dot(o_in_ref[...], w_ref[...])
    acc_scr[c] = alpha * x_ref[...] + (1.0 + gt_ref[...]) * m

    @pl.when(c == nchunk - 1)
    def _():
        _layer_norm_chunks(acc_scr, lng_ref, lnb_ref, o_ref, nchunk, tn)


def _oproj_ln(o_in, w_o, layer, x, gate, ln_g, ln_b, tm, tn, alpha):
    bsz, L, d = x.shape
    nchunk = d // tn
    if gate.shape[1] == 1:
        gt_spec = pl.BlockSpec((None, 1, tn), lambda b, i, c: (b, 0, c))
    else:
        gt_spec = pl.BlockSpec((None, tm, tn), lambda b, i, c: (b, i, c))
    vec = pl.BlockSpec((1, d), lambda b, i, c: (0, 0))
    return pl.pallas_call(
        functools.partial(_oproj_ln_kernel, nchunk=nchunk, tn=tn, alpha=alpha),
        grid=(bsz, L // tm, nchunk),
        in_specs=[
            pl.BlockSpec((None, tm, o_in.shape[2]), lambda b, i, c: (b, i, 0)),
            pl.BlockSpec((None, w_o.shape[1], tn), lambda b, i, c: (layer, 0, c)),
            pl.BlockSpec((None, tm, tn), lambda b, i, c: (b, i, c)),
            gt_spec,
            vec,
            vec,
        ],
        out_specs=pl.BlockSpec((None, tm, d), lambda b, i, c: (b, i, 0)),
        out_shape=jax.ShapeDtypeStruct((bsz, L, d), F32),
        scratch_shapes=[pltpu.VMEM((nchunk, tm, tn), F32)],
        compiler_params=_params(("parallel", "parallel", "arbitrary")),
        name="oproj_ln",
    )(o_in, w_o, x, gate, ln_g, ln_b)


def _ffn_kernel(x_ref, sc_ref, sh_ref, gt_ref, wg_ref, wv_ref, wc_ref, bc_ref, wd_ref,
                st_ref, lng_ref, lnb_ref, o_ref, cv_ref, h_scr, acc_scr, prev_scr,
                *, nb, T, nchunk, alpha):
    i = pl.program_id(1)
    c = pl.program_id(2)
    M = nb * T
    fc = wg_ref.shape[1]

    @pl.when(c == 0)
    def _():
        h_scr[...] = _modulate(x_ref, sc_ref, sh_ref)
        acc_scr[...] = jnp.zeros(acc_scr.shape, F32)

    @pl.when(i == 0)
    def _():
        prev_scr[...] = st_ref[c]

    @pl.when(i > 0)
    def _():
        prev_scr[...] = cv_ref[c]

    h = h_scr[...]
    g = _dot(h, wg_ref[...])
    v = _dot(h, wv_ref[...])

    row = lax.broadcasted_iota(jnp.int32, (M, 1), 0)
    if nb == 1:
        tpos = row
        p0 = prev_scr[0, 0:1, :]
        p1 = prev_scr[0, 1:2, :]
    else:
        tpos = lax.rem(row, T)
        p0 = jnp.broadcast_to(prev_scr[:, 0:1, :], (nb, T, fc)).reshape(M, fc)
        p1 = jnp.broadcast_to(prev_scr[:, 1:2, :], (nb, T, fc)).reshape(M, fc)
    s1 = jnp.where(tpos == 0, p1, pltpu.roll(g, 1, 0))
    s2 = jnp.where(tpos == 0, p0, jnp.where(tpos == 1, p1, pltpu.roll(g, 2, 0)))
    conv = bc_ref[...] + s2 * wc_ref[0:1, :] + s1 * wc_ref[1:2, :] + g * wc_ref[2:3, :]
    act = (_gelu(conv) * v).astype(BF16)
    acc_scr[...] += _dot(act, wd_ref[...])
    cv_ref[c] = g.reshape(nb, T, fc)[:, T - 2:, :]

    @pl.when(c == nchunk - 1)
    def _():
        r = alpha * x_ref[...] + (1.0 + gt_ref[...]) * acc_scr[...]
        mu = jnp.mean(r, axis=-1, keepdims=True)
        cen = r - mu
        var = jnp.mean(cen * cen, axis=-1, keepdims=True)
        o_ref[...] = cen * lax.rsqrt(var + LN_EPS) * lng_ref[...] + lnb_ref[...]


def _ffn(x, sc, sh, gate, w_up, w_conv, b_conv, w_down, layer, conv_state, ln_g, ln_b,
         nb, T, fc, alpha):
    S, R, d = x.shape
    dff = w_down.shape[1]
    nchunk = dff // fc
    tm = nb * T
    ntile = R // tm
    assert nb == 1 or ntile == 1

    def mod_spec(mod):
        if mod.shape[1] == 1:
            return pl.BlockSpec((None, 1, d), lambda s, i, c: (s, 0, 0))
        return pl.BlockSpec((None, tm, d), lambda s, i, c: (s, i, 0))

    vec = pl.BlockSpec((1, d), lambda s, i, c: (0, 0))
    st_spec = pl.BlockSpec((nchunk, nb, 2, fc), lambda s, i, c: (0, s, 0, 0))
    return pl.pallas_call(
        functools.partial(_ffn_kernel, nb=nb, T=T, nchunk=nchunk, alpha=alpha),
        grid=(S, ntile, nchunk),
        in_specs=[
            pl.BlockSpec((None, tm, d), lambda s, i, c: (s, i, 0)),
            mod_spec(sc),
            mod_spec(sh),
            mod_spec(gate),
            pl.BlockSpec((None, d, fc), lambda s, i, c: (layer, 0, c)),
            pl.BlockSpec((None, d, fc), lambda s, i, c: (layer, 0, nchunk + c)),
            pl.BlockSpec((w_conv.shape[0], fc), lambda s, i, c: (0, c)),
            pl.BlockSpec((1, fc), lambda s, i, c: (0, c)),
            pl.BlockSpec((None, fc, d), lambda s, i, c: (layer, c, 0)),
            st_spec,
            vec,
            vec,
        ],
        out_specs=[
            pl.BlockSpec((None, tm, d), lambda s, i, c: (s, i, 0)),
            st_spec,
        ],
        out_shape=[
            jax.ShapeDtypeStruct((S, R, d), F32),
            jax.ShapeDtypeStruct(conv_state.shape, F32),
        ],
        scratch_shapes=[
            pltpu.VMEM((tm, d), BF16),
            pltpu.VMEM((tm, d), F32),
            pltpu.VMEM((nb, 2, fc), F32),
        ],
        compiler_params=_params(("parallel", "arbitrary", "arbitrary")),
        name="conv_ffn",
    )(x, sc, sh, gate, w_up, w_up, w_conv, b_conv, w_down, conv_state, ln_g, ln_b)


def _qkv(x, sc, sh, w_qkv, layer, tm):
    da = w_qkv.shape[2] // 3
    (q,) = _mod_matmul(x, sc, sh, w_qkv, layer, 0, da, (BF16,), tm, da, "q_proj", Q_SCALE)
    k32, k16 = _mod_matmul(x, sc, sh, w_qkv, layer, da, da, (F32, BF16), tm, da, "k_proj",
                           f32_heads=(da // HEAD_DIM, HEAD_DIM))
    v32, v16 = _mod_matmul(x, sc, sh, w_qkv, layer, 2 * da, da, (F32, BF16), tm, da, "v_proj",
                           f32_heads=(da // (2 * HEAD_DIM), 2 * HEAD_DIM))
    return q, k32, v32, k16, v16


def _chunk_id(pos):
    assert CHUNK & (CHUNK - 1) == 0
    return lax.shift_right_logical(pos, CHUNK.bit_length() - 1)


def _sub_ln(o, g_ref, lam_init):
    o = o * lax.rsqrt(jnp.mean(o * o, axis=-1, keepdims=True) + LN_EPS)
    return o * g_ref[...] * (1.0 - lam_init)


def _lane_tile(x, width):
    return jnp.tile(x, (1, width // LANES))


def _online_update(s, v, m_scr, l_scr, acc_scr, idx):
    m_old = m_scr[idx]
    m_new = jnp.maximum(m_old, jnp.max(s, axis=-1, keepdims=True))
    corr = jnp.exp2(m_old - m_new)
    p = jnp.exp2(s - _lane_tile(m_new, s.shape[1]))
    l_scr[idx] = corr * l_scr[idx] + jnp.sum(p, axis=-1, keepdims=True)
    acc_scr[idx] = _lane_tile(corr, v.shape[1]) * acc_scr[idx] + _dot(p.astype(BF16), v)
    m_scr[idx] = m_new


def _attn_prompt_kernel(lam_ref, q_ref, k_ref, v_ref, g_ref, o_ref, m_scr, l_scr, acc_scr,
                        *, tq, sb, lam_init):
    i = pl.program_id(2)
    maps = (slice(0, HEAD_DIM), slice(HEAD_DIM, 2 * HEAD_DIM))
    base = pl.multiple_of(i * tq, tq)

    row_chunk = _chunk_id(lax.broadcasted_iota(jnp.int32, (sb, sb), 0))
    col_chunk = _chunk_id(lax.broadcasted_iota(jnp.int32, (sb, sb), 1))
    diag_keep = col_chunk <= row_chunk
    for r in range(tq // sb):
        rows = slice(r * sb, (r + 1) * sb)
        nk = (r + 1) * sb
        v = v_ref[pl.ds(base, nk), :]
        for idx, cols in enumerate(maps):
            s = _dot_nt(q_ref[rows, cols], k_ref[pl.ds(base, nk), cols])
            s_diag = jnp.where(diag_keep, s[:, r * sb:], MASK_VALUE)
            s = s_diag if r == 0 else jnp.concatenate([s[:, :r * sb], s_diag], axis=1)
            m = jnp.max(s, axis=-1, keepdims=True)
            p = jnp.exp2(s - m)
            m_scr[idx, rows] = jnp.broadcast_to(m, (sb, LANES))
            l_scr[idx, rows] = jnp.broadcast_to(jnp.sum(p, axis=-1, keepdims=True), (sb, LANES))
            acc_scr[idx, rows] = _dot(p.astype(BF16), v)

    def full_tiles(j0, count):
        starts = [pl.multiple_of((j0 + t) * tq, tq) for t in range(count)]
        s = [[_dot_nt(q_ref[:, cols], k_ref[pl.ds(st, tq), cols]) for cols in maps]
             for st in starts]
        for t, st in enumerate(starts):
            v = v_ref[pl.ds(st, tq), :]
            for idx in range(len(maps)):
                _online_update(s[t][idx], v, m_scr, l_scr, acc_scr, idx)

    def pair(jj, carry):
        full_tiles(2 * jj, 2)
        return carry

    lax.fori_loop(0, i // 2, pair, 0)

    @pl.when(i % 2 == 1)
    def _():
        full_tiles(i - 1, 1)

    lam = lam_ref[0, 0]
    hw = acc_scr.shape[2]
    o = (acc_scr[0] * _lane_tile(1.0 / l_scr[0], hw)
         - acc_scr[1] * _lane_tile(lam / l_scr[1], hw))
    o_ref[...] = _sub_ln(o, g_ref, lam_init).astype(o_ref.dtype)


def _attn_prompt(lam, q, k, v, subln_g, tq, sb, lam_init):
    bsz, L, da = q.shape
    hw = 2 * HEAD_DIM
    nh = da // hw
    assert tq % sb == 0 and sb % CHUNK == 0
    return pl.pallas_call(
        functools.partial(_attn_prompt_kernel, tq=tq, sb=sb, lam_init=lam_init),
        grid=(bsz, nh, L // tq),
        in_specs=[
            pl.BlockSpec(memory_space=pltpu.SMEM),
            pl.BlockSpec((None, tq, hw), lambda b, h, i: (b, i, h)),
            pl.BlockSpec((None, L, hw), lambda b, h, i: (b, 0, h)),
            pl.BlockSpec((None, L, hw), lambda b, h, i: (b, 0, h)),
            pl.BlockSpec((1, hw), lambda b, h, i: (0, 0)),
        ],
        out_specs=pl.BlockSpec((None, tq, hw), lambda b, h, i: (b, i, h)),
        out_shape=jax.ShapeDtypeStruct((bsz, L, da), BF16),
        scratch_shapes=[
            pltpu.VMEM((2, tq, LANES), F32),
            pltpu.VMEM((2, tq, LANES), F32),
            pltpu.VMEM((2, tq, hw), F32),
        ],
        compiler_params=_params(("parallel", "parallel", "arbitrary")),
        name="attn_prompt",
    )(lam, q, k, v, subln_g)


def _attn_sample_kernel(lam_ref, q_ref, ck_ref, cv_ref, kn_ref, vn_ref, g_ref, o_ref,
                        *, P, T, heads, lam_init):
    hw = 2 * HEAD_DIM
    qc = _chunk_id(P + lax.broadcasted_iota(jnp.int32, (T, 1), 0))
    keep_c = _chunk_id(lax.broadcasted_iota(jnp.int32, (1, P), 1)) <= qc
    keep_n = _chunk_id(P + lax.broadcasted_iota(jnp.int32, (1, T), 1)) <= qc

    def one_map(cols, cv, vn):
        q = q_ref[:, cols]
        sc = jnp.where(keep_c, _dot_nt(q, ck_ref[:, cols].astype(BF16)), MASK_VALUE)
        sn = jnp.where(keep_n, _dot_nt(q, kn_ref[:, cols]), MASK_VALUE)
        m = jnp.maximum(jnp.max(sc, axis=-1, keepdims=True),
                        jnp.max(sn, axis=-1, keepdims=True))
        pc = jnp.exp2(sc - m)
        pn = jnp.exp2(sn - m)
        l = jnp.sum(pc, axis=-1, keepdims=True) + jnp.sum(pn, axis=-1, keepdims=True)
        return (_dot(pc.astype(BF16), cv) + _dot(pn.astype(BF16), vn)) / l

    for h in range(heads):
        vcols = slice(h * hw, (h + 1) * hw)
        cv = cv_ref[:, vcols].astype(BF16)
        vn = vn_ref[:, vcols]
        o1 = one_map(slice(h * hw, h * hw + HEAD_DIM), cv, vn)
        o2 = one_map(slice(h * hw + HEAD_DIM, (h + 1) * hw), cv, vn)
        o = o1 - lam_ref[0, 0] * o2
        o_ref[:, vcols] = _sub_ln(o, g_ref, lam_init).astype(o_ref.dtype)


def _attn_sample(lam, q, cache_k, cache_v, layer, k_new, v_new, subln_g, heads, lam_init):
    bsz, T, da = q.shape
    P = cache_k.shape[2]
    hw = 2 * HEAD_DIM
    bw = heads * hw
    new_spec = pl.BlockSpec((None, T, bw), lambda b, h: (b, 0, h))
    cache_spec = pl.BlockSpec((None, None, P, bw), lambda b, h: (layer, b, 0, h))
    return pl.pallas_call(
        functools.partial(_attn_sample_kernel, P=P, T=T, heads=heads, lam_init=lam_init),
        grid=(bsz, da // bw),
        in_specs=[
            pl.BlockSpec(memory_space=pltpu.SMEM),
            new_spec, cache_spec, cache_spec, new_spec, new_spec,
            pl.BlockSpec((1, hw), lambda b, h: (0, 0)),
        ],
        out_specs=new_spec,
        out_shape=jax.ShapeDtypeStruct((bsz, T, da), BF16),
        compiler_params=_params(("parallel", "parallel")),
        name="attn_sample",
    )(lam, q, cache_k, cache_v, k_new, v_new, subln_g)


def _ssm_discretise(lam_re, lam_im, log_step, b_re, b_im):
    lr = jnp.minimum(lam_re, -1e-4)
    li = lam_im
    dt = jnp.exp(log_step)[:, None]
    mag = jnp.exp(lr * dt)
    abar_re = mag * jnp.cos(li * dt)
    abar_im = mag * jnp.sin(li * dt)
    nr = abar_re - 1.0
    ni = abar_im
    den = lr * lr + li * li
    kr = (nr * lr + ni * li) / den
    ki = (ni * lr - nr * li) / den
    bbar_re = kr[..., None] * b_re - ki[..., None] * b_im
    bbar_im = kr[..., None] * b_im + ki[..., None] * b_re
    return abar_re, abar_im, bbar_re, bbar_im


def _block_diag(m):
    nblk, gb, r, c = m.shape
    eye = jnp.eye(gb, dtype=m.dtype)
    return jnp.einsum("jgrc,gh->jgrhc", m, eye).reshape(nblk, gb * r, gb * c)


def _ssm_matrices(lam_re, lam_im, log_step, b_re, b_im, c_re, c_im):
    G = lam_re.shape[0]
    gb = SSM_GROUPS_PER_BLOCK
    nblk = G // gb
    abar_re, abar_im, bbar_re, bbar_im = _ssm_discretise(lam_re, lam_im, log_step, b_re, b_im)
    bt_re = bbar_re.reshape(nblk, gb, STATE_P, GROUP_CH).swapaxes(2, 3)
    bt_im = bbar_im.reshape(nblk, gb, STATE_P, GROUP_CH).swapaxes(2, 3)
    bd_b = jnp.concatenate([_block_diag(bt_re), _block_diag(bt_im)], axis=2).astype(BF16)
    ct_re = c_re.reshape(nblk, gb, GROUP_CH, STATE_P).swapaxes(2, 3)
    ct_im = c_im.reshape(nblk, gb, GROUP_CH, STATE_P).swapaxes(2, 3)
    bd_c = jnp.concatenate([_block_diag(ct_re), _block_diag(-ct_im)], axis=1).astype(BF16)
    a_re = abar_re.reshape(nblk, 1, gb * STATE_P)
    a_im = abar_im.reshape(nblk, 1, gb * STATE_P)
    return bd_b, a_re, a_im, bd_c


def _chunk_state(state, fc):
    nbt, w, dff = state.shape
    return state.reshape(nbt, w, dff // fc, fc).transpose(2, 0, 1, 3)


def _unchunk_state(state):
    nchunk, nbt, w, fc = state.shape
    return state.transpose(1, 2, 0, 3).reshape(nbt, w, nchunk * fc)


TM_PROMPT = 512
TN_GLU = 1024
FFN_CHUNK = 512
SSM_TT = 512
SAMPLE_HEADS = 2
TQ = 512
TQ_DIAG = 256


def kernel(x_prompt, x_sample, c_prompt, c_sample, cache_k, cache_v, state_ssm_re, state_ssm_im, state_conv, w_ada, b_ada, ln_g, ln_b, w_up, w_dconv, b_dconv, w_down, w_ssm_in, ssm_lam_re, ssm_lam_im, ssm_log_step, ssm_b_re, ssm_b_im, ssm_c_re, ssm_c_im, ssm_d, w_glu, w_qkv, lam_q1, lam_k1, lam_q2, lam_k2, subln_g, w_o):
    depth = w_ada.shape[0]
    bp, L, d = x_prompt.shape
    bs, T, _ = x_sample.shape
    dff = w_down.shape[1]
    alpha = (2 * depth) ** 0.25
    rows_s = bs * T

    mods = _ada(jnp.concatenate([c_prompt, c_sample], axis=0), w_ada, b_ada)
    w_up16, w_down16 = w_up.astype(BF16), w_down.astype(BF16)
    w_in16, w_glu16 = w_ssm_in.astype(BF16), w_glu.astype(BF16)
    w_qkv16, w_o16 = w_qkv.astype(BF16), w_o.astype(BF16)

    xp = x_prompt
    xs = x_sample.reshape(1, rows_s, d)
    zeros_conv = jnp.zeros((bp, state_conv.shape[2], dff), F32)
    outs = {name: [] for name in ("kp", "vp", "srp", "sip", "cvp", "ks", "vs", "srs", "sis", "cvs")}

    for i in range(depth):
        mp = [m.reshape(bp, 1, d) for m in jnp.split(mods[i, :bp], 6, axis=-1)]
        ms = [jnp.repeat(m, T, axis=0).reshape(1, rows_s, d)
              for m in jnp.split(mods[i, bp:], 6, axis=-1)]
        shp1, scp1, gtp1, shp2, scp2, gtp2 = mp
        shs1, scs1, gts1, shs2, scs2, gts2 = ms
        lng = ln_g[i].reshape(2, 1, d)
        lnb = ln_b[i].reshape(2, 1, d)
        j = i // 2
        if i % 2 == 0:
            bd_b, a_re, a_im, bd_c = _ssm_matrices(
                ssm_lam_re[j], ssm_lam_im[j], ssm_log_step[j], ssm_b_re[j], ssm_b_im[j],
                ssm_c_re[j], ssm_c_im[j])
            d_skip = ssm_d[j].reshape(1, d)
            gp = ssm_lam_re.shape[1] * ssm_lam_re.shape[2]
            (up,) = _mod_matmul(xp, scp1, shp1, w_in16, j, 0, d, (F32,), TM_PROMPT, d, "ssm_in")
            zero_state = jnp.zeros((bp, gp), F32)
            zp, srp, sip = _ssm_scan(up, d_skip, bd_b, a_re, a_im, bd_c, zero_state, zero_state,
                                     SSM_TT)
            xp = _glu_ln(zp, w_glu16, j, xp, gtp1, lng[0], lnb[0], TM_PROMPT, TN_GLU, alpha)
            (us,) = _mod_matmul(xs, scs1, shs1, w_in16, j, 0, d, (F32,), rows_s, d, "ssm_in")
            zs, srs, sis = _ssm_scan(us.reshape(bs, T, d), d_skip, bd_b, a_re, a_im, bd_c,
                                     state_ssm_re[j].reshape(bs, gp),
                                     state_ssm_im[j].reshape(bs, gp), T)
            xs = _glu_ln(zs.reshape(1, rows_s, d), w_glu16, j, xs, gts1, lng[0], lnb[0],
                         rows_s, TN_GLU, alpha)
            st_shape = ssm_lam_re.shape[1:]
            outs["srp"].append(srp.reshape((bp,) + st_shape))
            outs["sip"].append(sip.reshape((bp,) + st_shape))
            outs["srs"].append(srs.reshape((bs,) + st_shape))
            outs["sis"].append(sis.reshape((bs,) + st_shape))
        else:
            lam_init = 0.8 - 0.6 * math.exp(-0.3 * i)
            lam = (jnp.exp(jnp.sum(lam_q1[j] * lam_k1[j])) - jnp.exp(jnp.sum(lam_q2[j] * lam_k2[j]))
                   + lam_init).reshape(1, 1)
            sg = subln_g[j].reshape(1, 2 * HEAD_DIM)
            nsub = cache_k.shape[3]
            nhead = cache_v.shape[3]
            qp, kp, vp, kp16, vp16 = _qkv(xp, scp1, shp1, w_qkv16, j, TM_PROMPT)
            op = _attn_prompt(lam, qp, kp16, vp16, sg, TQ, TQ_DIAG, lam_init)
            xp = _oproj_ln(op, w_o16, j, xp, gtp1, lng[0], lnb[0], TM_PROMPT, d, alpha)
            qs, ksn, vsn, ks16, vs16 = _qkv(xs, scs1, shs1, w_qkv16, j, rows_s)
            P = cache_k.shape[2]
            da = w_o.shape[1]
            na = cache_k.shape[0]
            osm = _attn_sample(lam, qs.reshape(bs, T, da),
                               cache_k.reshape(na, bs, P, da), cache_v.reshape(na, bs, P, da), j,
                               ks16.reshape(bs, T, da), vs16.reshape(bs, T, da), sg,
                               SAMPLE_HEADS, lam_init)
            xs = _oproj_ln(osm.reshape(1, rows_s, da), w_o16, j, xs, gts1, lng[0], lnb[0],
                           rows_s, d, alpha)
            outs["kp"].append(kp.reshape(bp, L, nsub, HEAD_DIM))
            outs["vp"].append(vp.reshape(bp, L, nhead, 2 * HEAD_DIM))
            outs["ks"].append(ksn.reshape(bs, T, nsub, HEAD_DIM))
            outs["vs"].append(vsn.reshape(bs, T, nhead, 2 * HEAD_DIM))

        bconv = b_dconv[i].reshape(1, dff)
        xp, cvp = _ffn(xp, scp2, shp2, gtp2, w_up16, w_dconv[i], bconv, w_down16, i,
                       _chunk_state(zeros_conv, FFN_CHUNK), lng[1], lnb[1],
                       1, TM_PROMPT, FFN_CHUNK, alpha)
        xs, cvs = _ffn(xs, scs2, shs2, gts2, w_up16, w_dconv[i], bconv, w_down16, i,
                       _chunk_state(state_conv[i], FFN_CHUNK), lng[1], lnb[1],
                       bs, T, FFN_CHUNK, alpha)
        outs["cvp"].append(_unchunk_state(cvp))
        outs["cvs"].append(_unchunk_state(cvs))

    return (xp, xs.reshape(bs, T, d),
            jnp.stack(outs["kp"]), jnp.stack(outs["vp"]),
            jnp.stack(outs["srp"]), jnp.stack(outs["sip"]), jnp.stack(outs["cvp"]),
            jnp.stack(outs["ks"]), jnp.stack(outs["vs"]),
            jnp.stack(outs["srs"]), jnp.stack(outs["sis"]), jnp.stack(outs["cvs"]))
```

```python
import functools
import math

import jax
import jax.numpy as jnp
from jax import lax
from jax.experimental import pallas as pl
from jax.experimental.pallas import tpu as pltpu

F32 = jnp.float32
BF16 = jnp.bfloat16

CHUNK = 64
HEAD_DIM = 128
GROUP_CH = 16
STATE_P = 64
LN_EPS = 1e-5
MASK_VALUE = -1e30
Q_SCALE = math.log2(math.e) * HEAD_DIM ** -0.5

SUBLANES = 8
LANES = 128
VMEM_LIMIT = 56 * 1024 * 1024

SSM_GROUPS_PER_BLOCK = LANES // GROUP_CH
SSM_HS = SSM_GROUPS_PER_BLOCK * STATE_P


def _params(sem):
    return pltpu.CompilerParams(dimension_semantics=sem, vmem_limit_bytes=VMEM_LIMIT)


def _gelu(x):
    return 0.5 * x * (1.0 + lax.erf(x * math.sqrt(0.5)))


def _dot(a, b):
    return jnp.dot(a, b, preferred_element_type=F32)


def _dot_nt(a, b):
    return lax.dot_general(a, b, (((1,), (1,)), ((), ())), preferred_element_type=F32)


def _ada_kernel(c_ref, w_ref, b_ref, o_ref):
    c = c_ref[...]
    a = (c * jax.nn.sigmoid(c)).astype(BF16)
    o_ref[...] = _dot(a, w_ref[...].astype(BF16)) + b_ref[...]


def _ada(c_all, w_ada, b_ada, tn=1024):
    depth, d, n = w_ada.shape
    rows = c_all.shape[0]
    return pl.pallas_call(
        _ada_kernel,
        grid=(depth, n // tn),
        in_specs=[
            pl.BlockSpec((rows, d), lambda l, j: (0, 0)),
            pl.BlockSpec((None, d, tn), lambda l, j: (l, 0, j)),
            pl.BlockSpec((None, 1, tn), lambda l, j: (l, 0, j)),
        ],
        out_specs=pl.BlockSpec((None, rows, tn), lambda l, j: (l, 0, j)),
        out_shape=jax.ShapeDtypeStruct((depth, rows, n), F32),
        compiler_params=_params(("parallel", "parallel")),
        name="ada",
    )(c_all, w_ada, b_ada.reshape(depth, 1, n))


def _modulate(x_ref, sc_ref, sh_ref):
    return (x_ref[...] * (1.0 + sc_ref[...]) + sh_ref[...]).astype(BF16)


def _store_heads(o_ref, r, heads):
    nh, hd = heads
    rows = r.shape[0]
    for h in range(nh):
        part = r[:, h * hd:(h + 1) * hd]
        if hd == LANES:
            o_ref[pl.ds(h, rows, stride=nh), :] = part
        else:
            o_ref[:, h, :] = part


def _modmm_kernel(x_ref, sc_ref, sh_ref, w_ref, *rest, bf16_scale, f32_heads):
    *out_refs, h_scr = rest

    @pl.when(pl.program_id(2) == 0)
    def _():
        h_scr[...] = _modulate(x_ref, sc_ref, sh_ref)

    r = _dot(h_scr[...], w_ref[...])
    for o_ref in out_refs:
        if o_ref.dtype != F32:
            o_ref[...] = (r if bf16_scale == 1.0 else r * bf16_scale).astype(o_ref.dtype)
        elif f32_heads is None:
            o_ref[...] = r
        else:
            _store_heads(o_ref, r, f32_heads)


def _mod_spec(mod, tm):
    if mod.shape[1] == 1:
        return pl.BlockSpec((None, 1, mod.shape[2]), lambda b, i, n: (b, 0, 0))
    return pl.BlockSpec((None, tm, mod.shape[2]), lambda b, i, n: (b, i, 0))


def _mod_matmul(x, sc, sh, w, layer, col0, n, out_dtypes, tm, tn, name, bf16_scale=1.0,
                f32_heads=None):
    bsz, L, d = x.shape
    assert col0 % tn == 0 and n % tn == 0
    jb = col0 // tn
    nt = L // tm
    flat_spec = pl.BlockSpec((None, tm, tn), lambda b, i, j: (b, i, j))
    out_specs, out_shape = [], []
    for dt in out_dtypes:
        if dt != F32 or f32_heads is None:
            out_specs.append(flat_spec)
            out_shape.append(jax.ShapeDtypeStruct((bsz, L, n), dt))
            continue
        nh, hd = f32_heads
        assert tn == n == nh * hd
        if hd == LANES:
            out_specs.append(pl.BlockSpec((tm * nh, hd), lambda b, i, j: (b * nt + i, 0)))
            out_shape.append(jax.ShapeDtypeStruct((bsz * L * nh, hd), dt))
        else:
            out_specs.append(pl.BlockSpec((None, tm, nh, hd), lambda b, i, j: (b, i, 0, 0)))
            out_shape.append(jax.ShapeDtypeStruct((bsz, L, nh, hd), dt))
    return pl.pallas_call(
        functools.partial(_modmm_kernel, bf16_scale=bf16_scale, f32_heads=f32_heads),
        grid=(bsz, nt, n // tn),
        in_specs=[
            pl.BlockSpec((None, tm, d), lambda b, i, j: (b, i, 0)),
            _mod_spec(sc, tm),
            _mod_spec(sh, tm),
            pl.BlockSpec((None, d, tn), lambda b, i, j: (layer, 0, jb + j)),
        ],
        out_specs=out_specs,
        out_shape=out_shape,
        scratch_shapes=[pltpu.VMEM((tm, d), BF16)],
        compiler_params=_params(("parallel", "parallel", "arbitrary")),
        name=name,
    )(x, sc, sh, w)


def _ssm_scan_tile(u_ref, bdb_ref, are_ref, aim_ref, tb, x, h_scr, tt, cb):
    hs = SSM_HS
    for c in range(cb):
        lanes = slice(c * LANES, (c + 1) * LANES)
        for b in range(SUBLANES):
            tb[c, pl.ds(b, tt, stride=SUBLANES), :] = u_ref[b, :, lanes]
        x[c] = _dot(tb[c].astype(BF16), bdb_ref[c]).reshape(tt, SUBLANES, 2 * hs)
    for c in range(cb):
        ar = jnp.broadcast_to(are_ref[c], (SUBLANES, hs))
        ai = jnp.broadcast_to(aim_ref[c], (SUBLANES, hs))
        hr = h_scr[c, :, :hs]
        hi = h_scr[c, :, hs:]
        for t in range(tt):
            hr, hi = ((ar * hr - ai * hi) + x[c, t, :, :hs],
                      (ar * hi + ai * hr) + x[c, t, :, hs:])
            x[c, t, :, :hs] = hr
            x[c, t, :, hs:] = hi
        h_scr[c, :, :hs] = hr
        h_scr[c, :, hs:] = hi


def _ssm_out_tile(d_ref, bdc_ref, tb, x, z_ref, tt, cb):
    for c in range(cb):
        lanes = slice(c * LANES, (c + 1) * LANES)
        hb = x[c].reshape(tt * SUBLANES, 2 * SSM_HS).astype(BF16)
        y = _dot(hb, bdc_ref[c])
        tb[c] = _gelu(y + d_ref[:, lanes] * tb[c])
        for b in range(SUBLANES):
            z_ref[b, :, lanes] = tb[c, pl.ds(b, tt, stride=SUBLANES), :].astype(z_ref.dtype)


def _ssm_kernel(u_ref, d_ref, bdb_ref, are_ref, aim_ref, bdc_ref, sre_ref, sim_ref,
                z_ref, ore_ref, oim_ref, tb0, tb1, x0, x1, h_scr, *, tt, nt, cb):
    i = pl.program_id(2)
    hs = SSM_HS
    bufs = ((tb0, x0), (tb1, x1))
    scan = functools.partial(_ssm_scan_tile, u_ref, bdb_ref, are_ref, aim_ref)
    out = functools.partial(_ssm_out_tile, d_ref, bdc_ref)

    @pl.when(i == 0)
    def _():
        for c in range(cb):
            h_scr[c, :, :hs] = sre_ref[:, c * hs:(c + 1) * hs]
            h_scr[c, :, hs:] = sim_ref[:, c * hs:(c + 1) * hs]
        scan(*bufs[0], h_scr, tt, cb)

    for parity in range(2):
        @pl.when((i > 0) & (i < nt) & (i % 2 == parity))
        def _():
            scan(*bufs[parity], h_scr, tt, cb)
            out(*bufs[1 - parity], z_ref, tt, cb)

    @pl.when(i == nt)
    def _():
        out(*bufs[(nt - 1) % 2], z_ref, tt, cb)
        for c in range(cb):
            ore_ref[:, c * hs:(c + 1) * hs] = h_scr[c, :, :hs]
            oim_ref[:, c * hs:(c + 1) * hs] = h_scr[c, :, hs:]


def _ssm_scan(u, d_skip, bd_b, a_re, a_im, bd_c, s_re, s_im, tt, cb):
    bsz, L, d = u.shape
    nblk = d // LANES
    hs = SSM_HS
    nt = L // tt
    bw = cb * LANES
    st_spec = pl.BlockSpec((SUBLANES, cb * hs), lambda j, g, i: (g, j))
    in_spec = pl.BlockSpec((SUBLANES, tt, bw),
                           lambda j, g, i: (g, jnp.minimum(i, nt - 1), j))
    out_spec = pl.BlockSpec((SUBLANES, tt, bw),
                            lambda j, g, i: (g, jnp.maximum(i - 1, 0), j))
    return pl.pallas_call(
        functools.partial(_ssm_kernel, tt=tt, nt=nt, cb=cb),
        grid=(nblk // cb, bsz // SUBLANES, nt + 1),
        in_specs=[
            in_spec,
            pl.BlockSpec((1, bw), lambda j, g, i: (0, j)),
            pl.BlockSpec((cb, LANES, 2 * hs), lambda j, g, i: (j, 0, 0)),
            pl.BlockSpec((cb, 1, hs), lambda j, g, i: (j, 0, 0)),
            pl.BlockSpec((cb, 1, hs), lambda j, g, i: (j, 0, 0)),
            pl.BlockSpec((cb, 2 * hs, LANES), lambda j, g, i: (j, 0, 0)),
            st_spec,
            st_spec,
        ],
        out_specs=[out_spec, st_spec, st_spec],
        out_shape=[
            jax.ShapeDtypeStruct((bsz, L, d), BF16),
            jax.ShapeDtypeStruct(s_re.shape, F32),
            jax.ShapeDtypeStruct(s_im.shape, F32),
        ],
        scratch_shapes=[
            pltpu.VMEM((cb, tt * SUBLANES, LANES), F32),
            pltpu.VMEM((cb, tt * SUBLANES, LANES), F32),
            pltpu.VMEM((cb, tt, SUBLANES, 2 * hs), F32),
            pltpu.VMEM((cb, tt, SUBLANES, 2 * hs), F32),
            pltpu.VMEM((cb, SUBLANES, 2 * hs), F32),
        ],
        compiler_params=_params(("parallel", "parallel", "arbitrary")),
        name="ssm_scan",
    )(u, d_skip, bd_b, a_re, a_im, bd_c, s_re, s_im)


def _layer_norm_chunks(acc_scr, lng_ref, lnb_ref, o_ref, nchunk, tn):
    d = nchunk * tn
    parts = [acc_scr[c] for c in range(nchunk)]
    mu = sum(jnp.sum(p, axis=-1, keepdims=True) for p in parts) * (1.0 / d)
    cen = [p - mu for p in parts]
    var = sum(jnp.sum(q * q, axis=-1, keepdims=True) for q in cen) * (1.0 / d)
    inv = lax.rsqrt(var + LN_EPS)
    for c in range(nchunk):
        sl = slice(c * tn, (c + 1) * tn)
        o_ref[:, sl] = cen[c] * inv * lng_ref[:, sl] + lnb_ref[:, sl]


def _glu_ln_kernel(z_ref, wa_ref, wg_ref, x_ref, gt_ref, lng_ref, lnb_ref,
                   o_ref, acc_scr, *, nchunk, tn, alpha):
    c = pl.program_id(2)
    z = z_ref[...]
    a = _dot(z, wa_ref[...])
    g = _dot(z, wg_ref[...])
    m = a * jax.nn.sigmoid(g)
    acc_scr[c] = alpha * x_ref[...] + (1.0 + gt_ref[...]) * m

    @pl.when(c == nchunk - 1)
    def _():
        _layer_norm_chunks(acc_scr, lng_ref, lnb_ref, o_ref, nchunk, tn)


def _glu_ln(z, w_glu, layer, x, gate, ln_g, ln_b, tm, tn, alpha):
    bsz, L, d = x.shape
    nchunk = d // tn
    if gate.shape[1] == 1:
        gt_spec = pl.BlockSpec((None, 1, tn), lambda b, i, c: (b, 0, c))
    else:
        gt_spec = pl.BlockSpec((None, tm, tn), lambda b, i, c: (b, i, c))
    vec = pl.BlockSpec((1, d), lambda b, i, c: (0, 0))
    return pl.pallas_call(
        functools.partial(_glu_ln_kernel, nchunk=nchunk, tn=tn, alpha=alpha),
        grid=(bsz, L // tm, nchunk),
        in_specs=[
            pl.BlockSpec((None, tm, d), lambda b, i, c: (b, i, 0)),
            pl.BlockSpec((None, d, tn), lambda b, i, c: (layer, 0, c)),
            pl.BlockSpec((None, d, tn), lambda b, i, c: (layer, 0, nchunk + c)),
            pl.BlockSpec((None, tm, tn), lambda b, i, c: (b, i, c)),
            gt_spec,
            vec,
            vec,
        ],
        out_specs=pl.BlockSpec((None, tm, d), lambda b, i, c: (b, i, 0)),
        out_shape=jax.ShapeDtypeStruct((bsz, L, d), F32),
        scratch_shapes=[pltpu.VMEM((nchunk, tm, tn), F32)],
        compiler_params=_params(("parallel", "parallel", "arbitrary")),
        name="glu_ln",
    )(z, w_glu, w_glu, x, gate, ln_g, ln_b)


def _oproj_ln_kernel(o_in_ref, w_ref, x_ref, gt_ref, lng_ref, lnb_ref, o_ref, acc_scr,
                     *, nchunk, tn, alpha):
    c = pl.program_id(2)
    m = _dot(o_in_ref[...], w_ref[...])
    acc_scr[c] = alpha * x_ref[...] + (1.0 + gt_ref[...]) * m

    @pl.when(c == nchunk - 1)
    def _():
        _layer_norm_chunks(acc_scr, lng_ref, lnb_ref, o_ref, nchunk, tn)


def _oproj_ln(o_in, w_o, layer, x, gate, ln_g, ln_b, tm, tn, alpha):
    bsz, L, d = x.shape
    nchunk = d // tn
    if gate.shape[1] == 1:
        gt_spec = pl.BlockSpec((None, 1, tn), lambda b, i, c: (b, 0, c))
    else:
        gt_spec = pl.BlockSpec((None, tm, tn), lambda b, i, c: (b, i, c))
    vec = pl.BlockSpec((1, d), lambda b, i, c: (0, 0))
    return pl.pallas_call(
        functools.partial(_oproj_ln_kernel, nchunk=nchunk, tn=tn, alpha=alpha),
        grid=(bsz, L // tm, nchunk),
        in_specs=[
            pl.BlockSpec((None, tm, o_in.shape[2]), lambda b, i, c: (b, i, 0)),
            pl.BlockSpec((None, w_o.shape[1], tn), lambda b, i, c: (layer, 0, c)),
            pl.BlockSpec((None, tm, tn), lambda b, i, c: (b, i, c)),
            gt_spec,
            vec,
            vec,
        ],
        out_specs=pl.BlockSpec((None, tm, d), lambda b, i, c: (b, i, 0)),
        out_shape=jax.ShapeDtypeStruct((bsz, L, d), F32),
        scratch_shapes=[pltpu.VMEM((nchunk, tm, tn), F32)],
        compiler_params=_params(("parallel", "parallel", "arbitrary")),
        name="oproj_ln",
    )(o_in, w_o, x, gate, ln_g, ln_b)


def _ffn_kernel(x_ref, sc_ref, sh_ref, gt_ref, wg_ref, wv_ref, wc_ref, bc_ref, wd_ref,
                st_ref, lng_ref, lnb_ref, o_ref, cv_ref, h_scr, acc_scr, prev_scr,
                *, nb, T, nchunk, alpha):
    i = pl.program_id(1)
    c = pl.program_id(2)
    M = nb * T
    fc = wg_ref.shape[1]

    @pl.when(c == 0)
    def _():
        h_scr[...] = _modulate(x_ref, sc_ref, sh_ref)
        acc_scr[...] = jnp.zeros(acc_scr.shape, F32)

    @pl.when(i == 0)
    def _():
        prev_scr[...] = st_ref[c]

    @pl.when(i > 0)
    def _():
        prev_scr[...] = cv_ref[c]

    h = h_scr[...]
    g = _dot(h, wg_ref[...])
    v = _dot(h, wv_ref[...])

    row = lax.broadcasted_iota(jnp.int32, (M, 1), 0)
    if nb == 1:
        tpos = row
        p0 = prev_scr[0, 0:1, :]
        p1 = prev_scr[0, 1:2, :]
    else:
        tpos = lax.rem(row, T)
        p0 = jnp.broadcast_to(prev_scr[:, 0:1, :], (nb, T, fc)).reshape(M, fc)
        p1 = jnp.broadcast_to(prev_scr[:, 1:2, :], (nb, T, fc)).reshape(M, fc)
    s1 = jnp.where(tpos == 0, p1, pltpu.roll(g, 1, 0))
    s2 = jnp.where(tpos == 0, p0, jnp.where(tpos == 1, p1, pltpu.roll(g, 2, 0)))
    conv = bc_ref[...] + s2 * wc_ref[0:1, :] + s1 * wc_ref[1:2, :] + g * wc_ref[2:3, :]
    act = (_gelu(conv) * v).astype(BF16)
    acc_scr[...] += _dot(act, wd_ref[...])
    cv_ref[c] = g.reshape(nb, T, fc)[:, T - 2:, :]

    @pl.when(c == nchunk - 1)
    def _():
        r = alpha * x_ref[...] + (1.0 + gt_ref[...]) * acc_scr[...]
        mu = jnp.mean(r, axis=-1, keepdims=True)
        cen = r - mu
        var = jnp.mean(cen * cen, axis=-1, keepdims=True)
        o_ref[...] = cen * lax.rsqrt(var + LN_EPS) * lng_ref[...] + lnb_ref[...]


def _ffn(x, sc, sh, gate, w_up, w_conv, b_conv, w_down, layer, conv_state, ln_g, ln_b,
         nb, T, fc, alpha):
    S, R, d = x.shape
    dff = w_down.shape[1]
    nchunk = dff // fc
    tm = nb * T
    ntile = R // tm
    assert nb == 1 or ntile == 1

    def mod_spec(mod):
        if mod.shape[1] == 1:
            return pl.BlockSpec((None, 1, d), lambda s, i, c: (s, 0, 0))
        return pl.BlockSpec((None, tm, d), lambda s, i, c: (s, i, 0))

    vec = pl.BlockSpec((1, d), lambda s, i, c: (0, 0))
    st_spec = pl.BlockSpec((nchunk, nb, 2, fc), lambda s, i, c: (0, s, 0, 0))
    return pl.pallas_call(
        functools.partial(_ffn_kernel, nb=nb, T=T, nchunk=nchunk, alpha=alpha),
        grid=(S, ntile, nchunk),
        in_specs=[
            pl.BlockSpec((None, tm, d), lambda s, i, c: (s, i, 0)),
            mod_spec(sc),
            mod_spec(sh),
            mod_spec(gate),
            pl.BlockSpec((None, d, fc), lambda s, i, c: (layer, 0, c)),
            pl.BlockSpec((None, d, fc), lambda s, i, c: (layer, 0, nchunk + c)),
            pl.BlockSpec((w_conv.shape[0], fc), lambda s, i, c: (0, c)),
            pl.BlockSpec((1, fc), lambda s, i, c: (0, c)),
            pl.BlockSpec((None, fc, d), lambda s, i, c: (layer, c, 0)),
            st_spec,
            vec,
            vec,
        ],
        out_specs=[
            pl.BlockSpec((None, tm, d), lambda s, i, c: (s, i, 0)),
            st_spec,
        ],
        out_shape=[
            jax.ShapeDtypeStruct((S, R, d), F32),
            jax.ShapeDtypeStruct(conv_state.shape, F32),
        ],
        scratch_shapes=[
            pltpu.VMEM((tm, d), BF16),
            pltpu.VMEM((tm, d), F32),
            pltpu.VMEM((nb, 2, fc), F32),
        ],
        compiler_params=_params(("parallel", "arbitrary", "arbitrary")),
        name="conv_ffn",
    )(x, sc, sh, gate, w_up, w_up, w_conv, b_conv, w_down, conv_state, ln_g, ln_b)


def _qkv(x, sc, sh, w_qkv, layer, tm):
    da = w_qkv.shape[2] // 3
    (q,) = _mod_matmul(x, sc, sh, w_qkv, layer, 0, da, (BF16,), tm, da, "q_proj", Q_SCALE)
    k32, k16 = _mod_matmul(x, sc, sh, w_qkv, layer, da, da, (F32, BF16), tm, da, "k_proj",
                           f32_heads=(da // HEAD_DIM, HEAD_DIM))
    v32, v16 = _mod_matmul(x, sc, sh, w_qkv, layer, 2 * da, da, (F32, BF16), tm, da, "v_proj",
                           f32_heads=(da // (2 * HEAD_DIM), 2 * HEAD_DIM))
    return q, k32, v32, k16, v16


def _chunk_id(pos):
    assert CHUNK & (CHUNK - 1) == 0
    return lax.shift_right_logical(pos, CHUNK.bit_length() - 1)


def _sub_ln(o, g_ref, lam_init):
    o = o * lax.rsqrt(jnp.mean(o * o, axis=-1, keepdims=True) + LN_EPS)
    return o * g_ref[...] * (1.0 - lam_init)


def _lane_tile(x, width):
    return jnp.tile(x, (1, width // LANES))


def _online_update(s, v, m_scr, l_scr, acc_scr, idx):
    m_old = m_scr[idx]
    m_new = jnp.maximum(m_old, jnp.max(s, axis=-1, keepdims=True))
    corr = jnp.exp2(m_old - m_new)
    p = jnp.exp2(s - _lane_tile(m_new, s.shape[1]))
    l_scr[idx] = corr * l_scr[idx] + jnp.sum(p, axis=-1, keepdims=True)
    acc_scr[idx] = _lane_tile(corr, v.shape[1]) * acc_scr[idx] + _dot(p.astype(BF16), v)
    m_scr[idx] = m_new


def _attn_prompt_kernel(lam_ref, q_ref, k_ref, v_ref, g_ref, o_ref, m_scr, l_scr, acc_scr,
                        *, tq, sb, lam_init):
    i = pl.program_id(2)
    maps = (slice(0, HEAD_DIM), slice(HEAD_DIM, 2 * HEAD_DIM))
    base = pl.multiple_of(i * tq, tq)

    row_chunk = _chunk_id(lax.broadcasted_iota(jnp.int32, (sb, sb), 0))
    col_chunk = _chunk_id(lax.broadcasted_iota(jnp.int32, (sb, sb), 1))
    diag_keep = col_chunk <= row_chunk
    for r in range(tq // sb):
        rows = slice(r * sb, (r + 1) * sb)
        nk = (r + 1) * sb
        v = v_ref[pl.ds(base, nk), :]
        for idx, cols in enumerate(maps):
            s = _dot_nt(q_ref[rows, cols], k_ref[pl.ds(base, nk), cols])
            s_diag = jnp.where(diag_keep, s[:, r * sb:], MASK_VALUE)
            s = s_diag if r == 0 else jnp.concatenate([s[:, :r * sb], s_diag], axis=1)
            m = jnp.max(s, axis=-1, keepdims=True)
            p = jnp.exp2(s - m)
            m_scr[idx, rows] = jnp.broadcast_to(m, (sb, LANES))
            l_scr[idx, rows] = jnp.broadcast_to(jnp.sum(p, axis=-1, keepdims=True), (sb, LANES))
            acc_scr[idx, rows] = _dot(p.astype(BF16), v)

    def full_tiles(j0, count):
        starts = [pl.multiple_of((j0 + t) * tq, tq) for t in range(count)]
        s = [[_dot_nt(q_ref[:, cols], k_ref[pl.ds(st, tq), cols]) for cols in maps]
             for st in starts]
        for t, st in enumerate(starts):
            v = v_ref[pl.ds(st, tq), :]
            for idx in range(len(maps)):
                _online_update(s[t][idx], v, m_scr, l_scr, acc_scr, idx)

    def pair(jj, carry):
        full_tiles(2 * jj, 2)
        return carry

    lax.fori_loop(0, i // 2, pair, 0)

    @pl.when(i % 2 == 1)
    def _():
        full_tiles(i - 1, 1)

    lam = lam_ref[0, 0]
    hw = acc_scr.shape[2]
    o = (acc_scr[0] * _lane_tile(1.0 / l_scr[0], hw)
         - acc_scr[1] * _lane_tile(lam / l_scr[1], hw))
    o_ref[...] = _sub_ln(o, g_ref, lam_init).astype(o_ref.dtype)


def _attn_prompt(lam, q, k, v, subln_g, tq, sb, lam_init):
    bsz, L, da = q.shape
    hw = 2 * HEAD_DIM
    nh = da // hw
    assert tq % sb == 0 and sb % CHUNK == 0
    return pl.pallas_call(
        functools.partial(_attn_prompt_kernel, tq=tq, sb=sb, lam_init=lam_init),
        grid=(bsz, nh, L // tq),
        in_specs=[
            pl.BlockSpec(memory_space=pltpu.SMEM),
            pl.BlockSpec((None, tq, hw), lambda b, h, i: (b, i, h)),
            pl.BlockSpec((None, L, hw), lambda b, h, i: (b, 0, h)),
            pl.BlockSpec((None, L, hw), lambda b, h, i: (b, 0, h)),
            pl.BlockSpec((1, hw), lambda b, h, i: (0, 0)),
        ],
        out_specs=pl.BlockSpec((None, tq, hw), lambda b, h, i: (b, i, h)),
        out_shape=jax.ShapeDtypeStruct((bsz, L, da), BF16),
        scratch_shapes=[
            pltpu.VMEM((2, tq, LANES), F32),
            pltpu.VMEM((2, tq, LANES), F32),
            pltpu.VMEM((2, tq, hw), F32),
        ],
        compiler_params=_params(("parallel", "parallel", "arbitrary")),
        name="attn_prompt",
    )(lam, q, k, v, subln_g)


def _attn_sample_kernel(lam_ref, q_ref, ck_ref, cv_ref, kn_ref, vn_ref, g_ref, o_ref,
                        *, P, T, heads, lam_init):
    hw = 2 * HEAD_DIM
    qc = _chunk_id(P + lax.broadcasted_iota(jnp.int32, (T, 1), 0))
    keep_c = _chunk_id(lax.broadcasted_iota(jnp.int32, (1, P), 1)) <= qc
    keep_n = _chunk_id(P + lax.broadcasted_iota(jnp.int32, (1, T), 1)) <= qc

    def one_map(cols, cv, vn):
        q = q_ref[:, cols]
        sc = jnp.where(keep_c, _dot_nt(q, ck_ref[:, cols]), MASK_VALUE)
        sn = jnp.where(keep_n, _dot_nt(q, kn_ref[:, cols]), MASK_VALUE)
        m = jnp.maximum(jnp.max(sc, axis=-1, keepdims=True),
                        jnp.max(sn, axis=-1, keepdims=True))
        pc = jnp.exp2(sc - m)
        pn = jnp.exp2(sn - m)
        l = jnp.sum(pc, axis=-1, keepdims=True) + jnp.sum(pn, axis=-1, keepdims=True)
        return (_dot(pc.astype(BF16), cv) + _dot(pn.astype(BF16), vn)) / l

    for h in range(heads):
        vcols = slice(h * hw, (h + 1) * hw)
        cv = cv_ref[:, vcols]
        vn = vn_ref[:, vcols]
        o1 = one_map(slice(h * hw, h * hw + HEAD_DIM), cv, vn)
        o2 = one_map(slice(h * hw + HEAD_DIM, (h + 1) * hw), cv, vn)
        o = o1 - lam_ref[0, 0] * o2
        o_ref[:, vcols] = _sub_ln(o, g_ref, lam_init).astype(o_ref.dtype)


def _attn_sample(lam, q, cache_k, cache_v, layer, k_new, v_new, subln_g, heads, lam_init):
    bsz, T, da = q.shape
    P = cache_k.shape[2]
    hw = 2 * HEAD_DIM
    bw = heads * hw
    new_spec = pl.BlockSpec((None, T, bw), lambda b, h: (b, 0, h))
    cache_spec = pl.BlockSpec((None, None, P, bw), lambda b, h: (layer, b, 0, h))
    return pl.pallas_call(
        functools.partial(_attn_sample_kernel, P=P, T=T, heads=heads, lam_init=lam_init),
        grid=(bsz, da // bw),
        in_specs=[
            pl.BlockSpec(memory_space=pltpu.SMEM),
            new_spec, cache_spec, cache_spec, new_spec, new_spec,
            pl.BlockSpec((1, hw), lambda b, h: (0, 0)),
        ],
        out_specs=new_spec,
        out_shape=jax.ShapeDtypeStruct((bsz, T, da), BF16),
        compiler_params=_params(("parallel", "parallel")),
        name="attn_sample",
    )(lam, q, cache_k, cache_v, k_new, v_new, subln_g)


def _ssm_discretise(lam_re, lam_im, log_step, b_re, b_im):
    lr = jnp.minimum(lam_re, -1e-4)
    li = lam_im
    dt = jnp.exp(log_step)[:, None]
    mag = jnp.exp(lr * dt)
    abar_re = mag * jnp.cos(li * dt)
    abar_im = mag * jnp.sin(li * dt)
    nr = abar_re - 1.0
    ni = abar_im
    den = lr * lr + li * li
    kr = (nr * lr + ni * li) / den
    ki = (ni * lr - nr * li) / den
    bbar_re = kr[..., None] * b_re - ki[..., None] * b_im
    bbar_im = kr[..., None] * b_im + ki[..., None] * b_re
    return abar_re, abar_im, bbar_re, bbar_im


def _block_diag(m):
    nblk, gb, r, c = m.shape
    eye = jnp.eye(gb, dtype=m.dtype)
    return jnp.einsum("jgrc,gh->jgrhc", m, eye).reshape(nblk, gb * r, gb * c)


def _ssm_matrices(lam_re, lam_im, log_step, b_re, b_im, c_re, c_im):
    G = lam_re.shape[0]
    gb = SSM_GROUPS_PER_BLOCK
    nblk = G // gb
    abar_re, abar_im, bbar_re, bbar_im = _ssm_discretise(lam_re, lam_im, log_step, b_re, b_im)
    bt_re = bbar_re.reshape(nblk, gb, STATE_P, GROUP_CH).swapaxes(2, 3)
    bt_im = bbar_im.reshape(nblk, gb, STATE_P, GROUP_CH).swapaxes(2, 3)
    bd_b = jnp.concatenate([_block_diag(bt_re), _block_diag(bt_im)], axis=2).astype(BF16)
    ct_re = c_re.reshape(nblk, gb, GROUP_CH, STATE_P).swapaxes(2, 3)
    ct_im = c_im.reshape(nblk, gb, GROUP_CH, STATE_P).swapaxes(2, 3)
    bd_c = jnp.concatenate([_block_diag(ct_re), _block_diag(-ct_im)], axis=1).astype(BF16)
    a_re = abar_re.reshape(nblk, 1, gb * STATE_P)
    a_im = abar_im.reshape(nblk, 1, gb * STATE_P)
    return bd_b, a_re, a_im, bd_c


def _chunk_state(state, fc):
    nbt, w, dff = state.shape
    return state.reshape(nbt, w, dff // fc, fc).transpose(2, 0, 1, 3)


def _unchunk_state(state):
    nchunk, nbt, w, fc = state.shape
    return state.transpose(1, 2, 0, 3).reshape(nbt, w, nchunk * fc)


TM_PROMPT = 512
TN_GLU = 1024
FFN_CHUNK = 512
SSM_TT = 128
SSM_CB = 4
SAMPLE_HEADS = 2
TQ = 512
TQ_DIAG = 256


def kernel(x_prompt, x_sample, c_prompt, c_sample, cache_k, cache_v, state_ssm_re, state_ssm_im, state_conv, w_ada, b_ada, ln_g, ln_b, w_up, w_dconv, b_dconv, w_down, w_ssm_in, ssm_lam_re, ssm_lam_im, ssm_log_step, ssm_b_re, ssm_b_im, ssm_c_re, ssm_c_im, ssm_d, w_glu, w_qkv, lam_q1, lam_k1, lam_q2, lam_k2, subln_g, w_o):
    depth = w_ada.shape[0]
    bp, L, d = x_prompt.shape
    bs, T, _ = x_sample.shape
    dff = w_down.shape[1]
    alpha = (2 * depth) ** 0.25
    rows_s = bs * T

    mods = _ada(jnp.concatenate([c_prompt, c_sample], axis=0), w_ada, b_ada)
    w_up16, w_down16 = w_up.astype(BF16), w_down.astype(BF16)
    w_in16, w_glu16 = w_ssm_in.astype(BF16), w_glu.astype(BF16)
    w_qkv16, w_o16 = w_qkv.astype(BF16), w_o.astype(BF16)

    xp = x_prompt
    xs = x_sample.reshape(1, rows_s, d)
    zeros_conv = jnp.zeros((bp, state_conv.shape[2], dff), F32)
    outs = {name: [] for name in ("kp", "vp", "srp", "sip", "cvp", "ks", "vs", "srs", "sis", "cvs")}

    for i in range(depth):
        mp = [m.reshape(bp, 1, d) for m in jnp.split(mods[i, :bp], 6, axis=-1)]
        ms = [jnp.repeat(m, T, axis=0).reshape(1, rows_s, d)
              for m in jnp.split(mods[i, bp:], 6, axis=-1)]
        shp1, scp1, gtp1, shp2, scp2, gtp2 = mp
        shs1, scs1, gts1, shs2, scs2, gts2 = ms
        lng = ln_g[i].reshape(2, 1, d)
        lnb = ln_b[i].reshape(2, 1, d)
        j = i // 2
        if i % 2 == 0:
            bd_b, a_re, a_im, bd_c = _ssm_matrices(
                ssm_lam_re[j], ssm_lam_im[j], ssm_log_step[j], ssm_b_re[j], ssm_b_im[j],
                ssm_c_re[j], ssm_c_im[j])
            d_skip = ssm_d[j].reshape(1, d)
            gp = ssm_lam_re.shape[1] * ssm_lam_re.shape[2]
            (up,) = _mod_matmul(xp, scp1, shp1, w_in16, j, 0, d, (F32,), TM_PROMPT, d, "ssm_in")
            zero_state = jnp.zeros((bp, gp), F32)
            zp, srp, sip = _ssm_scan(up, d_skip, bd_b, a_re, a_im, bd_c, zero_state, zero_state,
                                     SSM_TT, SSM_CB)
            xp = _glu_ln(zp, w_glu16, j, xp, gtp1, lng[0], lnb[0], TM_PROMPT, TN_GLU, alpha)
            (us,) = _mod_matmul(xs, scs1, shs1, w_in16, j, 0, d, (F32,), rows_s, d, "ssm_in")
            zs, srs, sis = _ssm_scan(us.reshape(bs, T, d), d_skip, bd_b, a_re, a_im, bd_c,
                                     state_ssm_re[j].reshape(bs, gp),
                                     state_ssm_im[j].reshape(bs, gp), T, SSM_CB)
            xs = _glu_ln(zs.reshape(1, rows_s, d), w_glu16, j, xs, gts1, lng[0], lnb[0],
                         rows_s, TN_GLU, alpha)
            st_shape = ssm_lam_re.shape[1:]
            outs["srp"].append(srp.reshape((bp,) + st_shape))
            outs["sip"].append(sip.reshape((bp,) + st_shape))
            outs["srs"].append(srs.reshape((bs,) + st_shape))
            outs["sis"].append(sis.reshape((bs,) + st_shape))
        else:
            lam_init = 0.8 - 0.6 * math.exp(-0.3 * i)
            lam = (jnp.exp(jnp.sum(lam_q1[j] * lam_k1[j])) - jnp.exp(jnp.sum(lam_q2[j] * lam_k2[j]))
                   + lam_init).reshape(1, 1)
            sg = subln_g[j].reshape(1, 2 * HEAD_DIM)
            nsub = cache_k.shape[3]
            nhead = cache_v.shape[3]
            qp, kp, vp, kp16, vp16 = _qkv(xp, scp1, shp1, w_qkv16, j, TM_PROMPT)
            op = _attn_prompt(lam, qp, kp16, vp16, sg, TQ, TQ_DIAG, lam_init)
            xp = _oproj_ln(op, w_o16, j, xp, gtp1, lng[0], lnb[0], TM_PROMPT, d, alpha)
            qs, ksn, vsn, ks16, vs16 = _qkv(xs, scs1, shs1, w_qkv16, j, rows_s)
            P = cache_k.shape[2]
            da = w_o.shape[1]
            na = cache_k.shape[0]
            osm = _attn_sample(lam, qs.reshape(bs, T, da),
                               cache_k.astype(BF16).reshape(na, bs, P, da),
                               cache_v.astype(BF16).reshape(na, bs, P, da), j,
                               ks16.reshape(bs, T, da), vs16.reshape(bs, T, da), sg,
                               SAMPLE_HEADS, lam_init)
            xs = _oproj_ln(osm.reshape(1, rows_s, da), w_o16, j, xs, gts1, lng[0], lnb[0],
                           rows_s, d, alpha)
            outs["kp"].append(kp.reshape(bp, L, nsub, HEAD_DIM))
            outs["vp"].append(vp.reshape(bp, L, nhead, 2 * HEAD_DIM))
            outs["ks"].append(ksn.reshape(bs, T, nsub, HEAD_DIM))
            outs["vs"].append(vsn.reshape(bs, T, nhead, 2 * HEAD_DIM))

        bconv = b_dconv[i].reshape(1, dff)
        xp, cvp = _ffn(xp, scp2, shp2, gtp2, w_up16, w_dconv[i], bconv, w_down16, i,
                       _chunk_state(zeros_conv, FFN_CHUNK), lng[1], lnb[1],
                       1, TM_PROMPT, FFN_CHUNK, alpha)
        xs, cvs = _ffn(xs, scs2, shs2, gts2, w_up16, w_dconv[i], bconv, w_down16, i,
                       _chunk_state(state_conv[i], FFN_CHUNK), lng[1], lnb[1],
                       bs, T, FFN_CHUNK, alpha)
        outs["cvp"].append(_unchunk_state(cvp))
        outs["cvs"].append(_unchunk_state(cvs))

    return (xp, xs.reshape(bs, T, d),
            jnp.stack(outs["kp"]), jnp.stack(outs["vp"]),
            jnp.stack(outs["srp"]), jnp.stack(outs["sip"]), jnp.stack(outs["cvp"]),
            jnp.stack(outs["ks"]), jnp.stack(outs["vs"]),
            jnp.stack(outs["srs"]), jnp.stack(outs["sis"]), jnp.stack(outs["cvs"]))
```

```python
import functools
import math

import jax
import jax.numpy as jnp
from jax import lax
from jax.experimental import pallas as pl
from jax.experimental.pallas import tpu as pltpu

F32 = jnp.float32
BF16 = jnp.bfloat16

CHUNK = 64
HEAD_DIM = 128
GROUP_CH = 16
STATE_P = 64
LN_EPS = 1e-5
MASK_VALUE = -1e30
Q_SCALE = math.log2(math.e) * HEAD_DIM ** -0.5

SUBLANES = 8
LANES = 128
VMEM_LIMIT = 56 * 1024 * 1024

SSM_GROUPS_PER_BLOCK = LANES // GROUP_CH
SSM_HS = SSM_GROUPS_PER_BLOCK * STATE_P


def _params(sem):
    return pltpu.CompilerParams(dimension_semantics=sem, vmem_limit_bytes=VMEM_LIMIT)


def _gelu(x):
    return 0.5 * x * (1.0 + lax.erf(x * math.sqrt(0.5)))


def _dot(a, b):
    return jnp.dot(a, b, preferred_element_type=F32)


def _dot_nt(a, b):
    return lax.dot_general(a, b, (((1,), (1,)), ((), ())), preferred_element_type=F32)


def _ada_kernel(c_ref, w_ref, b_ref, o_ref):
    c = c_ref[...]
    a = (c * jax.nn.sigmoid(c)).astype(BF16)
    o_ref[...] = _dot(a, w_ref[...].astype(BF16)) + b_ref[...]


def _ada(c_all, w_ada, b_ada, tn=1024):
    depth, d, n = w_ada.shape
    rows = c_all.shape[0]
    return pl.pallas_call(
        _ada_kernel,
        grid=(depth, n // tn),
        in_specs=[
            pl.BlockSpec((rows, d), lambda l, j: (0, 0)),
            pl.BlockSpec((None, d, tn), lambda l, j: (l, 0, j)),
            pl.BlockSpec((None, 1, tn), lambda l, j: (l, 0, j)),
        ],
        out_specs=pl.BlockSpec((None, rows, tn), lambda l, j: (l, 0, j)),
        out_shape=jax.ShapeDtypeStruct((depth, rows, n), F32),
        compiler_params=_params(("parallel", "parallel")),
        name="ada",
    )(c_all, w_ada, b_ada.reshape(depth, 1, n))


def _modulate(x_ref, sc_ref, sh_ref):
    return (x_ref[...] * (1.0 + sc_ref[...]) + sh_ref[...]).astype(BF16)


def _store_heads(o_ref, r, heads):
    nh, hd = heads
    rows = r.shape[0]
    for h in range(nh):
        part = r[:, h * hd:(h + 1) * hd]
        if hd == LANES:
            o_ref[pl.ds(h, rows, stride=nh), :] = part
        else:
            o_ref[:, h, :] = part


def _modmm_kernel(x_ref, sc_ref, sh_ref, w_ref, *rest, bf16_scale, f32_heads):
    *out_refs, h_scr = rest

    @pl.when(pl.program_id(2) == 0)
    def _():
        h_scr[...] = _modulate(x_ref, sc_ref, sh_ref)

    r = _dot(h_scr[...], w_ref[...])
    for o_ref in out_refs:
        if o_ref.dtype != F32:
            o_ref[...] = (r if bf16_scale == 1.0 else r * bf16_scale).astype(o_ref.dtype)
        elif f32_heads is None:
            o_ref[...] = r
        else:
            _store_heads(o_ref, r, f32_heads)


def _mod_spec(mod, tm):
    if mod.shape[1] == 1:
        return pl.BlockSpec((None, 1, mod.shape[2]), lambda b, i, n: (b, 0, 0))
    return pl.BlockSpec((None, tm, mod.shape[2]), lambda b, i, n: (b, i, 0))


def _mod_matmul(x, sc, sh, w, layer, col0, n, out_dtypes, tm, tn, name, bf16_scale=1.0,
                f32_heads=None):
    bsz, L, d = x.shape
    assert col0 % tn == 0 and n % tn == 0
    jb = col0 // tn
    nt = L // tm
    flat_spec = pl.BlockSpec((None, tm, tn), lambda b, i, j: (b, i, j))
    out_specs, out_shape = [], []
    for dt in out_dtypes:
        if dt != F32 or f32_heads is None:
            out_specs.append(flat_spec)
            out_shape.append(jax.ShapeDtypeStruct((bsz, L, n), dt))
            continue
        nh, hd = f32_heads
        assert tn == n == nh * hd
        if hd == LANES:
            out_specs.append(pl.BlockSpec((tm * nh, hd), lambda b, i, j: (b * nt + i, 0)))
            out_shape.append(jax.ShapeDtypeStruct((bsz * L * nh, hd), dt))
        else:
            out_specs.append(pl.BlockSpec((None, tm, nh, hd), lambda b, i, j: (b, i, 0, 0)))
            out_shape.append(jax.ShapeDtypeStruct((bsz, L, nh, hd), dt))
    return pl.pallas_call(
        functools.partial(_modmm_kernel, bf16_scale=bf16_scale, f32_heads=f32_heads),
        grid=(bsz, nt, n // tn),
        in_specs=[
            pl.BlockSpec((None, tm, d), lambda b, i, j: (b, i, 0)),
            _mod_spec(sc, tm),
            _mod_spec(sh, tm),
            pl.BlockSpec((None, d, tn), lambda b, i, j: (layer, 0, jb + j)),
        ],
        out_specs=out_specs,
        out_shape=out_shape,
        scratch_shapes=[pltpu.VMEM((tm, d), BF16)],
        compiler_params=_params(("parallel", "parallel", "arbitrary")),
        name=name,
    )(x, sc, sh, w)


def _ssm_scan_tile(u_ref, bdb_ref, are_ref, aim_ref, tb, x, h_scr, tt, cb):
    hs = SSM_HS
    for c in range(cb):
        lanes = slice(c * LANES, (c + 1) * LANES)
        for b in range(SUBLANES):
            tb[c, pl.ds(b, tt, stride=SUBLANES), :] = u_ref[b, :, lanes]
        x[c] = _dot(tb[c].astype(BF16), bdb_ref[c]).reshape(tt, SUBLANES, 2 * hs)
    for c in range(cb):
        ar = jnp.broadcast_to(are_ref[c], (SUBLANES, hs))
        ai = jnp.broadcast_to(aim_ref[c], (SUBLANES, hs))
        hr = h_scr[c, :, :hs]
        hi = h_scr[c, :, hs:]
        for t in range(tt):
            hr, hi = ((ar * hr - ai * hi) + x[c, t, :, :hs],
                      (ar * hi + ai * hr) + x[c, t, :, hs:])
            x[c, t, :, :hs] = hr
            x[c, t, :, hs:] = hi
        h_scr[c, :, :hs] = hr
        h_scr[c, :, hs:] = hi


def _ssm_out_tile(d_ref, bdc_ref, tb, x, z_ref, tt, cb):
    for c in range(cb):
        lanes = slice(c * LANES, (c + 1) * LANES)
        hb = x[c].reshape(tt * SUBLANES, 2 * SSM_HS).astype(BF16)
        y = _dot(hb, bdc_ref[c])
        tb[c] = _gelu(y + d_ref[:, lanes] * tb[c])
        for b in range(SUBLANES):
            z_ref[b, :, lanes] = tb[c, pl.ds(b, tt, stride=SUBLANES), :].astype(z_ref.dtype)


def _ssm_kernel(u_ref, d_ref, bdb_ref, are_ref, aim_ref, bdc_ref, sre_ref, sim_ref,
                z_ref, ore_ref, oim_ref, tb0, tb1, x0, x1, h_scr, *, tt, nt, cb):
    i = pl.program_id(2)
    hs = SSM_HS
    bufs = ((tb0, x0), (tb1, x1))
    scan = functools.partial(_ssm_scan_tile, u_ref, bdb_ref, are_ref, aim_ref)
    out = functools.partial(_ssm_out_tile, d_ref, bdc_ref)

    @pl.when(i == 0)
    def _():
        for c in range(cb):
            h_scr[c, :, :hs] = sre_ref[:, c * hs:(c + 1) * hs]
            h_scr[c, :, hs:] = sim_ref[:, c * hs:(c + 1) * hs]
        scan(*bufs[0], h_scr, tt, cb)

    for parity in range(2):
        @pl.when((i > 0) & (i < nt) & (i % 2 == parity))
        def _():
            scan(*bufs[parity], h_scr, tt, cb)
            out(*bufs[1 - parity], z_ref, tt, cb)

    @pl.when(i == nt)
    def _():
        out(*bufs[(nt - 1) % 2], z_ref, tt, cb)
        for c in range(cb):
            ore_ref[:, c * hs:(c + 1) * hs] = h_scr[c, :, :hs]
            oim_ref[:, c * hs:(c + 1) * hs] = h_scr[c, :, hs:]


def _ssm_scan(u, d_skip, bd_b, a_re, a_im, bd_c, s_re, s_im, tt, cb):
    bsz, L, d = u.shape
    nblk = d // LANES
    hs = SSM_HS
    nt = L // tt
    bw = cb * LANES
    st_spec = pl.BlockSpec((SUBLANES, cb * hs), lambda j, g, i: (g, j))
    in_spec = pl.BlockSpec((SUBLANES, tt, bw),
                           lambda j, g, i: (g, jnp.minimum(i, nt - 1), j))
    out_spec = pl.BlockSpec((SUBLANES, tt, bw),
                            lambda j, g, i: (g, jnp.maximum(i - 1, 0), j))
    return pl.pallas_call(
        functools.partial(_ssm_kernel, tt=tt, nt=nt, cb=cb),
        grid=(nblk // cb, bsz // SUBLANES, nt + 1),
        in_specs=[
            in_spec,
            pl.BlockSpec((1, bw), lambda j, g, i: (0, j)),
            pl.BlockSpec((cb, LANES, 2 * hs), lambda j, g, i: (j, 0, 0)),
            pl.BlockSpec((cb, 1, hs), lambda j, g, i: (j, 0, 0)),
            pl.BlockSpec((cb, 1, hs), lambda j, g, i: (j, 0, 0)),
            pl.BlockSpec((cb, 2 * hs, LANES), lambda j, g, i: (j, 0, 0)),
            st_spec,
            st_spec,
        ],
        out_specs=[out_spec, st_spec, st_spec],
        out_shape=[
            jax.ShapeDtypeStruct((bsz, L, d), BF16),
            jax.ShapeDtypeStruct(s_re.shape, F32),
            jax.ShapeDtypeStruct(s_im.shape, F32),
        ],
        scratch_shapes=[
            pltpu.VMEM((cb, tt * SUBLANES, LANES), F32),
            pltpu.VMEM((cb, tt * SUBLANES, LANES), F32),
            pltpu.VMEM((cb, tt, SUBLANES, 2 * hs), F32),
            pltpu.VMEM((cb, tt, SUBLANES, 2 * hs), F32),
            pltpu.VMEM((cb, SUBLANES, 2 * hs), F32),
        ],
        compiler_params=_params(("parallel", "parallel", "arbitrary")),
        name="ssm_scan",
    )(u, d_skip, bd_b, a_re, a_im, bd_c, s_re, s_im)


def _layer_norm_chunks(acc_scr, lng_ref, lnb_ref, o_ref, nchunk, tn):
    d = nchunk * tn
    parts = [acc_scr[c] for c in range(nchunk)]
    mu = sum(jnp.sum(p, axis=-1, keepdims=True) for p in parts) * (1.0 / d)
    cen = [p - mu for p in parts]
    var = sum(jnp.sum(q * q, axis=-1, keepdims=True) for q in cen) * (1.0 / d)
    inv = lax.rsqrt(var + LN_EPS)
    for c in range(nchunk):
        sl = slice(c * tn, (c + 1) * tn)
        o_ref[:, sl] = cen[c] * inv * lng_ref[:, sl] + lnb_ref[:, sl]


def _glu_ln_kernel(z_ref, wa_ref, wg_ref, x_ref, gt_ref, lng_ref, lnb_ref,
                   o_ref, acc_scr, *, nchunk, tn, alpha):
    c = pl.program_id(2)
    z = z_ref[...]
    a = _dot(z, wa_ref[...])
    g = _dot(z, wg_ref[...])
    m = a * jax.nn.sigmoid(g)
    acc_scr[c] = alpha * x_ref[...] + (1.0 + gt_ref[...]) * m

    @pl.when(c == nchunk - 1)
    def _():
        _layer_norm_chunks(acc_scr, lng_ref, lnb_ref, o_ref, nchunk, tn)


def _glu_ln(z, w_glu, layer, x, gate, ln_g, ln_b, tm, tn, alpha):
    bsz, L, d = x.shape
    nchunk = d // tn
    if gate.shape[1] == 1:
        gt_spec = pl.BlockSpec((None, 1, tn), lambda b, i, c: (b, 0, c))
    else:
        gt_spec = pl.BlockSpec((None, tm, tn), lambda b, i, c: (b, i, c))
    vec = pl.BlockSpec((1, d), lambda b, i, c: (0, 0))
    return pl.pallas_call(
        functools.partial(_glu_ln_kernel, nchunk=nchunk, tn=tn, alpha=alpha),
        grid=(bsz, L // tm, nchunk),
        in_specs=[
            pl.BlockSpec((None, tm, d), lambda b, i, c: (b, i, 0)),
            pl.BlockSpec((None, d, tn), lambda b, i, c: (layer, 0, c)),
            pl.BlockSpec((None, d, tn), lambda b, i, c: (layer, 0, nchunk + c)),
            pl.BlockSpec((None, tm, tn), lambda b, i, c: (b, i, c)),
            gt_spec,
            vec,
            vec,
        ],
        out_specs=pl.BlockSpec((None, tm, d), lambda b, i, c: (b, i, 0)),
        out_shape=jax.ShapeDtypeStruct((bsz, L, d), F32),
        scratch_shapes=[pltpu.VMEM((nchunk, tm, tn), F32)],
        compiler_params=_params(("parallel", "parallel", "arbitrary")),
        name="glu_ln",
    )(z, w_glu, w_glu, x, gate, ln_g, ln_b)


def _oproj_ln_kernel(o_in_ref, w_ref, x_ref, gt_ref, lng_ref, lnb_ref, o_ref, acc_scr,
                     *, nchunk, tn, alpha):
    c = pl.program_id(2)
    m = _dot(o_in_ref[...], w_ref[...])
    acc_scr[c] = alpha * x_ref[...] + (1.0 + gt_ref[...]) * m

    @pl.when(c == nchunk - 1)
    def _():
        _layer_norm_chunks(acc_scr, lng_ref, lnb_ref, o_ref, nchunk, tn)


def _oproj_ln(o_in, w_o, layer, x, gate, ln_g, ln_b, tm, tn, alpha):
    bsz, L, d = x.shape
    nchunk = d // tn
    if gate.shape[1] == 1:
        gt_spec = pl.BlockSpec((None, 1, tn), lambda b, i, c: (b, 0, c))
    else:
        gt_spec = pl.BlockSpec((None, tm, tn), lambda b, i, c: (b, i, c))
    vec = pl.BlockSpec((1, d), lambda b, i, c: (0, 0))
    return pl.pallas_call(
        functools.partial(_oproj_ln_kernel, nchunk=nchunk, tn=tn, alpha=alpha),
        grid=(bsz, L // tm, nchunk),
        in_specs=[
            pl.BlockSpec((None, tm, o_in.shape[2]), lambda b, i, c: (b, i, 0)),
            pl.BlockSpec((None, w_o.shape[1], tn), lambda b, i, c: (layer, 0, c)),
            pl.BlockSpec((None, tm, tn), lambda b, i, c: (b, i, c)),
            gt_spec,
            vec,
            vec,
        ],
        out_specs=pl.BlockSpec((None, tm, d), lambda b, i, c: (b, i, 0)),
        out_shape=jax.ShapeDtypeStruct((bsz, L, d), F32),
        scratch_shapes=[pltpu.VMEM((nchunk, tm, tn), F32)],
        compiler_params=_params(("parallel", "parallel", "arbitrary")),
        name="oproj_ln",
    )(o_in, w_o, x, gate, ln_g, ln_b)


def _ffn_kernel(x_ref, sc_ref, sh_ref, gt_ref, wg_ref, wv_ref, wc_ref, bc_ref, wd_ref,
                st_ref, lng_ref, lnb_ref, o_ref, cv_ref, h_scr, acc_scr, prev_scr,
                *, nb, T, nchunk, alpha):
    i = pl.program_id(1)
    c = pl.program_id(2)
    M = nb * T
    fc = wg_ref.shape[1]

    @pl.when(c == 0)
    def _():
        h_scr[...] = _modulate(x_ref, sc_ref, sh_ref)
        acc_scr[...] = jnp.zeros(acc_scr.shape, F32)

    @pl.when(i == 0)
    def _():
        prev_scr[...] = st_ref[c]

    @pl.when(i > 0)
    def _():
        prev_scr[...] = cv_ref[c]

    h = h_scr[...]
    g = _dot(h, wg_ref[...])
    v = _dot(h, wv_ref[...])

    row = lax.broadcasted_iota(jnp.int32, (M, 1), 0)
    if nb == 1:
        tpos = row
        p0 = prev_scr[0, 0:1, :]
        p1 = prev_scr[0, 1:2, :]
    else:
        tpos = lax.rem(row, T)
        p0 = jnp.broadcast_to(prev_scr[:, 0:1, :], (nb, T, fc)).reshape(M, fc)
        p1 = jnp.broadcast_to(prev_scr[:, 1:2, :], (nb, T, fc)).reshape(M, fc)
    s1 = jnp.where(tpos == 0, p1, pltpu.roll(g, 1, 0))
    s2 = jnp.where(tpos == 0, p0, jnp.where(tpos == 1, p1, pltpu.roll(g, 2, 0)))
    conv = bc_ref[...] + s2 * wc_ref[0:1, :] + s1 * wc_ref[1:2, :] + g * wc_ref[2:3, :]
    act = (_gelu(conv) * v).astype(BF16)
    acc_scr[...] += _dot(act, wd_ref[...])
    cv_ref[c] = g.reshape(nb, T, fc)[:, T - 2:, :]

    @pl.when(c == nchunk - 1)
    def _():
        r = alpha * x_ref[...] + (1.0 + gt_ref[...]) * acc_scr[...]
        mu = jnp.mean(r, axis=-1, keepdims=True)
        cen = r - mu
        var = jnp.mean(cen * cen, axis=-1, keepdims=True)
        o_ref[...] = cen * lax.rsqrt(var + LN_EPS) * lng_ref[...] + lnb_ref[...]


def _ffn(x, sc, sh, gate, w_up, w_conv, b_conv, w_down, layer, conv_state, ln_g, ln_b,
         nb, T, fc, alpha):
    S, R, d = x.shape
    dff = w_down.shape[1]
    nchunk = dff // fc
    tm = nb * T
    ntile = R // tm
    assert nb == 1 or ntile == 1

    def mod_spec(mod):
        if mod.shape[1] == 1:
            return pl.BlockSpec((None, 1, d), lambda s, i, c: (s, 0, 0))
        return pl.BlockSpec((None, tm, d), lambda s, i, c: (s, i, 0))

    vec = pl.BlockSpec((1, d), lambda s, i, c: (0, 0))
    st_spec = pl.BlockSpec((nchunk, nb, 2, fc), lambda s, i, c: (0, s, 0, 0))
    return pl.pallas_call(
        functools.partial(_ffn_kernel, nb=nb, T=T, nchunk=nchunk, alpha=alpha),
        grid=(S, ntile, nchunk),
        in_specs=[
            pl.BlockSpec((None, tm, d), lambda s, i, c: (s, i, 0)),
            mod_spec(sc),
            mod_spec(sh),
            mod_spec(gate),
            pl.BlockSpec((None, d, fc), lambda s, i, c: (layer, 0, c)),
            pl.BlockSpec((None, d, fc), lambda s, i, c: (layer, 0, nchunk + c)),
            pl.BlockSpec((w_conv.shape[0], fc), lambda s, i, c: (0, c)),
            pl.BlockSpec((1, fc), lambda s, i, c: (0, c)),
            pl.BlockSpec((None, fc, d), lambda s, i, c: (layer, c, 0)),
            st_spec,
            vec,
            vec,
        ],
        out_specs=[
            pl.BlockSpec((None, tm, d), lambda s, i, c: (s, i, 0)),
            st_spec,
        ],
        out_shape=[
            jax.ShapeDtypeStruct((S, R, d), F32),
            jax.ShapeDtypeStruct(conv_state.shape, F32),
        ],
        scratch_shapes=[
            pltpu.VMEM((tm, d), BF16),
            pltpu.VMEM((tm, d), F32),
            pltpu.VMEM((nb, 2, fc), F32),
        ],
        compiler_params=_params(("parallel", "arbitrary", "arbitrary")),
        name="conv_ffn",
    )(x, sc, sh, gate, w_up, w_up, w_conv, b_conv, w_down, conv_state, ln_g, ln_b)


def _qkv(x, sc, sh, w_qkv, layer, tm):
    da = w_qkv.shape[2] // 3
    (q,) = _mod_matmul(x, sc, sh, w_qkv, layer, 0, da, (BF16,), tm, da, "q_proj", Q_SCALE)
    k32, k16 = _mod_matmul(x, sc, sh, w_qkv, layer, da, da, (F32, BF16), tm, da, "k_proj",
                           f32_heads=(da // HEAD_DIM, HEAD_DIM))
    v32, v16 = _mod_matmul(x, sc, sh, w_qkv, layer, 2 * da, da, (F32, BF16), tm, da, "v_proj",
                           f32_heads=(da // (2 * HEAD_DIM), 2 * HEAD_DIM))
    return q, k32, v32, k16, v16


def _chunk_id(pos):
    assert CHUNK & (CHUNK - 1) == 0
    return lax.shift_right_logical(pos, CHUNK.bit_length() - 1)


def _sub_ln(o, g_ref, lam_init):
    o = o * lax.rsqrt(jnp.mean(o * o, axis=-1, keepdims=True) + LN_EPS)
    return o * g_ref[...] * (1.0 - lam_init)


def _lane_tile(x, width):
    if width < LANES:
        return x[:, :width]
    return jnp.tile(x, (1, width // LANES))


def _online_update(s, v, m_scr, l_scr, acc_scr, idx):
    m_old = m_scr[idx]
    m_new = jnp.maximum(m_old, jnp.max(s, axis=-1, keepdims=True))
    corr = jnp.exp2(m_old - m_new)
    p = jnp.exp2(s - _lane_tile(m_new, s.shape[1]))
    l_scr[idx] = corr * l_scr[idx] + jnp.sum(p, axis=-1, keepdims=True)
    acc_scr[idx] = _lane_tile(corr, v.shape[1]) * acc_scr[idx] + _dot(p.astype(BF16), v)
    m_scr[idx] = m_new


def _attn_prompt_kernel(lam_ref, q_ref, k_ref, v_ref, g_ref, o_ref, m_scr, l_scr, acc_scr,
                        *, tq, sb, lam_init):
    i = pl.program_id(2)
    maps = (slice(0, HEAD_DIM), slice(HEAD_DIM, 2 * HEAD_DIM))
    base = pl.multiple_of(i * tq, tq)

    row_chunk = _chunk_id(lax.broadcasted_iota(jnp.int32, (sb, sb), 0))
    col_chunk = _chunk_id(lax.broadcasted_iota(jnp.int32, (sb, sb), 1))
    diag_keep = col_chunk <= row_chunk
    for r in range(tq // sb):
        rows = slice(r * sb, (r + 1) * sb)
        nk = (r + 1) * sb
        v = v_ref[pl.ds(base, nk), :]
        for idx, cols in enumerate(maps):
            s = _dot_nt(q_ref[rows, cols], k_ref[pl.ds(base, nk), cols])
            s_diag = jnp.where(diag_keep, s[:, r * sb:], MASK_VALUE)
            s = s_diag if r == 0 else jnp.concatenate([s[:, :r * sb], s_diag], axis=1)
            m = jnp.max(s, axis=-1, keepdims=True)
            p = jnp.exp2(s - m)
            m_scr[idx, rows] = jnp.broadcast_to(m, (sb, LANES))
            l_scr[idx, rows] = jnp.broadcast_to(jnp.sum(p, axis=-1, keepdims=True), (sb, LANES))
            acc_scr[idx, rows] = _dot(p.astype(BF16), v)

    def full_tiles(j0, count):
        starts = [pl.multiple_of((j0 + t) * tq, tq) for t in range(count)]
        s = [[_dot_nt(q_ref[:, cols], k_ref[pl.ds(st, tq), cols]) for cols in maps]
             for st in starts]
        for t, st in enumerate(starts):
            v = v_ref[pl.ds(st, tq), :]
            for idx in range(len(maps)):
                _online_update(s[t][idx], v, m_scr, l_scr, acc_scr, idx)

    def pair(jj, carry):
        full_tiles(2 * jj, 2)
        return carry

    lax.fori_loop(0, i // 2, pair, 0)

    @pl.when(i % 2 == 1)
    def _():
        full_tiles(i - 1, 1)

    lam = lam_ref[0, 0]
    hw = acc_scr.shape[2]
    o = (acc_scr[0] * _lane_tile(1.0 / l_scr[0], hw)
         - acc_scr[1] * _lane_tile(lam / l_scr[1], hw))
    o_ref[...] = _sub_ln(o, g_ref, lam_init).astype(o_ref.dtype)


def _attn_prompt(lam, q, k, v, subln_g, tq, sb, lam_init):
    bsz, L, da = q.shape
    hw = 2 * HEAD_DIM
    nh = da // hw
    assert tq % sb == 0 and sb % CHUNK == 0
    return pl.pallas_call(
        functools.partial(_attn_prompt_kernel, tq=tq, sb=sb, lam_init=lam_init),
        grid=(bsz, nh, L // tq),
        in_specs=[
            pl.BlockSpec(memory_space=pltpu.SMEM),
            pl.BlockSpec((None, tq, hw), lambda b, h, i: (b, i, h)),
            pl.BlockSpec((None, L, hw), lambda b, h, i: (b, 0, h)),
            pl.BlockSpec((None, L, hw), lambda b, h, i: (b, 0, h)),
            pl.BlockSpec((1, hw), lambda b, h, i: (0, 0)),
        ],
        out_specs=pl.BlockSpec((None, tq, hw), lambda b, h, i: (b, i, h)),
        out_shape=jax.ShapeDtypeStruct((bsz, L, da), BF16),
        scratch_shapes=[
            pltpu.VMEM((2, tq, LANES), F32),
            pltpu.VMEM((2, tq, LANES), F32),
            pltpu.VMEM((2, tq, hw), F32),
        ],
        compiler_params=_params(("parallel", "parallel", "arbitrary")),
        name="attn_prompt",
    )(lam, q, k, v, subln_g)


def _attn_sample_kernel(lam_ref, q_ref, ck_ref, cv_ref, kn_ref, vn_ref, g_ref, o_ref,
                        m_scr, l_scr, acc_scr, vh_scr, *, P, T, pc, nsub, lam_init):
    j = pl.program_id(1)
    hw = 2 * HEAD_DIM
    qc = _chunk_id(P + lax.broadcasted_iota(jnp.int32, (T, 1), 0))

    @pl.when(j == 0)
    def _():
        m_scr[...] = jnp.full(m_scr.shape, MASK_VALUE, F32)
        l_scr[...] = jnp.zeros(l_scr.shape, F32)
        acc_scr[...] = jnp.zeros(acc_scr.shape, F32)

    keep_c = _chunk_id(j * pc + lax.broadcasted_iota(jnp.int32, (1, pc), 1)) <= qc
    nh = nsub // 2
    for half in range(2):
        vh_scr[half] = cv_ref[:, half * LANES:(half + 1) * LANES]
    scores = []
    for s in range(nsub):
        k = ck_ref[pl.ds(s, pc, stride=nsub), :].astype(BF16)
        q = q_ref[:, s * HEAD_DIM:(s + 1) * HEAD_DIM]
        scores.append(jnp.where(keep_c, _dot_nt(q, k), MASK_VALUE))
    for h in range(nh):
        v = jnp.concatenate([vh_scr[half, pl.ds(h, pc, stride=nh), :] for half in range(2)],
                            axis=1).astype(BF16)
        for s in (2 * h, 2 * h + 1):
            _online_update(scores[s], v, m_scr, l_scr, acc_scr, s)

    @pl.when(j == pl.num_programs(1) - 1)
    def _():
        keep_n = _chunk_id(P + lax.broadcasted_iota(jnp.int32, (1, T), 1)) <= qc
        lam = lam_ref[0, 0]
        new_scores = []
        for s in range(nsub):
            cols = slice(s * HEAD_DIM, (s + 1) * HEAD_DIM)
            new_scores.append(
                jnp.where(keep_n, _dot_nt(q_ref[:, cols], kn_ref[:, cols]), MASK_VALUE))
        for h in range(nh):
            vcols = slice(h * hw, (h + 1) * hw)
            vn = vn_ref[:, vcols]
            for s in (2 * h, 2 * h + 1):
                _online_update(new_scores[s], vn, m_scr, l_scr, acc_scr, s)
        for h in range(nh):
            vcols = slice(h * hw, (h + 1) * hw)
            o = (acc_scr[2 * h] * _lane_tile(1.0 / l_scr[2 * h], hw)
                 - acc_scr[2 * h + 1] * _lane_tile(lam / l_scr[2 * h + 1], hw))
            o_ref[:, vcols] = _sub_ln(o, g_ref, lam_init).astype(o_ref.dtype)


def _attn_sample(lam, q, cache_k, cache_v, layer, k_new, v_new, subln_g, pc, lam_init):
    bsz, T, da = q.shape
    na, _, P, nsub, _ = cache_k.shape
    hw = 2 * HEAD_DIM
    new_spec = pl.BlockSpec((None, T, da), lambda b, j: (b, 0, 0))
    return pl.pallas_call(
        functools.partial(_attn_sample_kernel, P=P, T=T, pc=pc, nsub=nsub, lam_init=lam_init),
        grid=(bsz, P // pc),
        in_specs=[
            pl.BlockSpec(memory_space=pltpu.SMEM),
            new_spec,
            pl.BlockSpec((None, None, pc * nsub, HEAD_DIM), lambda b, j: (layer, b, j, 0)),
            pl.BlockSpec((None, None, pc * (nsub // 2), hw), lambda b, j: (layer, b, j, 0)),
            new_spec,
            new_spec,
            pl.BlockSpec((1, hw), lambda b, j: (0, 0)),
        ],
        out_specs=new_spec,
        out_shape=jax.ShapeDtypeStruct((bsz, T, da), BF16),
        scratch_shapes=[
            pltpu.VMEM((nsub, T, LANES), F32),
            pltpu.VMEM((nsub, T, LANES), F32),
            pltpu.VMEM((nsub, T, hw), F32),
            pltpu.VMEM((2, pc * (nsub // 2), LANES), F32),
        ],
        compiler_params=_params(("parallel", "arbitrary")),
        name="attn_sample",
    )(lam, q, cache_k.reshape(na, bsz, P * nsub, HEAD_DIM),
      cache_v.reshape(na, bsz, P * (nsub // 2), hw), k_new, v_new, subln_g)


def _ssm_discretise(lam_re, lam_im, log_step, b_re, b_im):
    lr = jnp.minimum(lam_re, -1e-4)
    li = lam_im
    dt = jnp.exp(log_step)[:, None]
    mag = jnp.exp(lr * dt)
    abar_re = mag * jnp.cos(li * dt)
    abar_im = mag * jnp.sin(li * dt)
    nr = abar_re - 1.0
    ni = abar_im
    den = lr * lr + li * li
    kr = (nr * lr + ni * li) / den
    ki = (ni * lr - nr * li) / den
    bbar_re = kr[..., None] * b_re - ki[..., None] * b_im
    bbar_im = kr[..., None] * b_im + ki[..., None] * b_re
    return abar_re, abar_im, bbar_re, bbar_im


def _block_diag(m):
    nblk, gb, r, c = m.shape
    eye = jnp.eye(gb, dtype=m.dtype)
    return jnp.einsum("jgrc,gh->jgrhc", m, eye).reshape(nblk, gb * r, gb * c)


def _ssm_matrices(lam_re, lam_im, log_step, b_re, b_im, c_re, c_im):
    G = lam_re.shape[0]
    gb = SSM_GROUPS_PER_BLOCK
    nblk = G // gb
    abar_re, abar_im, bbar_re, bbar_im = _ssm_discretise(lam_re, lam_im, log_step, b_re, b_im)
    bt_re = bbar_re.reshape(nblk, gb, STATE_P, GROUP_CH).swapaxes(2, 3)
    bt_im = bbar_im.reshape(nblk, gb, STATE_P, GROUP_CH).swapaxes(2, 3)
    bd_b = jnp.concatenate([_block_diag(bt_re), _block_diag(bt_im)], axis=2).astype(BF16)
    ct_re = c_re.reshape(nblk, gb, GROUP_CH, STATE_P).swapaxes(2, 3)
    ct_im = c_im.reshape(nblk, gb, GROUP_CH, STATE_P).swapaxes(2, 3)
    bd_c = jnp.concatenate([_block_diag(ct_re), _block_diag(-ct_im)], axis=1).astype(BF16)
    a_re = abar_re.reshape(nblk, 1, gb * STATE_P)
    a_im = abar_im.reshape(nblk, 1, gb * STATE_P)
    return bd_b, a_re, a_im, bd_c


def _chunk_state(state, fc):
    nbt, w, dff = state.shape
    return state.reshape(nbt, w, dff // fc, fc).transpose(2, 0, 1, 3)


def _unchunk_state(state):
    nchunk, nbt, w, fc = state.shape
    return state.transpose(1, 2, 0, 3).reshape(nbt, w, nchunk * fc)


TM_PROMPT = 512
TN_GLU = 1024
FFN_CHUNK = 512
SSM_TT = 128
SSM_CB = 4
SAMPLE_PC = 512
TQ = 512
TQ_DIAG = 256


def kernel(x_prompt, x_sample, c_prompt, c_sample, cache_k, cache_v, state_ssm_re, state_ssm_im, state_conv, w_ada, b_ada, ln_g, ln_b, w_up, w_dconv, b_dconv, w_down, w_ssm_in, ssm_lam_re, ssm_lam_im, ssm_log_step, ssm_b_re, ssm_b_im, ssm_c_re, ssm_c_im, ssm_d, w_glu, w_qkv, lam_q1, lam_k1, lam_q2, lam_k2, subln_g, w_o):
    depth = w_ada.shape[0]
    bp, L, d = x_prompt.shape
    bs, T, _ = x_sample.shape
    dff = w_down.shape[1]
    alpha = (2 * depth) ** 0.25
    rows_s = bs * T

    mods = _ada(jnp.concatenate([c_prompt, c_sample], axis=0), w_ada, b_ada)
    w_up16, w_down16 = w_up.astype(BF16), w_down.astype(BF16)
    w_in16, w_glu16 = w_ssm_in.astype(BF16), w_glu.astype(BF16)
    w_qkv16, w_o16 = w_qkv.astype(BF16), w_o.astype(BF16)

    xp = x_prompt
    xs = x_sample.reshape(1, rows_s, d)
    zeros_conv = jnp.zeros((bp, state_conv.shape[2], dff), F32)
    outs = {name: [] for name in ("kp", "vp", "srp", "sip", "cvp", "ks", "vs", "srs", "sis", "cvs")}

    for i in range(depth):
        mp = [m.reshape(bp, 1, d) for m in jnp.split(mods[i, :bp], 6, axis=-1)]
        ms = [jnp.repeat(m, T, axis=0).reshape(1, rows_s, d)
              for m in jnp.split(mods[i, bp:], 6, axis=-1)]
        shp1, scp1, gtp1, shp2, scp2, gtp2 = mp
        shs1, scs1, gts1, shs2, scs2, gts2 = ms
        lng = ln_g[i].reshape(2, 1, d)
        lnb = ln_b[i].reshape(2, 1, d)
        j = i // 2
        if i % 2 == 0:
            bd_b, a_re, a_im, bd_c = _ssm_matrices(
                ssm_lam_re[j], ssm_lam_im[j], ssm_log_step[j], ssm_b_re[j], ssm_b_im[j],
                ssm_c_re[j], ssm_c_im[j])
            d_skip = ssm_d[j].reshape(1, d)
            gp = ssm_lam_re.shape[1] * ssm_lam_re.shape[2]
            (up,) = _mod_matmul(xp, scp1, shp1, w_in16, j, 0, d, (F32,), TM_PROMPT, d, "ssm_in")
            zero_state = jnp.zeros((bp, gp), F32)
            zp, srp, sip = _ssm_scan(up, d_skip, bd_b, a_re, a_im, bd_c, zero_state, zero_state,
                                     SSM_TT, SSM_CB)
            xp = _glu_ln(zp, w_glu16, j, xp, gtp1, lng[0], lnb[0], TM_PROMPT, TN_GLU, alpha)
            (us,) = _mod_matmul(xs, scs1, shs1, w_in16, j, 0, d, (F32,), rows_s, d, "ssm_in")
            zs, srs, sis = _ssm_scan(us.reshape(bs, T, d), d_skip, bd_b, a_re, a_im, bd_c,
                                     state_ssm_re[j].reshape(bs, gp),
                                     state_ssm_im[j].reshape(bs, gp), T, SSM_CB)
            xs = _glu_ln(zs.reshape(1, rows_s, d), w_glu16, j, xs, gts1, lng[0], lnb[0],
                         rows_s, TN_GLU, alpha)
            st_shape = ssm_lam_re.shape[1:]
            outs["srp"].append(srp.reshape((bp,) + st_shape))
            outs["sip"].append(sip.reshape((bp,) + st_shape))
            outs["srs"].append(srs.reshape((bs,) + st_shape))
            outs["sis"].append(sis.reshape((bs,) + st_shape))
        else:
            lam_init = 0.8 - 0.6 * math.exp(-0.3 * i)
            lam = (jnp.exp(jnp.sum(lam_q1[j] * lam_k1[j])) - jnp.exp(jnp.sum(lam_q2[j] * lam_k2[j]))
                   + lam_init).reshape(1, 1)
            sg = subln_g[j].reshape(1, 2 * HEAD_DIM)
            nsub = cache_k.shape[3]
            nhead = cache_v.shape[3]
            qp, kp, vp, kp16, vp16 = _qkv(xp, scp1, shp1, w_qkv16, j, TM_PROMPT)
            op = _attn_prompt(lam, qp, kp16, vp16, sg, TQ, TQ_DIAG, lam_init)
            xp = _oproj_ln(op, w_o16, j, xp, gtp1, lng[0], lnb[0], TM_PROMPT, d, alpha)
            qs, ksn, vsn, ks16, vs16 = _qkv(xs, scs1, shs1, w_qkv16, j, rows_s)
            da = w_o.shape[1]
            osm = _attn_sample(lam, qs.reshape(bs, T, da), cache_k, cache_v, j,
                               ks16.reshape(bs, T, da), vs16.reshape(bs, T, da), sg,
                               SAMPLE_PC, lam_init)
            xs = _oproj_ln(osm.reshape(1, rows_s, da), w_o16, j, xs, gts1, lng[0], lnb[0],
                           rows_s, d, alpha)
            outs["kp"].append(kp.reshape(bp, L, nsub, HEAD_DIM))
            outs["vp"].append(vp.reshape(bp, L, nhead, 2 * HEAD_DIM))
            outs["ks"].append(ksn.reshape(bs, T, nsub, HEAD_DIM))
            outs["vs"].append(vsn.reshape(bs, T, nhead, 2 * HEAD_DIM))

        bconv = b_dconv[i].reshape(1, dff)
        xp, cvp = _ffn(xp, scp2, shp2, gtp2, w_up16, w_dconv[i], bconv, w_down16, i,
                       _chunk_state(zeros_conv, FFN_CHUNK), lng[1], lnb[1],
                       1, TM_PROMPT, FFN_CHUNK, alpha)
        xs, cvs = _ffn(xs, scs2, shs2, gts2, w_up16, w_dconv[i], bconv, w_down16, i,
                       _chunk_state(state_conv[i], FFN_CHUNK), lng[1], lnb[1],
                       bs, T, FFN_CHUNK, alpha)
        outs["cvp"].append(_unchunk_state(cvp))
        outs["cvs"].append(_unchunk_state(cvs))

    return (xp, xs.reshape(bs, T, d),
            jnp.stack(outs["kp"]), jnp.stack(outs["vp"]),
            jnp.stack(outs["srp"]), jnp.stack(outs["sip"]), jnp.stack(outs["cvp"]),
            jnp.stack(outs["ks"]), jnp.stack(outs["vs"]),
            jnp.stack(outs["srs"]), jnp.stack(outs["sis"]), jnp.stack(outs["cvs"]))
```

```python
import functools
import math

import jax
import jax.numpy as jnp
from jax import lax
from jax.experimental import pallas as pl
from jax.experimental.pallas import tpu as pltpu

F32 = jnp.float32
BF16 = jnp.bfloat16

CHUNK = 64
HEAD_DIM = 128
GROUP_CH = 16
STATE_P = 64
LN_EPS = 1e-5
MASK_VALUE = -1e30
Q_SCALE = math.log2(math.e) * HEAD_DIM ** -0.5

SUBLANES = 8
LANES = 128
VMEM_LIMIT = 56 * 1024 * 1024

SSM_GROUPS_PER_BLOCK = LANES // GROUP_CH
SSM_HS = SSM_GROUPS_PER_BLOCK * STATE_P


def _params(sem):
    return pltpu.CompilerParams(dimension_semantics=sem, vmem_limit_bytes=VMEM_LIMIT)


def _gelu(x):
    return 0.5 * x * (1.0 + lax.erf(x * math.sqrt(0.5)))


def _dot(a, b):
    return jnp.dot(a, b, preferred_element_type=F32)


def _dot_nt(a, b):
    return lax.dot_general(a, b, (((1,), (1,)), ((), ())), preferred_element_type=F32)


def _ada_kernel(c_ref, w_ref, b_ref, o_ref):
    c = c_ref[...]
    a = (c * jax.nn.sigmoid(c)).astype(BF16)
    o_ref[...] = _dot(a, w_ref[...].astype(BF16)) + b_ref[...]


def _ada(c_all, w_ada, b_ada, tn=1024):
    depth, d, n = w_ada.shape
    rows = c_all.shape[0]
    return pl.pallas_call(
        _ada_kernel,
        grid=(depth, n // tn),
        in_specs=[
            pl.BlockSpec((rows, d), lambda l, j: (0, 0)),
            pl.BlockSpec((None, d, tn), lambda l, j: (l, 0, j)),
            pl.BlockSpec((None, 1, tn), lambda l, j: (l, 0, j)),
        ],
        out_specs=pl.BlockSpec((None, rows, tn), lambda l, j: (l, 0, j)),
        out_shape=jax.ShapeDtypeStruct((depth, rows, n), F32),
        compiler_params=_params(("parallel", "parallel")),
        name="ada",
    )(c_all, w_ada, b_ada.reshape(depth, 1, n))


def _modulate(x_ref, sc_ref, sh_ref):
    return (x_ref[...] * (1.0 + sc_ref[...]) + sh_ref[...]).astype(BF16)


def _store_heads(o_ref, r, heads):
    nh, hd = heads
    rows = r.shape[0]
    for h in range(nh):
        part = r[:, h * hd:(h + 1) * hd]
        if hd == LANES:
            o_ref[pl.ds(h, rows, stride=nh), :] = part
        else:
            o_ref[:, h, :] = part


def _modmm_kernel(x_ref, sc_ref, sh_ref, w_ref, *rest, bf16_scale, f32_heads):
    *out_refs, h_scr = rest

    @pl.when(pl.program_id(2) == 0)
    def _():
        h_scr[...] = _modulate(x_ref, sc_ref, sh_ref)

    r = _dot(h_scr[...], w_ref[...])
    for o_ref in out_refs:
        if o_ref.dtype != F32:
            o_ref[...] = (r if bf16_scale == 1.0 else r * bf16_scale).astype(o_ref.dtype)
        elif f32_heads is None:
            o_ref[...] = r
        else:
            _store_heads(o_ref, r, f32_heads)


def _mod_spec(mod, tm):
    if mod.shape[1] == 1:
        return pl.BlockSpec((None, 1, mod.shape[2]), lambda b, i, n: (b, 0, 0))
    return pl.BlockSpec((None, tm, mod.shape[2]), lambda b, i, n: (b, i, 0))


def _mod_matmul(x, sc, sh, w, layer, col0, n, out_dtypes, tm, tn, name, bf16_scale=1.0,
                f32_heads=None):
    bsz, L, d = x.shape
    assert col0 % tn == 0 and n % tn == 0
    jb = col0 // tn
    nt = L // tm
    flat_spec = pl.BlockSpec((None, tm, tn), lambda b, i, j: (b, i, j))
    out_specs, out_shape = [], []
    for dt in out_dtypes:
        if dt != F32 or f32_heads is None:
            out_specs.append(flat_spec)
            out_shape.append(jax.ShapeDtypeStruct((bsz, L, n), dt))
            continue
        nh, hd = f32_heads
        assert tn == n == nh * hd
        if hd == LANES:
            out_specs.append(pl.BlockSpec((tm * nh, hd), lambda b, i, j: (b * nt + i, 0)))
            out_shape.append(jax.ShapeDtypeStruct((bsz * L * nh, hd), dt))
        else:
            out_specs.append(pl.BlockSpec((None, tm, nh, hd), lambda b, i, j: (b, i, 0, 0)))
            out_shape.append(jax.ShapeDtypeStruct((bsz, L, nh, hd), dt))
    return pl.pallas_call(
        functools.partial(_modmm_kernel, bf16_scale=bf16_scale, f32_heads=f32_heads),
        grid=(bsz, nt, n // tn),
        in_specs=[
            pl.BlockSpec((None, tm, d), lambda b, i, j: (b, i, 0)),
            _mod_spec(sc, tm),
            _mod_spec(sh, tm),
            pl.BlockSpec((None, d, tn), lambda b, i, j: (layer, 0, jb + j)),
        ],
        out_specs=out_specs,
        out_shape=out_shape,
        scratch_shapes=[pltpu.VMEM((tm, d), BF16)],
        compiler_params=_params(("parallel", "parallel", "arbitrary")),
        name=name,
    )(x, sc, sh, w)


def _ssm_scan_tile(u_ref, bdb_ref, are_ref, aim_ref, tb, x, h_scr, tt, cb):
    hs = SSM_HS
    for c in range(cb):
        lanes = slice(c * LANES, (c + 1) * LANES)
        for b in range(SUBLANES):
            tb[c, pl.ds(b, tt, stride=SUBLANES), :] = u_ref[b, :, lanes]
        x[c] = _dot(tb[c].astype(BF16), bdb_ref[c]).reshape(tt, SUBLANES, 2 * hs)
    for c in range(cb):
        ar = jnp.broadcast_to(are_ref[c], (SUBLANES, hs))
        ai = jnp.broadcast_to(aim_ref[c], (SUBLANES, hs))
        hr = h_scr[c, :, :hs]
        hi = h_scr[c, :, hs:]
        for t in range(tt):
            hr, hi = ((ar * hr - ai * hi) + x[c, t, :, :hs],
                      (ar * hi + ai * hr) + x[c, t, :, hs:])
            x[c, t, :, :hs] = hr
            x[c, t, :, hs:] = hi
        h_scr[c, :, :hs] = hr
        h_scr[c, :, hs:] = hi


def _ssm_out_tile(d_ref, bdc_ref, tb, x, z_ref, tt, cb):
    for c in range(cb):
        lanes = slice(c * LANES, (c + 1) * LANES)
        hb = x[c].reshape(tt * SUBLANES, 2 * SSM_HS).astype(BF16)
        y = _dot(hb, bdc_ref[c])
        tb[c] = _gelu(y + d_ref[:, lanes] * tb[c])
        for b in range(SUBLANES):
            z_ref[b, :, lanes] = tb[c, pl.ds(b, tt, stride=SUBLANES), :].astype(z_ref.dtype)


def _ssm_kernel(u_ref, d_ref, bdb_ref, are_ref, aim_ref, bdc_ref, sre_ref, sim_ref,
                z_ref, ore_ref, oim_ref, tb0, tb1, x0, x1, h_scr, *, tt, nt, cb):
    i = pl.program_id(2)
    hs = SSM_HS
    bufs = ((tb0, x0), (tb1, x1))
    scan = functools.partial(_ssm_scan_tile, u_ref, bdb_ref, are_ref, aim_ref)
    out = functools.partial(_ssm_out_tile, d_ref, bdc_ref)

    @pl.when(i == 0)
    def _():
        for c in range(cb):
            h_scr[c, :, :hs] = sre_ref[:, c * hs:(c + 1) * hs]
            h_scr[c, :, hs:] = sim_ref[:, c * hs:(c + 1) * hs]
        scan(*bufs[0], h_scr, tt, cb)

    for parity in range(2):
        @pl.when((i > 0) & (i < nt) & (i % 2 == parity))
        def _():
            scan(*bufs[parity], h_scr, tt, cb)
            out(*bufs[1 - parity], z_ref, tt, cb)

    @pl.when(i == nt)
    def _():
        out(*bufs[(nt - 1) % 2], z_ref, tt, cb)
        for c in range(cb):
            ore_ref[:, c * hs:(c + 1) * hs] = h_scr[c, :, :hs]
            oim_ref[:, c * hs:(c + 1) * hs] = h_scr[c, :, hs:]


def _ssm_scan(u, d_skip, bd_b, a_re, a_im, bd_c, s_re, s_im, tt, cb):
    bsz, L, d = u.shape
    nblk = d // LANES
    hs = SSM_HS
    nt = L // tt
    bw = cb * LANES
    st_spec = pl.BlockSpec((SUBLANES, cb * hs), lambda j, g, i: (g, j))
    in_spec = pl.BlockSpec((SUBLANES, tt, bw),
                           lambda j, g, i: (g, jnp.minimum(i, nt - 1), j))
    out_spec = pl.BlockSpec((SUBLANES, tt, bw),
                            lambda j, g, i: (g, jnp.maximum(i - 1, 0), j))
    return pl.pallas_call(
        functools.partial(_ssm_kernel, tt=tt, nt=nt, cb=cb),
        grid=(nblk // cb, bsz // SUBLANES, nt + 1),
        in_specs=[
            in_spec,
            pl.BlockSpec((1, bw), lambda j, g, i: (0, j)),
            pl.BlockSpec((cb, LANES, 2 * hs), lambda j, g, i: (j, 0, 0)),
            pl.BlockSpec((cb, 1, hs), lambda j, g, i: (j, 0, 0)),
            pl.BlockSpec((cb, 1, hs), lambda j, g, i: (j, 0, 0)),
            pl.BlockSpec((cb, 2 * hs, LANES), lambda j, g, i: (j, 0, 0)),
            st_spec,
            st_spec,
        ],
        out_specs=[out_spec, st_spec, st_spec],
        out_shape=[
            jax.ShapeDtypeStruct((bsz, L, d), BF16),
            jax.ShapeDtypeStruct(s_re.shape, F32),
            jax.ShapeDtypeStruct(s_im.shape, F32),
        ],
        scratch_shapes=[
            pltpu.VMEM((cb, tt * SUBLANES, LANES), F32),
            pltpu.VMEM((cb, tt * SUBLANES, LANES), F32),
            pltpu.VMEM((cb, tt, SUBLANES, 2 * hs), F32),
            pltpu.VMEM((cb, tt, SUBLANES, 2 * hs), F32),
            pltpu.VMEM((cb, SUBLANES, 2 * hs), F32),
        ],
        compiler_params=_params(("parallel", "parallel", "arbitrary")),
        name="ssm_scan",
    )(u, d_skip, bd_b, a_re, a_im, bd_c, s_re, s_im)


def _layer_norm_chunks(acc_scr, lng_ref, lnb_ref, o_ref, nchunk, tn):
    d = nchunk * tn
    parts = [acc_scr[c] for c in range(nchunk)]
    mu = sum(jnp.sum(p, axis=-1, keepdims=True) for p in parts) * (1.0 / d)
    cen = [p - mu for p in parts]
    var = sum(jnp.sum(q * q, axis=-1, keepdims=True) for q in cen) * (1.0 / d)
    inv = lax.rsqrt(var + LN_EPS)
    for c in range(nchunk):
        sl = slice(c * tn, (c + 1) * tn)
        o_ref[:, sl] = cen[c] * inv * lng_ref[:, sl] + lnb_ref[:, sl]


def _glu_ln_kernel(z_ref, wa_ref, wg_ref, x_ref, gt_ref, lng_ref, lnb_ref,
                   o_ref, acc_scr, *, nchunk, tn, alpha):
    c = pl.program_id(2)
    z = z_ref[...]
    a = _dot(z, wa_ref[...])
    g = _dot(z, wg_ref[...])
    m = a * jax.nn.sigmoid(g)
    acc_scr[c] = alpha * x_ref[...] + (1.0 + gt_ref[...]) * m

    @pl.when(c == nchunk - 1)
    def _():
        _layer_norm_chunks(acc_scr, lng_ref, lnb_ref, o_ref, nchunk, tn)


def _glu_ln(z, w_glu, layer, x, gate, ln_g, ln_b, tm, tn, alpha):
    bsz, L, d = x.shape
    nchunk = d // tn
    if gate.shape[1] == 1:
        gt_spec = pl.BlockSpec((None, 1, tn), lambda b, i, c: (b, 0, c))
    else:
        gt_spec = pl.BlockSpec((None, tm, tn), lambda b, i, c: (b, i, c))
    vec = pl.BlockSpec((1, d), lambda b, i, c: (0, 0))
    return pl.pallas_call(
        functools.partial(_glu_ln_kernel, nchunk=nchunk, tn=tn, alpha=alpha),
        grid=(bsz, L // tm, nchunk),
        in_specs=[
            pl.BlockSpec((None, tm, d), lambda b, i, c: (b, i, 0)),
            pl.BlockSpec((None, d, tn), lambda b, i, c: (layer, 0, c)),
            pl.BlockSpec((None, d, tn), lambda b, i, c: (layer, 0, nchunk + c)),
            pl.BlockSpec((None, tm, tn), lambda b, i, c: (b, i, c)),
            gt_spec,
            vec,
            vec,
        ],
        out_specs=pl.BlockSpec((None, tm, d), lambda b, i, c: (b, i, 0)),
        out_shape=jax.ShapeDtypeStruct((bsz, L, d), F32),
        scratch_shapes=[pltpu.VMEM((nchunk, tm, tn), F32)],
        compiler_params=_params(("parallel", "parallel", "arbitrary")),
        name="glu_ln",
    )(z, w_glu, w_glu, x, gate, ln_g, ln_b)


def _oproj_ln_kernel(o_in_ref, w_ref, x_ref, gt_ref, lng_ref, lnb_ref, o_ref, acc_scr,
                     *, nchunk, tn, alpha):
    c = pl.program_id(2)
    m = _dot(o_in_ref[...], w_ref[...])
    acc_scr[c] = alpha * x_ref[...] + (1.0 + gt_ref[...]) * m

    @pl.when(c == nchunk - 1)
    def _():
        _layer_norm_chunks(acc_scr, lng_ref, lnb_ref, o_ref, nchunk, tn)


def _oproj_ln(o_in, w_o, layer, x, gate, ln_g, ln_b, tm, tn, alpha):
    bsz, L, d = x.shape
    nchunk = d // tn
    if gate.shape[1] == 1:
        gt_spec = pl.BlockSpec((None, 1, tn), lambda b, i, c: (b, 0, c))
    else:
        gt_spec = pl.BlockSpec((None, tm, tn), lambda b, i, c: (b, i, c))
    vec = pl.BlockSpec((1, d), lambda b, i, c: (0, 0))
    return pl.pallas_call(
        functools.partial(_oproj_ln_kernel, nchunk=nchunk, tn=tn, alpha=alpha),
        grid=(bsz, L // tm, nchunk),
        in_specs=[
            pl.BlockSpec((None, tm, o_in.shape[2]), lambda b, i, c: (b, i, 0)),
            pl.BlockSpec((None, w_o.shape[1], tn), lambda b, i, c: (layer, 0, c)),
            pl.BlockSpec((None, tm, tn), lambda b, i, c: (b, i, c)),
            gt_spec,
            vec,
            vec,
        ],
        out_specs=pl.BlockSpec((None, tm, d), lambda b, i, c: (b, i, 0)),
        out_shape=jax.ShapeDtypeStruct((bsz, L, d), F32),
        scratch_shapes=[pltpu.VMEM((nchunk, tm, tn), F32)],
        compiler_params=_params(("parallel", "parallel", "arbitrary")),
        name="oproj_ln",
    )(o_in, w_o, x, gate, ln_g, ln_b)


LN_ROWS = 128


def _layer_norm_rows(r_ref, lng_ref, lnb_ref, o_ref):
    for r0 in range(0, r_ref.shape[0], LN_ROWS):
        rows = slice(r0, min(r0 + LN_ROWS, r_ref.shape[0]))
        r = r_ref[rows, :]
        mu = jnp.mean(r, axis=-1, keepdims=True)
        cen = r - mu
        var = jnp.mean(cen * cen, axis=-1, keepdims=True)
        o_ref[rows, :] = cen * lax.rsqrt(var + LN_EPS) * lng_ref[...] + lnb_ref[...]


def _ffn_kernel(x_ref, sc_ref, sh_ref, gt_ref, wg_ref, wv_ref, wc_ref, bc_ref, wd_ref,
                st_ref, lng_ref, lnb_ref, o_ref, cv_ref, h_scr, acc_scr, r_scr, prev_scr,
                *, nb, T, nchunk, ntile, ntotal, alpha):
    t = pl.program_id(0)
    c = pl.program_id(1)
    M = nb * T
    fc = wg_ref.shape[1]
    first_in_seq = lax.rem(t, ntile) == 0

    def chunk_step(first_chunk, last_chunk, finish_previous):
        if first_chunk:
            h_scr[...] = _modulate(x_ref, sc_ref, sh_ref)
            acc_scr[...] = alpha * x_ref[...]
        h = h_scr[...]
        g = _dot(h, wg_ref[...])
        v = _dot(h, wv_ref[...])
        row = lax.broadcasted_iota(jnp.int32, (M, 1), 0)
        if nb == 1:
            tpos = row
            p0 = prev_scr[0, 0:1, :]
            p1 = prev_scr[0, 1:2, :]
        else:
            tpos = lax.rem(row, T)
            p0 = jnp.broadcast_to(prev_scr[:, 0:1, :], (nb, T, fc)).reshape(M, fc)
            p1 = jnp.broadcast_to(prev_scr[:, 1:2, :], (nb, T, fc)).reshape(M, fc)
        s1 = jnp.where(tpos == 0, p1, pltpu.roll(g, 1, 0))
        s2 = jnp.where(tpos == 0, p0, jnp.where(tpos == 1, p1, pltpu.roll(g, 2, 0)))
        conv = bc_ref[...] + s2 * wc_ref[0:1, :] + s1 * wc_ref[1:2, :] + g * wc_ref[2:3, :]
        act = (_gelu(conv) * v).astype(BF16)
        total = acc_scr[...] + (1.0 + gt_ref[...]) * _dot(act, wd_ref[...])
        if last_chunk:
            r_scr[...] = total
        else:
            acc_scr[...] = total
        cv_ref[c] = g.reshape(nb, T, fc)[:, T - 2:, :]
        if finish_previous:
            _layer_norm_rows(r_scr, lng_ref, lnb_ref, o_ref)

    live = t < ntotal

    @pl.when(live & first_in_seq)
    def _():
        prev_scr[...] = st_ref[c]

    @pl.when(live & jnp.logical_not(first_in_seq))
    def _():
        prev_scr[...] = cv_ref[c]

    @pl.when((c == 0) & (t == 0))
    def _():
        chunk_step(True, False, False)

    @pl.when((c == 0) & (t > 0) & live)
    def _():
        chunk_step(True, False, True)

    @pl.when((c > 0) & (c < nchunk - 1) & live)
    def _():
        chunk_step(False, False, False)

    @pl.when((c == nchunk - 1) & live)
    def _():
        chunk_step(False, True, False)

    @pl.when((c == 0) & (t == ntotal))
    def _():
        _layer_norm_rows(r_scr, lng_ref, lnb_ref, o_ref)


def _ffn(x, sc, sh, gate, w_up, w_conv, b_conv, w_down, layer, conv_state, ln_g, ln_b,
         nb, T, fc, alpha):
    S, R, d = x.shape
    dff = w_down.shape[1]
    nchunk = dff // fc
    tm = nb * T
    ntile = R // tm
    ntotal = S * ntile
    assert nb == 1 or ntile == 1
    assert nchunk >= 3

    def tile_of(t):
        tc = jnp.minimum(t, ntotal - 1)
        return tc // ntile, lax.rem(tc, ntile)

    def prev_tile_of(t):
        tp = jnp.maximum(t - 1, 0)
        return tp // ntile, lax.rem(tp, ntile)

    def chunk_of(t, c):
        return jnp.where(t < ntotal, c, nchunk - 1)

    row_mode = dict(pipeline_mode=pl.Buffered(1)) if ntotal == 1 else {}

    def mod_spec(mod):
        if mod.shape[1] == 1:
            return pl.BlockSpec((None, 1, d), lambda t, c: (tile_of(t)[0], 0, 0))
        return pl.BlockSpec((None, tm, d), lambda t, c: (*tile_of(t), 0), **row_mode)

    vec = pl.BlockSpec((1, d), lambda t, c: (0, 0))
    st_spec = pl.BlockSpec((nchunk, nb, 2, fc), lambda t, c: (0, tile_of(t)[0], 0, 0))
    return pl.pallas_call(
        functools.partial(_ffn_kernel, nb=nb, T=T, nchunk=nchunk, ntile=ntile, ntotal=ntotal,
                          alpha=alpha),
        grid=(ntotal + 1, nchunk),
        in_specs=[
            pl.BlockSpec((None, tm, d), lambda t, c: (*tile_of(t), 0), **row_mode),
            mod_spec(sc),
            mod_spec(sh),
            mod_spec(gate),
            pl.BlockSpec((None, d, fc), lambda t, c: (layer, 0, chunk_of(t, c))),
            pl.BlockSpec((None, d, fc), lambda t, c: (layer, 0, nchunk + chunk_of(t, c))),
            pl.BlockSpec((w_conv.shape[0], fc), lambda t, c: (0, chunk_of(t, c))),
            pl.BlockSpec((1, fc), lambda t, c: (0, chunk_of(t, c))),
            pl.BlockSpec((None, fc, d), lambda t, c: (layer, chunk_of(t, c), 0)),
            st_spec,
            vec,
            vec,
        ],
        out_specs=[
            pl.BlockSpec((None, tm, d), lambda t, c: (*prev_tile_of(t), 0)),
            st_spec,
        ],
        out_shape=[
            jax.ShapeDtypeStruct((S, R, d), F32),
            jax.ShapeDtypeStruct(conv_state.shape, F32),
        ],
        scratch_shapes=[
            pltpu.VMEM((tm, d), BF16),
            pltpu.VMEM((tm, d), F32),
            pltpu.VMEM((tm, d), F32),
            pltpu.VMEM((nb, 2, fc), F32),
        ],
        compiler_params=_params(("arbitrary", "arbitrary")),
        name="conv_ffn",
    )(x, sc, sh, gate, w_up, w_up, w_conv, b_conv, w_down, conv_state, ln_g, ln_b)


def _qkv(x, sc, sh, w_qkv, layer, tm):
    da = w_qkv.shape[2] // 3
    (q,) = _mod_matmul(x, sc, sh, w_qkv, layer, 0, da, (BF16,), tm, da, "q_proj", Q_SCALE)
    k32, k16 = _mod_matmul(x, sc, sh, w_qkv, layer, da, da, (F32, BF16), tm, da, "k_proj",
                           f32_heads=(da // HEAD_DIM, HEAD_DIM))
    v32, v16 = _mod_matmul(x, sc, sh, w_qkv, layer, 2 * da, da, (F32, BF16), tm, da, "v_proj",
                           f32_heads=(da // (2 * HEAD_DIM), 2 * HEAD_DIM))
    return q, k32, v32, k16, v16


def _chunk_id(pos):
    assert CHUNK & (CHUNK - 1) == 0
    return lax.shift_right_logical(pos, CHUNK.bit_length() - 1)


def _sub_ln(o, g_ref, lam_init):
    o = o * lax.rsqrt(jnp.mean(o * o, axis=-1, keepdims=True) + LN_EPS)
    return o * g_ref[...] * (1.0 - lam_init)


def _lane_tile(x, width):
    if width < LANES:
        return x[:, :width]
    return jnp.tile(x, (1, width // LANES))


def _online_update(s, v, m_scr, l_scr, acc_scr, idx):
    m_old = m_scr[idx]
    m_new = jnp.maximum(m_old, jnp.max(s, axis=-1, keepdims=True))
    corr = jnp.exp2(m_old - m_new)
    p = jnp.exp2(s - _lane_tile(m_new, s.shape[1]))
    l_scr[idx] = corr * l_scr[idx] + jnp.sum(p, axis=-1, keepdims=True)
    acc_scr[idx] = _lane_tile(corr, v.shape[1]) * acc_scr[idx] + _dot(p.astype(BF16), v)
    m_scr[idx] = m_new


def _attn_prompt_kernel(lam_ref, q_ref, k_ref, v_ref, g_ref, o_ref, m_scr, l_scr, acc_scr,
                        *, tq, sb, lam_init):
    i = pl.program_id(2)
    maps = (slice(0, HEAD_DIM), slice(HEAD_DIM, 2 * HEAD_DIM))
    base = pl.multiple_of(i * tq, tq)

    row_chunk = _chunk_id(lax.broadcasted_iota(jnp.int32, (sb, sb), 0))
    col_chunk = _chunk_id(lax.broadcasted_iota(jnp.int32, (sb, sb), 1))
    diag_keep = col_chunk <= row_chunk
    for r in range(tq // sb):
        rows = slice(r * sb, (r + 1) * sb)
        nk = (r + 1) * sb
        v = v_ref[pl.ds(base, nk), :]
        for idx, cols in enumerate(maps):
            s = _dot_nt(q_ref[rows, cols], k_ref[pl.ds(base, nk), cols])
            s_diag = jnp.where(diag_keep, s[:, r * sb:], MASK_VALUE)
            s = s_diag if r == 0 else jnp.concatenate([s[:, :r * sb], s_diag], axis=1)
            m = jnp.max(s, axis=-1, keepdims=True)
            p = jnp.exp2(s - m)
            m_scr[idx, rows] = jnp.broadcast_to(m, (sb, LANES))
            l_scr[idx, rows] = jnp.broadcast_to(jnp.sum(p, axis=-1, keepdims=True), (sb, LANES))
            acc_scr[idx, rows] = _dot(p.astype(BF16), v)

    def full_tiles(j0, count):
        starts = [pl.multiple_of((j0 + t) * tq, tq) for t in range(count)]
        s = [[_dot_nt(q_ref[:, cols], k_ref[pl.ds(st, tq), cols]) for cols in maps]
             for st in starts]
        for t, st in enumerate(starts):
            v = v_ref[pl.ds(st, tq), :]
            for idx in range(len(maps)):
                _online_update(s[t][idx], v, m_scr, l_scr, acc_scr, idx)

    def pair(jj, carry):
        full_tiles(2 * jj, 2)
        return carry

    lax.fori_loop(0, i // 2, pair, 0)

    @pl.when(i % 2 == 1)
    def _():
        full_tiles(i - 1, 1)

    lam = lam_ref[0, 0]
    hw = acc_scr.shape[2]
    o = (acc_scr[0] * _lane_tile(1.0 / l_scr[0], hw)
         - acc_scr[1] * _lane_tile(lam / l_scr[1], hw))
    o_ref[...] = _sub_ln(o, g_ref, lam_init).astype(o_ref.dtype)


def _attn_prompt(lam, q, k, v, subln_g, tq, sb, lam_init):
    bsz, L, da = q.shape
    hw = 2 * HEAD_DIM
    nh = da // hw
    assert tq % sb == 0 and sb % CHUNK == 0
    return pl.pallas_call(
        functools.partial(_attn_prompt_kernel, tq=tq, sb=sb, lam_init=lam_init),
        grid=(bsz, nh, L // tq),
        in_specs=[
            pl.BlockSpec(memory_space=pltpu.SMEM),
            pl.BlockSpec((None, tq, hw), lambda b, h, i: (b, i, h)),
            pl.BlockSpec((None, L, hw), lambda b, h, i: (b, 0, h)),
            pl.BlockSpec((None, L, hw), lambda b, h, i: (b, 0, h)),
            pl.BlockSpec((1, hw), lambda b, h, i: (0, 0)),
        ],
        out_specs=pl.BlockSpec((None, tq, hw), lambda b, h, i: (b, i, h)),
        out_shape=jax.ShapeDtypeStruct((bsz, L, da), BF16),
        scratch_shapes=[
            pltpu.VMEM((2, tq, LANES), F32),
            pltpu.VMEM((2, tq, LANES), F32),
            pltpu.VMEM((2, tq, hw), F32),
        ],
        compiler_params=_params(("parallel", "parallel", "arbitrary")),
        name="attn_prompt",
    )(lam, q, k, v, subln_g)


def _attn_sample_kernel(lam_ref, q_ref, ck_ref, cv_ref, kn_ref, vn_ref, g_ref, o_ref,
                        m_scr, l_scr, acc_scr, vh_scr, *, P, T, pc, nsub, lam_init):
    j = pl.program_id(1)
    hw = 2 * HEAD_DIM
    qc = _chunk_id(P + lax.broadcasted_iota(jnp.int32, (T, 1), 0))

    @pl.when(j == 0)
    def _():
        m_scr[...] = jnp.full(m_scr.shape, MASK_VALUE, F32)
        l_scr[...] = jnp.zeros(l_scr.shape, F32)
        acc_scr[...] = jnp.zeros(acc_scr.shape, F32)

    keep_c = _chunk_id(j * pc + lax.broadcasted_iota(jnp.int32, (1, pc), 1)) <= qc
    nh = nsub // 2
    for half in range(2):
        vh_scr[half] = cv_ref[:, half * LANES:(half + 1) * LANES]
    scores = []
    for s in range(nsub):
        k = ck_ref[pl.ds(s, pc, stride=nsub), :].astype(BF16)
        q = q_ref[:, s * HEAD_DIM:(s + 1) * HEAD_DIM]
        scores.append(jnp.where(keep_c, _dot_nt(q, k), MASK_VALUE))
    for h in range(nh):
        v = jnp.concatenate([vh_scr[half, pl.ds(h, pc, stride=nh), :] for half in range(2)],
                            axis=1).astype(BF16)
        for s in (2 * h, 2 * h + 1):
            _online_update(scores[s], v, m_scr, l_scr, acc_scr, s)

    @pl.when(j == pl.num_programs(1) - 1)
    def _():
        keep_n = _chunk_id(P + lax.broadcasted_iota(jnp.int32, (1, T), 1)) <= qc
        lam = lam_ref[0, 0]
        new_scores = []
        for s in range(nsub):
            cols = slice(s * HEAD_DIM, (s + 1) * HEAD_DIM)
            new_scores.append(
                jnp.where(keep_n, _dot_nt(q_ref[:, cols], kn_ref[:, cols]), MASK_VALUE))
        for h in range(nh):
            vcols = slice(h * hw, (h + 1) * hw)
            vn = vn_ref[:, vcols]
            for s in (2 * h, 2 * h + 1):
                _online_update(new_scores[s], vn, m_scr, l_scr, acc_scr, s)
        for h in range(nh):
            vcols = slice(h * hw, (h + 1) * hw)
            o = (acc_scr[2 * h] * _lane_tile(1.0 / l_scr[2 * h], hw)
                 - acc_scr[2 * h + 1] * _lane_tile(lam / l_scr[2 * h + 1], hw))
            o_ref[:, vcols] = _sub_ln(o, g_ref, lam_init).astype(o_ref.dtype)


def _attn_sample(lam, q, cache_k, cache_v, layer, k_new, v_new, subln_g, pc, lam_init):
    bsz, T, da = q.shape
    na, _, P, nsub, _ = cache_k.shape
    hw = 2 * HEAD_DIM
    new_spec = pl.BlockSpec((None, T, da), lambda b, j: (b, 0, 0))
    return pl.pallas_call(
        functools.partial(_attn_sample_kernel, P=P, T=T, pc=pc, nsub=nsub, lam_init=lam_init),
        grid=(bsz, P // pc),
        in_specs=[
            pl.BlockSpec(memory_space=pltpu.SMEM),
            new_spec,
            pl.BlockSpec((None, None, pc * nsub, HEAD_DIM), lambda b, j: (layer, b, j, 0)),
            pl.BlockSpec((None, None, pc * (nsub // 2), hw), lambda b, j: (layer, b, j, 0)),
            new_spec,
            new_spec,
            pl.BlockSpec((1, hw), lambda b, j: (0, 0)),
        ],
        out_specs=new_spec,
        out_shape=jax.ShapeDtypeStruct((bsz, T, da), BF16),
        scratch_shapes=[
            pltpu.VMEM((nsub, T, LANES), F32),
            pltpu.VMEM((nsub, T, LANES), F32),
            pltpu.VMEM((nsub, T, hw), F32),
            pltpu.VMEM((2, pc * (nsub // 2), LANES), F32),
        ],
        compiler_params=_params(("parallel", "arbitrary")),
        name="attn_sample",
    )(lam, q, cache_k.reshape(na, bsz, P * nsub, HEAD_DIM),
      cache_v.reshape(na, bsz, P * (nsub // 2), hw), k_new, v_new, subln_g)


def _ssm_discretise(lam_re, lam_im, log_step, b_re, b_im):
    lr = jnp.minimum(lam_re, -1e-4)
    li = lam_im
    dt = jnp.exp(log_step)[:, None]
    mag = jnp.exp(lr * dt)
    abar_re = mag * jnp.cos(li * dt)
    abar_im = mag * jnp.sin(li * dt)
    nr = abar_re - 1.0
    ni = abar_im
    den = lr * lr + li * li
    kr = (nr * lr + ni * li) / den
    ki = (ni * lr - nr * li) / den
    bbar_re = kr[..., None] * b_re - ki[..., None] * b_im
    bbar_im = kr[..., None] * b_im + ki[..., None] * b_re
    return abar_re, abar_im, bbar_re, bbar_im


def _block_diag(m):
    nblk, gb, r, c = m.shape
    eye = jnp.eye(gb, dtype=m.dtype)
    return jnp.einsum("jgrc,gh->jgrhc", m, eye).reshape(nblk, gb * r, gb * c)


def _ssm_matrices(lam_re, lam_im, log_step, b_re, b_im, c_re, c_im):
    G = lam_re.shape[0]
    gb = SSM_GROUPS_PER_BLOCK
    nblk = G // gb
    abar_re, abar_im, bbar_re, bbar_im = _ssm_discretise(lam_re, lam_im, log_step, b_re, b_im)
    bt_re = bbar_re.reshape(nblk, gb, STATE_P, GROUP_CH).swapaxes(2, 3)
    bt_im = bbar_im.reshape(nblk, gb, STATE_P, GROUP_CH).swapaxes(2, 3)
    bd_b = jnp.concatenate([_block_diag(bt_re), _block_diag(bt_im)], axis=2).astype(BF16)
    ct_re = c_re.reshape(nblk, gb, GROUP_CH, STATE_P).swapaxes(2, 3)
    ct_im = c_im.reshape(nblk, gb, GROUP_CH, STATE_P).swapaxes(2, 3)
    bd_c = jnp.concatenate([_block_diag(ct_re), _block_diag(-ct_im)], axis=1).astype(BF16)
    a_re = abar_re.reshape(nblk, 1, gb * STATE_P)
    a_im = abar_im.reshape(nblk, 1, gb * STATE_P)
    return bd_b, a_re, a_im, bd_c


def _chunk_state(state, fc):
    nbt, w, dff = state.shape
    return state.reshape(nbt, w, dff // fc, fc).transpose(2, 0, 1, 3)


def _unchunk_state(state):
    nchunk, nbt, w, fc = state.shape
    return state.transpose(1, 2, 0, 3).reshape(nbt, w, nchunk * fc)


TM_PROMPT = 512
TN_GLU = 1024
FFN_CHUNK = 512
SSM_TT = 128
SSM_CB = 4
SAMPLE_PC = 512
TQ = 512
TQ_DIAG = 256


def kernel(x_prompt, x_sample, c_prompt, c_sample, cache_k, cache_v, state_ssm_re, state_ssm_im, state_conv, w_ada, b_ada, ln_g, ln_b, w_up, w_dconv, b_dconv, w_down, w_ssm_in, ssm_lam_re, ssm_lam_im, ssm_log_step, ssm_b_re, ssm_b_im, ssm_c_re, ssm_c_im, ssm_d, w_glu, w_qkv, lam_q1, lam_k1, lam_q2, lam_k2, subln_g, w_o):
    depth = w_ada.shape[0]
    bp, L, d = x_prompt.shape
    bs, T, _ = x_sample.shape
    dff = w_down.shape[1]
    alpha = (2 * depth) ** 0.25
    rows_s = bs * T

    mods = _ada(jnp.concatenate([c_prompt, c_sample], axis=0), w_ada, b_ada)
    w_up16, w_down16 = w_up.astype(BF16), w_down.astype(BF16)
    w_in16, w_glu16 = w_ssm_in.astype(BF16), w_glu.astype(BF16)
    w_qkv16, w_o16 = w_qkv.astype(BF16), w_o.astype(BF16)

    xp = x_prompt
    xs = x_sample.reshape(1, rows_s, d)
    zeros_conv = jnp.zeros((bp, state_conv.shape[2], dff), F32)
    outs = {name: [] for name in ("kp", "vp", "srp", "sip", "cvp", "ks", "vs", "srs", "sis", "cvs")}

    for i in range(depth):
        mp = [m.reshape(bp, 1, d) for m in jnp.split(mods[i, :bp], 6, axis=-1)]
        ms = [jnp.repeat(m, T, axis=0).reshape(1, rows_s, d)
              for m in jnp.split(mods[i, bp:], 6, axis=-1)]
        shp1, scp1, gtp1, shp2, scp2, gtp2 = mp
        shs1, scs1, gts1, shs2, scs2, gts2 = ms
        lng = ln_g[i].reshape(2, 1, d)
        lnb = ln_b[i].reshape(2, 1, d)
        j = i // 2
        if i % 2 == 0:
            bd_b, a_re, a_im, bd_c = _ssm_matrices(
                ssm_lam_re[j], ssm_lam_im[j], ssm_log_step[j], ssm_b_re[j], ssm_b_im[j],
                ssm_c_re[j], ssm_c_im[j])
            d_skip = ssm_d[j].reshape(1, d)
            gp = ssm_lam_re.shape[1] * ssm_lam_re.shape[2]
            (up,) = _mod_matmul(xp, scp1, shp1, w_in16, j, 0, d, (F32,), TM_PROMPT, d, "ssm_in")
            zero_state = jnp.zeros((bp, gp), F32)
            zp, srp, sip = _ssm_scan(up, d_skip, bd_b, a_re, a_im, bd_c, zero_state, zero_state,
                                     SSM_TT, SSM_CB)
            xp = _glu_ln(zp, w_glu16, j, xp, gtp1, lng[0], lnb[0], TM_PROMPT, TN_GLU, alpha)
            (us,) = _mod_matmul(xs, scs1, shs1, w_in16, j, 0, d, (F32,), rows_s, d, "ssm_in")
            zs, srs, sis = _ssm_scan(us.reshape(bs, T, d), d_skip, bd_b, a_re, a_im, bd_c,
                                     state_ssm_re[j].reshape(bs, gp),
                                     state_ssm_im[j].reshape(bs, gp), T, SSM_CB)
            xs = _glu_ln(zs.reshape(1, rows_s, d), w_glu16, j, xs, gts1, lng[0], lnb[0],
                         rows_s, TN_GLU, alpha)
            st_shape = ssm_lam_re.shape[1:]
            outs["srp"].append(srp.reshape((bp,) + st_shape))
            outs["sip"].append(sip.reshape((bp,) + st_shape))
            outs["srs"].append(srs.reshape((bs,) + st_shape))
            outs["sis"].append(sis.reshape((bs,) + st_shape))
        else:
            lam_init = 0.8 - 0.6 * math.exp(-0.3 * i)
            lam = (jnp.exp(jnp.sum(lam_q1[j] * lam_k1[j])) - jnp.exp(jnp.sum(lam_q2[j] * lam_k2[j]))
                   + lam_init).reshape(1, 1)
            sg = subln_g[j].reshape(1, 2 * HEAD_DIM)
            nsub = cache_k.shape[3]
            nhead = cache_v.shape[3]
            qp, kp, vp, kp16, vp16 = _qkv(xp, scp1, shp1, w_qkv16, j, TM_PROMPT)
            op = _attn_prompt(lam, qp, kp16, vp16, sg, TQ, TQ_DIAG, lam_init)
            xp = _oproj_ln(op, w_o16, j, xp, gtp1, lng[0], lnb[0], TM_PROMPT, d, alpha)
            qs, ksn, vsn, ks16, vs16 = _qkv(xs, scs1, shs1, w_qkv16, j, rows_s)
            da = w_o.shape[1]
            osm = _attn_sample(lam, qs.reshape(bs, T, da), cache_k, cache_v, j,
                               ks16.reshape(bs, T, da), vs16.reshape(bs, T, da), sg,
                               SAMPLE_PC, lam_init)
            xs = _oproj_ln(osm.reshape(1, rows_s, da), w_o16, j, xs, gts1, lng[0], lnb[0],
                           rows_s, d, alpha)
            outs["kp"].append(kp.reshape(bp, L, nsub, HEAD_DIM))
            outs["vp"].append(vp.reshape(bp, L, nhead, 2 * HEAD_DIM))
            outs["ks"].append(ksn.reshape(bs, T, nsub, HEAD_DIM))
            outs["vs"].append(vsn.reshape(bs, T, nhead, 2 * HEAD_DIM))

        bconv = b_dconv[i].reshape(1, dff)
        xp, cvp = _ffn(xp, scp2, shp2, gtp2, w_up16, w_dconv[i], bconv, w_down16, i,
                       _chunk_state(zeros_conv, FFN_CHUNK), lng[1], lnb[1],
                       1, TM_PROMPT, FFN_CHUNK, alpha)
        xs, cvs = _ffn(xs, scs2, shs2, gts2, w_up16, w_dconv[i], bconv, w_down16, i,
                       _chunk_state(state_conv[i], FFN_CHUNK), lng[1], lnb[1],
                       bs, T, FFN_CHUNK, alpha)
        outs["cvp"].append(_unchunk_state(cvp))
        outs["cvs"].append(_unchunk_state(cvs))

    return (xp, xs.reshape(bs, T, d),
            jnp.stack(outs["kp"]), jnp.stack(outs["vp"]),
            jnp.stack(outs["srp"]), jnp.stack(outs["sip"]), jnp.stack(outs["cvp"]),
            jnp.stack(outs["ks"]), jnp.stack(outs["vs"]),
            jnp.stack(outs["srs"]), jnp.stack(outs["sis"]), jnp.stack(outs["cvs"]))
```

```python
import functools
import math

import jax
import jax.numpy as jnp
from jax import lax
from jax.experimental import pallas as pl
from jax.experimental.pallas import tpu as pltpu

F32 = jnp.float32
BF16 = jnp.bfloat16

CHUNK = 64
HEAD_DIM = 128
GROUP_CH = 16
STATE_P = 64
LN_EPS = 1e-5
MASK_VALUE = -1e30
Q_SCALE = math.log2(math.e) * HEAD_DIM ** -0.5

SUBLANES = 8
LANES = 128
VMEM_LIMIT = 56 * 1024 * 1024

SSM_GROUPS_PER_BLOCK = LANES // GROUP_CH
SSM_HS = SSM_GROUPS_PER_BLOCK * STATE_P


def _params(sem):
    return pltpu.CompilerParams(dimension_semantics=sem, vmem_limit_bytes=VMEM_LIMIT)


def _gelu(x):
    return 0.5 * x * (1.0 + lax.erf(x * math.sqrt(0.5)))


def _dot(a, b):
    return jnp.dot(a, b, preferred_element_type=F32)


def _dot_nt(a, b):
    return lax.dot_general(a, b, (((1,), (1,)), ((), ())), preferred_element_type=F32)


def _ada_kernel(c_ref, w_ref, b_ref, o_ref):
    c = c_ref[...]
    a = (c * jax.nn.sigmoid(c)).astype(BF16)
    o_ref[...] = _dot(a, w_ref[...].astype(BF16)) + b_ref[...]


def _ada(c_all, w_ada, b_ada, tn=1024):
    depth, d, n = w_ada.shape
    rows = c_all.shape[0]
    return pl.pallas_call(
        _ada_kernel,
        grid=(depth, n // tn),
        in_specs=[
            pl.BlockSpec((rows, d), lambda l, j: (0, 0)),
            pl.BlockSpec((None, d, tn), lambda l, j: (l, 0, j)),
            pl.BlockSpec((None, 1, tn), lambda l, j: (l, 0, j)),
        ],
        out_specs=pl.BlockSpec((None, rows, tn), lambda l, j: (l, 0, j)),
        out_shape=jax.ShapeDtypeStruct((depth, rows, n), F32),
        compiler_params=_params(("parallel", "parallel")),
        name="ada",
    )(c_all, w_ada, b_ada.reshape(depth, 1, n))


def _modulate(x_ref, sc_ref, sh_ref):
    return (x_ref[...] * (1.0 + sc_ref[...]) + sh_ref[...]).astype(BF16)


def _store_heads(o_ref, r, heads):
    nh, hd = heads
    rows = r.shape[0]
    for h in range(nh):
        part = r[:, h * hd:(h + 1) * hd]
        if hd == LANES:
            o_ref[pl.ds(h, rows, stride=nh), :] = part
        else:
            o_ref[:, h, :] = part


def _modmm_kernel(x_ref, sc_ref, sh_ref, w_ref, *rest, bf16_scale, f32_heads):
    *out_refs, h_scr = rest

    @pl.when(pl.program_id(2) == 0)
    def _():
        h_scr[...] = _modulate(x_ref, sc_ref, sh_ref)

    r = _dot(h_scr[...], w_ref[...])
    for o_ref in out_refs:
        if o_ref.dtype != F32:
            o_ref[...] = (r if bf16_scale == 1.0 else r * bf16_scale).astype(o_ref.dtype)
        elif f32_heads is None:
            o_ref[...] = r
        else:
            _store_heads(o_ref, r, f32_heads)


def _mod_spec(mod, tm):
    if mod.shape[1] == 1:
        return pl.BlockSpec((None, 1, mod.shape[2]), lambda b, i, n: (b, 0, 0))
    return pl.BlockSpec((None, tm, mod.shape[2]), lambda b, i, n: (b, i, 0))


def _mod_matmul(x, sc, sh, w, layer, col0, n, out_dtypes, tm, tn, name, bf16_scale=1.0,
                f32_heads=None):
    bsz, L, d = x.shape
    assert col0 % tn == 0 and n % tn == 0
    jb = col0 // tn
    nt = L // tm
    flat_spec = pl.BlockSpec((None, tm, tn), lambda b, i, j: (b, i, j))
    out_specs, out_shape = [], []
    for dt in out_dtypes:
        if dt != F32 or f32_heads is None:
            out_specs.append(flat_spec)
            out_shape.append(jax.ShapeDtypeStruct((bsz, L, n), dt))
            continue
        nh, hd = f32_heads
        assert tn == n == nh * hd
        if hd == LANES:
            out_specs.append(pl.BlockSpec((tm * nh, hd), lambda b, i, j: (b * nt + i, 0)))
            out_shape.append(jax.ShapeDtypeStruct((bsz * L * nh, hd), dt))
        else:
            out_specs.append(pl.BlockSpec((None, tm, nh, hd), lambda b, i, j: (b, i, 0, 0)))
            out_shape.append(jax.ShapeDtypeStruct((bsz, L, nh, hd), dt))
    return pl.pallas_call(
        functools.partial(_modmm_kernel, bf16_scale=bf16_scale, f32_heads=f32_heads),
        grid=(bsz, nt, n // tn),
        in_specs=[
            pl.BlockSpec((None, tm, d), lambda b, i, j: (b, i, 0)),
            _mod_spec(sc, tm),
            _mod_spec(sh, tm),
            pl.BlockSpec((None, d, tn), lambda b, i, j: (layer, 0, jb + j)),
        ],
        out_specs=out_specs,
        out_shape=out_shape,
        scratch_shapes=[pltpu.VMEM((tm, d), BF16)],
        compiler_params=_params(("parallel", "parallel", "arbitrary")),
        name=name,
    )(x, sc, sh, w)


def _ssm_scan_tile(u_ref, bdb_ref, are_ref, aim_ref, tb, x, h_scr, tt, cb):
    hs = SSM_HS
    for c in range(cb):
        lanes = slice(c * LANES, (c + 1) * LANES)
        for b in range(SUBLANES):
            tb[c, pl.ds(b, tt, stride=SUBLANES), :] = u_ref[b, :, lanes]
        x[c] = _dot(tb[c].astype(BF16), bdb_ref[c]).reshape(tt, SUBLANES, 2 * hs)
    for c in range(cb):
        ar = jnp.broadcast_to(are_ref[c], (SUBLANES, hs))
        ai = jnp.broadcast_to(aim_ref[c], (SUBLANES, hs))
        hr = h_scr[c, :, :hs]
        hi = h_scr[c, :, hs:]
        for t in range(tt):
            hr, hi = ((ar * hr - ai * hi) + x[c, t, :, :hs],
                      (ar * hi + ai * hr) + x[c, t, :, hs:])
            x[c, t, :, :hs] = hr
            x[c, t, :, hs:] = hi
        h_scr[c, :, :hs] = hr
        h_scr[c, :, hs:] = hi


def _ssm_out_tile(d_ref, bdc_ref, tb, x, z_ref, tt, cb):
    for c in range(cb):
        lanes = slice(c * LANES, (c + 1) * LANES)
        hb = x[c].reshape(tt * SUBLANES, 2 * SSM_HS).astype(BF16)
        y = _dot(hb, bdc_ref[c])
        tb[c] = _gelu(y + d_ref[:, lanes] * tb[c])
        for b in range(SUBLANES):
            z_ref[b, :, lanes] = tb[c, pl.ds(b, tt, stride=SUBLANES), :].astype(z_ref.dtype)


def _ssm_kernel(u_ref, d_ref, bdb_ref, are_ref, aim_ref, bdc_ref, sre_ref, sim_ref,
                z_ref, ore_ref, oim_ref, tb0, tb1, x0, x1, h_scr, *, tt, nt, cb):
    i = pl.program_id(2)
    hs = SSM_HS
    bufs = ((tb0, x0), (tb1, x1))
    scan = functools.partial(_ssm_scan_tile, u_ref, bdb_ref, are_ref, aim_ref)
    out = functools.partial(_ssm_out_tile, d_ref, bdc_ref)

    @pl.when(i == 0)
    def _():
        for c in range(cb):
            h_scr[c, :, :hs] = sre_ref[:, c * hs:(c + 1) * hs]
            h_scr[c, :, hs:] = sim_ref[:, c * hs:(c + 1) * hs]
        scan(*bufs[0], h_scr, tt, cb)

    for parity in range(2):
        @pl.when((i > 0) & (i < nt) & (i % 2 == parity))
        def _():
            scan(*bufs[parity], h_scr, tt, cb)
            out(*bufs[1 - parity], z_ref, tt, cb)

    @pl.when(i == nt)
    def _():
        out(*bufs[(nt - 1) % 2], z_ref, tt, cb)
        for c in range(cb):
            ore_ref[:, c * hs:(c + 1) * hs] = h_scr[c, :, :hs]
            oim_ref[:, c * hs:(c + 1) * hs] = h_scr[c, :, hs:]


def _ssm_scan(u, d_skip, bd_b, a_re, a_im, bd_c, s_re, s_im, tt, cb):
    bsz, L, d = u.shape
    nblk = d // LANES
    hs = SSM_HS
    nt = L // tt
    bw = cb * LANES
    st_spec = pl.BlockSpec((SUBLANES, cb * hs), lambda j, g, i: (g, j))
    in_spec = pl.BlockSpec((SUBLANES, tt, bw),
                           lambda j, g, i: (g, jnp.minimum(i, nt - 1), j))
    out_spec = pl.BlockSpec((SUBLANES, tt, bw),
                            lambda j, g, i: (g, jnp.maximum(i - 1, 0), j))
    return pl.pallas_call(
        functools.partial(_ssm_kernel, tt=tt, nt=nt, cb=cb),
        grid=(nblk // cb, bsz // SUBLANES, nt + 1),
        in_specs=[
            in_spec,
            pl.BlockSpec((1, bw), lambda j, g, i: (0, j)),
            pl.BlockSpec((cb, LANES, 2 * hs), lambda j, g, i: (j, 0, 0)),
            pl.BlockSpec((cb, 1, hs), lambda j, g, i: (j, 0, 0)),
            pl.BlockSpec((cb, 1, hs), lambda j, g, i: (j, 0, 0)),
            pl.BlockSpec((cb, 2 * hs, LANES), lambda j, g, i: (j, 0, 0)),
            st_spec,
            st_spec,
        ],
        out_specs=[out_spec, st_spec, st_spec],
        out_shape=[
            jax.ShapeDtypeStruct((bsz, L, d), BF16),
            jax.ShapeDtypeStruct(s_re.shape, F32),
            jax.ShapeDtypeStruct(s_im.shape, F32),
        ],
        scratch_shapes=[
            pltpu.VMEM((cb, tt * SUBLANES, LANES), F32),
            pltpu.VMEM((cb, tt * SUBLANES, LANES), F32),
            pltpu.VMEM((cb, tt, SUBLANES, 2 * hs), F32),
            pltpu.VMEM((cb, tt, SUBLANES, 2 * hs), F32),
            pltpu.VMEM((cb, SUBLANES, 2 * hs), F32),
        ],
        compiler_params=_params(("parallel", "parallel", "arbitrary")),
        name="ssm_scan",
    )(u, d_skip, bd_b, a_re, a_im, bd_c, s_re, s_im)


def _layer_norm_chunks(acc_scr, lng_ref, lnb_ref, o_ref, nchunk, tn):
    d = nchunk * tn
    parts = [acc_scr[c] for c in range(nchunk)]
    mu = sum(jnp.sum(p, axis=-1, keepdims=True) for p in parts) * (1.0 / d)
    cen = [p - mu for p in parts]
    var = sum(jnp.sum(q * q, axis=-1, keepdims=True) for q in cen) * (1.0 / d)
    inv = lax.rsqrt(var + LN_EPS)
    for c in range(nchunk):
        sl = slice(c * tn, (c + 1) * tn)
        o_ref[:, sl] = cen[c] * inv * lng_ref[:, sl] + lnb_ref[:, sl]


def _glu_ln_kernel(z_ref, wa_ref, wg_ref, x_ref, gt_ref, lng_ref, lnb_ref,
                   o_ref, acc_scr, *, nchunk, tn, alpha):
    c = pl.program_id(2)
    z = z_ref[...]
    a = _dot(z, wa_ref[...])
    g = _dot(z, wg_ref[...])
    m = a * jax.nn.sigmoid(g)
    acc_scr[c] = alpha * x_ref[...] + (1.0 + gt_ref[...]) * m

    @pl.when(c == nchunk - 1)
    def _():
        _layer_norm_chunks(acc_scr, lng_ref, lnb_ref, o_ref, nchunk, tn)


def _glu_ln(z, w_glu, layer, x, gate, ln_g, ln_b, tm, tn, alpha):
    bsz, L, d = x.shape
    nchunk = d // tn
    if gate.shape[1] == 1:
        gt_spec = pl.BlockSpec((None, 1, tn), lambda b, i, c: (b, 0, c))
    else:
        gt_spec = pl.BlockSpec((None, tm, tn), lambda b, i, c: (b, i, c))
    vec = pl.BlockSpec((1, d), lambda b, i, c: (0, 0))
    return pl.pallas_call(
        functools.partial(_glu_ln_kernel, nchunk=nchunk, tn=tn, alpha=alpha),
        grid=(bsz, L // tm, nchunk),
        in_specs=[
            pl.BlockSpec((None, tm, d), lambda b, i, c: (b, i, 0)),
            pl.BlockSpec((None, d, tn), lambda b, i, c: (layer, 0, c)),
            pl.BlockSpec((None, d, tn), lambda b, i, c: (layer, 0, nchunk + c)),
            pl.BlockSpec((None, tm, tn), lambda b, i, c: (b, i, c)),
            gt_spec,
            vec,
            vec,
        ],
        out_specs=pl.BlockSpec((None, tm, d), lambda b, i, c: (b, i, 0)),
        out_shape=jax.ShapeDtypeStruct((bsz, L, d), F32),
        scratch_shapes=[pltpu.VMEM((nchunk, tm, tn), F32)],
        compiler_params=_params(("parallel", "parallel", "arbitrary")),
        name="glu_ln",
    )(z, w_glu, w_glu, x, gate, ln_g, ln_b)


def _oproj_ln_kernel(o_in_ref, w_ref, x_ref, gt_ref, lng_ref, lnb_ref, o_ref, acc_scr,
                     *, nchunk, tn, alpha):
    c = pl.program_id(2)
    m = _dot(o_in_ref[...], w_ref[...])
    acc_scr[c] = alpha * x_ref[...] + (1.0 + gt_ref[...]) * m

    @pl.when(c == nchunk - 1)
    def _():
        _layer_norm_chunks(acc_scr, lng_ref, lnb_ref, o_ref, nchunk, tn)


def _oproj_ln(o_in, w_o, layer, x, gate, ln_g, ln_b, tm, tn, alpha):
    bsz, L, d = x.shape
    nchunk = d // tn
    if gate.shape[1] == 1:
        gt_spec = pl.BlockSpec((None, 1, tn), lambda b, i, c: (b, 0, c))
    else:
        gt_spec = pl.BlockSpec((None, tm, tn), lambda b, i, c: (b, i, c))
    vec = pl.BlockSpec((1, d), lambda b, i, c: (0, 0))
    return pl.pallas_call(
        functools.partial(_oproj_ln_kernel, nchunk=nchunk, tn=tn, alpha=alpha),
        grid=(bsz, L // tm, nchunk),
        in_specs=[
            pl.BlockSpec((None, tm, o_in.shape[2]), lambda b, i, c: (b, i, 0)),
            pl.BlockSpec((None, w_o.shape[1], tn), lambda b, i, c: (layer, 0, c)),
            pl.BlockSpec((None, tm, tn), lambda b, i, c: (b, i, c)),
            gt_spec,
            vec,
            vec,
        ],
        out_specs=pl.BlockSpec((None, tm, d), lambda b, i, c: (b, i, 0)),
        out_shape=jax.ShapeDtypeStruct((bsz, L, d), F32),
        scratch_shapes=[pltpu.VMEM((nchunk, tm, tn), F32)],
        compiler_params=_params(("parallel", "parallel", "arbitrary")),
        name="oproj_ln",
    )(o_in, w_o, x, gate, ln_g, ln_b)


LN_ROWS = 128


def _layer_norm_rows(r_ref, lng_ref, lnb_ref, o_ref):
    for r0 in range(0, r_ref.shape[0], LN_ROWS):
        rows = slice(r0, min(r0 + LN_ROWS, r_ref.shape[0]))
        r = r_ref[rows, :]
        mu = jnp.mean(r, axis=-1, keepdims=True)
        cen = r - mu
        var = jnp.mean(cen * cen, axis=-1, keepdims=True)
        o_ref[rows, :] = cen * lax.rsqrt(var + LN_EPS) * lng_ref[...] + lnb_ref[...]


def _ffn_kernel(x_ref, sc_ref, sh_ref, gt_ref, wgv_ref, wc_ref, bc_ref, wd_ref,
                st_ref, lng_ref, lnb_ref, o_ref, cv_ref, h_scr, acc_scr, r_scr, prev_scr,
                *, nb, T, nchunk, ntile, ntotal, alpha):
    t = pl.program_id(0)
    c = pl.program_id(1)
    M = nb * T
    fc = wd_ref.shape[0]
    first_in_seq = lax.rem(t, ntile) == 0

    def chunk_step(first_chunk, last_chunk, finish_previous):
        if first_chunk:
            h_scr[...] = _modulate(x_ref, sc_ref, sh_ref)
            acc_scr[...] = alpha * x_ref[...]
        h = h_scr[...]
        gv = _dot(h, wgv_ref[...])
        g = gv[:, :fc]
        v = gv[:, fc:]
        row = lax.broadcasted_iota(jnp.int32, (M, 1), 0)
        if nb == 1:
            tpos = row
            p0 = prev_scr[0, 0:1, :]
            p1 = prev_scr[0, 1:2, :]
        else:
            tpos = lax.rem(row, T)
            p0 = jnp.broadcast_to(prev_scr[:, 0:1, :], (nb, T, fc)).reshape(M, fc)
            p1 = jnp.broadcast_to(prev_scr[:, 1:2, :], (nb, T, fc)).reshape(M, fc)
        s1 = jnp.where(tpos == 0, p1, pltpu.roll(g, 1, 0))
        s2 = jnp.where(tpos == 0, p0, jnp.where(tpos == 1, p1, pltpu.roll(g, 2, 0)))
        conv = bc_ref[...] + s2 * wc_ref[0:1, :] + s1 * wc_ref[1:2, :] + g * wc_ref[2:3, :]
        act = (_gelu(conv) * v).astype(BF16)
        total = acc_scr[...] + (1.0 + gt_ref[...]) * _dot(act, wd_ref[...])
        if last_chunk:
            r_scr[...] = total
        else:
            acc_scr[...] = total
        cv_ref[c] = g.reshape(nb, T, fc)[:, T - 2:, :]
        if finish_previous:
            _layer_norm_rows(r_scr, lng_ref, lnb_ref, o_ref)

    live = t < ntotal

    @pl.when(live & first_in_seq)
    def _():
        prev_scr[...] = st_ref[c]

    @pl.when(live & jnp.logical_not(first_in_seq))
    def _():
        prev_scr[...] = cv_ref[c]

    @pl.when((c == 0) & (t == 0))
    def _():
        chunk_step(True, False, False)

    @pl.when((c == 0) & (t > 0) & live)
    def _():
        chunk_step(True, False, True)

    @pl.when((c > 0) & (c < nchunk - 1) & live)
    def _():
        chunk_step(False, False, False)

    @pl.when((c == nchunk - 1) & live)
    def _():
        chunk_step(False, True, False)

    @pl.when((c == 0) & (t == ntotal))
    def _():
        _layer_norm_rows(r_scr, lng_ref, lnb_ref, o_ref)


def _ffn(x, sc, sh, gate, w_gv, w_conv, b_conv, w_down, layer, conv_state, ln_g, ln_b,
         nb, T, alpha):
    S, R, d = x.shape
    dff = w_down.shape[1]
    nchunk = w_gv.shape[1]
    fc = dff // nchunk
    tm = nb * T
    ntile = R // tm
    ntotal = S * ntile
    assert nb == 1 or ntile == 1
    assert nchunk >= 3

    def tile_of(t):
        tc = jnp.minimum(t, ntotal - 1)
        return tc // ntile, lax.rem(tc, ntile)

    def prev_tile_of(t):
        tp = jnp.maximum(t - 1, 0)
        return tp // ntile, lax.rem(tp, ntile)

    def chunk_of(t, c):
        return jnp.where(t < ntotal, c, nchunk - 1)

    row_mode = dict(pipeline_mode=pl.Buffered(1)) if ntotal == 1 else {}

    def mod_spec(mod):
        if mod.shape[1] == 1:
            return pl.BlockSpec((None, 1, d), lambda t, c: (tile_of(t)[0], 0, 0))
        return pl.BlockSpec((None, tm, d), lambda t, c: (*tile_of(t), 0), **row_mode)

    vec = pl.BlockSpec((1, d), lambda t, c: (0, 0))
    st_spec = pl.BlockSpec((nchunk, nb, 2, fc), lambda t, c: (0, tile_of(t)[0], 0, 0))
    return pl.pallas_call(
        functools.partial(_ffn_kernel, nb=nb, T=T, nchunk=nchunk, ntile=ntile, ntotal=ntotal,
                          alpha=alpha),
        grid=(ntotal + 1, nchunk),
        in_specs=[
            pl.BlockSpec((None, tm, d), lambda t, c: (*tile_of(t), 0), **row_mode),
            mod_spec(sc),
            mod_spec(sh),
            mod_spec(gate),
            pl.BlockSpec((None, None, d, 2 * fc), lambda t, c: (layer, chunk_of(t, c), 0, 0)),
            pl.BlockSpec((w_conv.shape[0], fc), lambda t, c: (0, chunk_of(t, c))),
            pl.BlockSpec((1, fc), lambda t, c: (0, chunk_of(t, c))),
            pl.BlockSpec((None, fc, d), lambda t, c: (layer, chunk_of(t, c), 0)),
            st_spec,
            vec,
            vec,
        ],
        out_specs=[
            pl.BlockSpec((None, tm, d), lambda t, c: (*prev_tile_of(t), 0)),
            st_spec,
        ],
        out_shape=[
            jax.ShapeDtypeStruct((S, R, d), F32),
            jax.ShapeDtypeStruct(conv_state.shape, F32),
        ],
        scratch_shapes=[
            pltpu.VMEM((tm, d), BF16),
            pltpu.VMEM((tm, d), F32),
            pltpu.VMEM((tm, d), F32),
            pltpu.VMEM((nb, 2, fc), F32),
        ],
        compiler_params=_params(("arbitrary", "arbitrary")),
        name="conv_ffn",
    )(x, sc, sh, gate, w_gv, w_conv, b_conv, w_down, conv_state, ln_g, ln_b)


def _qkv(x, sc, sh, w_qkv, layer, tm):
    da = w_qkv.shape[2] // 3
    (q,) = _mod_matmul(x, sc, sh, w_qkv, layer, 0, da, (BF16,), tm, da, "q_proj", Q_SCALE)
    k32, k16 = _mod_matmul(x, sc, sh, w_qkv, layer, da, da, (F32, BF16), tm, da, "k_proj",
                           f32_heads=(da // HEAD_DIM, HEAD_DIM))
    v32, v16 = _mod_matmul(x, sc, sh, w_qkv, layer, 2 * da, da, (F32, BF16), tm, da, "v_proj",
                           f32_heads=(da // (2 * HEAD_DIM), 2 * HEAD_DIM))
    return q, k32, v32, k16, v16


def _chunk_id(pos):
    assert CHUNK & (CHUNK - 1) == 0
    return lax.shift_right_logical(pos, CHUNK.bit_length() - 1)


def _sub_ln(o, g_ref, lam_init):
    o = o * lax.rsqrt(jnp.mean(o * o, axis=-1, keepdims=True) + LN_EPS)
    return o * g_ref[...] * (1.0 - lam_init)


def _lane_tile(x, width):
    if width < LANES:
        return x[:, :width]
    return jnp.tile(x, (1, width // LANES))


def _online_update(s, v, m_scr, l_scr, acc_scr, idx):
    m_old = m_scr[idx]
    m_new = jnp.maximum(m_old, jnp.max(s, axis=-1, keepdims=True))
    corr = jnp.exp2(m_old - m_new)
    p = jnp.exp2(s - _lane_tile(m_new, s.shape[1]))
    l_scr[idx] = corr * l_scr[idx] + jnp.sum(p, axis=-1, keepdims=True)
    acc_scr[idx] = _lane_tile(corr, v.shape[1]) * acc_scr[idx] + _dot(p.astype(BF16), v)
    m_scr[idx] = m_new


def _attn_prompt_kernel(lam_ref, q_ref, k_ref, v_ref, g_ref, o_ref, m_scr, l_scr, acc_scr,
                        *, tq, sb, lam_init):
    i = pl.program_id(2)
    maps = (slice(0, HEAD_DIM), slice(HEAD_DIM, 2 * HEAD_DIM))
    base = pl.multiple_of(i * tq, tq)

    row_chunk = _chunk_id(lax.broadcasted_iota(jnp.int32, (sb, sb), 0))
    col_chunk = _chunk_id(lax.broadcasted_iota(jnp.int32, (sb, sb), 1))
    diag_keep = col_chunk <= row_chunk
    for r in range(tq // sb):
        rows = slice(r * sb, (r + 1) * sb)
        nk = (r + 1) * sb
        v = v_ref[pl.ds(base, nk), :]
        for idx, cols in enumerate(maps):
            s = _dot_nt(q_ref[rows, cols], k_ref[pl.ds(base, nk), cols])
            s_diag = jnp.where(diag_keep, s[:, r * sb:], MASK_VALUE)
            s = s_diag if r == 0 else jnp.concatenate([s[:, :r * sb], s_diag], axis=1)
            m = jnp.max(s, axis=-1, keepdims=True)
            p = jnp.exp2(s - m)
            m_scr[idx, rows] = jnp.broadcast_to(m, (sb, LANES))
            l_scr[idx, rows] = jnp.broadcast_to(jnp.sum(p, axis=-1, keepdims=True), (sb, LANES))
            acc_scr[idx, rows] = _dot(p.astype(BF16), v)

    def full_tiles(j0, count):
        starts = [pl.multiple_of((j0 + t) * tq, tq) for t in range(count)]
        s = [[_dot_nt(q_ref[:, cols], k_ref[pl.ds(st, tq), cols]) for cols in maps]
             for st in starts]
        for t, st in enumerate(starts):
            v = v_ref[pl.ds(st, tq), :]
            for idx in range(len(maps)):
                _online_update(s[t][idx], v, m_scr, l_scr, acc_scr, idx)

    def pair(jj, carry):
        full_tiles(2 * jj, 2)
        return carry

    lax.fori_loop(0, i // 2, pair, 0)

    @pl.when(i % 2 == 1)
    def _():
        full_tiles(i - 1, 1)

    lam = lam_ref[0, 0]
    hw = acc_scr.shape[2]
    o = (acc_scr[0] * _lane_tile(1.0 / l_scr[0], hw)
         - acc_scr[1] * _lane_tile(lam / l_scr[1], hw))
    o_ref[...] = _sub_ln(o, g_ref, lam_init).astype(o_ref.dtype)


def _attn_prompt(lam, q, k, v, subln_g, tq, sb, lam_init):
    bsz, L, da = q.shape
    hw = 2 * HEAD_DIM
    nh = da // hw
    assert tq % sb == 0 and sb % CHUNK == 0
    return pl.pallas_call(
        functools.partial(_attn_prompt_kernel, tq=tq, sb=sb, lam_init=lam_init),
        grid=(bsz, nh, L // tq),
        in_specs=[
            pl.BlockSpec(memory_space=pltpu.SMEM),
            pl.BlockSpec((None, tq, hw), lambda b, h, i: (b, i, h)),
            pl.BlockSpec((None, L, hw), lambda b, h, i: (b, 0, h)),
            pl.BlockSpec((None, L, hw), lambda b, h, i: (b, 0, h)),
            pl.BlockSpec((1, hw), lambda b, h, i: (0, 0)),
        ],
        out_specs=pl.BlockSpec((None, tq, hw), lambda b, h, i: (b, i, h)),
        out_shape=jax.ShapeDtypeStruct((bsz, L, da), BF16),
        scratch_shapes=[
            pltpu.VMEM((2, tq, LANES), F32),
            pltpu.VMEM((2, tq, LANES), F32),
            pltpu.VMEM((2, tq, hw), F32),
        ],
        compiler_params=_params(("parallel", "parallel", "arbitrary")),
        name="attn_prompt",
    )(lam, q, k, v, subln_g)


def _attn_sample_kernel(lam_ref, q_ref, ck_ref, cv_ref, kn_ref, vn_ref, g_ref, o_ref,
                        m_scr, l_scr, acc_scr, vh_scr, *, P, T, pc, nsub, lam_init):
    j = pl.program_id(1)
    hw = 2 * HEAD_DIM
    qc = _chunk_id(P + lax.broadcasted_iota(jnp.int32, (T, 1), 0))

    @pl.when(j == 0)
    def _():
        m_scr[...] = jnp.full(m_scr.shape, MASK_VALUE, F32)
        l_scr[...] = jnp.zeros(l_scr.shape, F32)
        acc_scr[...] = jnp.zeros(acc_scr.shape, F32)

    keep_c = _chunk_id(j * pc + lax.broadcasted_iota(jnp.int32, (1, pc), 1)) <= qc
    nh = nsub // 2
    for half in range(2):
        vh_scr[half] = cv_ref[:, half * LANES:(half + 1) * LANES]
    scores = []
    for s in range(nsub):
        k = ck_ref[pl.ds(s, pc, stride=nsub), :].astype(BF16)
        q = q_ref[:, s * HEAD_DIM:(s + 1) * HEAD_DIM]
        scores.append(jnp.where(keep_c, _dot_nt(q, k), MASK_VALUE))
    for h in range(nh):
        v = jnp.concatenate([vh_scr[half, pl.ds(h, pc, stride=nh), :] for half in range(2)],
                            axis=1).astype(BF16)
        for s in (2 * h, 2 * h + 1):
            _online_update(scores[s], v, m_scr, l_scr, acc_scr, s)

    @pl.when(j == pl.num_programs(1) - 1)
    def _():
        keep_n = _chunk_id(P + lax.broadcasted_iota(jnp.int32, (1, T), 1)) <= qc
        lam = lam_ref[0, 0]
        new_scores = []
        for s in range(nsub):
            cols = slice(s * HEAD_DIM, (s + 1) * HEAD_DIM)
            new_scores.append(
                jnp.where(keep_n, _dot_nt(q_ref[:, cols], kn_ref[:, cols]), MASK_VALUE))
        for h in range(nh):
            vcols = slice(h * hw, (h + 1) * hw)
            vn = vn_ref[:, vcols]
            for s in (2 * h, 2 * h + 1):
                _online_update(new_scores[s], vn, m_scr, l_scr, acc_scr, s)
        for h in range(nh):
            vcols = slice(h * hw, (h + 1) * hw)
            o = (acc_scr[2 * h] * _lane_tile(1.0 / l_scr[2 * h], hw)
                 - acc_scr[2 * h + 1] * _lane_tile(lam / l_scr[2 * h + 1], hw))
            o_ref[:, vcols] = _sub_ln(o, g_ref, lam_init).astype(o_ref.dtype)


def _attn_sample(lam, q, cache_k, cache_v, layer, k_new, v_new, subln_g, pc, lam_init):
    bsz, T, da = q.shape
    na, _, P, nsub, _ = cache_k.shape
    hw = 2 * HEAD_DIM
    new_spec = pl.BlockSpec((None, T, da), lambda b, j: (b, 0, 0))
    return pl.pallas_call(
        functools.partial(_attn_sample_kernel, P=P, T=T, pc=pc, nsub=nsub, lam_init=lam_init),
        grid=(bsz, P // pc),
        in_specs=[
            pl.BlockSpec(memory_space=pltpu.SMEM),
            new_spec,
            pl.BlockSpec((None, None, pc * nsub, HEAD_DIM), lambda b, j: (layer, b, j, 0)),
            pl.BlockSpec((None, None, pc * (nsub // 2), hw), lambda b, j: (layer, b, j, 0)),
            new_spec,
            new_spec,
            pl.BlockSpec((1, hw), lambda b, j: (0, 0)),
        ],
        out_specs=new_spec,
        out_shape=jax.ShapeDtypeStruct((bsz, T, da), BF16),
        scratch_shapes=[
            pltpu.VMEM((nsub, T, LANES), F32),
            pltpu.VMEM((nsub, T, LANES), F32),
            pltpu.VMEM((nsub, T, hw), F32),
            pltpu.VMEM((2, pc * (nsub // 2), LANES), F32),
        ],
        compiler_params=_params(("parallel", "arbitrary")),
        name="attn_sample",
    )(lam, q, cache_k.reshape(na, bsz, P * nsub, HEAD_DIM),
      cache_v.reshape(na, bsz, P * (nsub // 2), hw), k_new, v_new, subln_g)


def _ssm_discretise(lam_re, lam_im, log_step, b_re, b_im):
    lr = jnp.minimum(lam_re, -1e-4)
    li = lam_im
    dt = jnp.exp(log_step)[:, None]
    mag = jnp.exp(lr * dt)
    abar_re = mag * jnp.cos(li * dt)
    abar_im = mag * jnp.sin(li * dt)
    nr = abar_re - 1.0
    ni = abar_im
    den = lr * lr + li * li
    kr = (nr * lr + ni * li) / den
    ki = (ni * lr - nr * li) / den
    bbar_re = kr[..., None] * b_re - ki[..., None] * b_im
    bbar_im = kr[..., None] * b_im + ki[..., None] * b_re
    return abar_re, abar_im, bbar_re, bbar_im


def _block_diag(m):
    nblk, gb, r, c = m.shape
    eye = jnp.eye(gb, dtype=m.dtype)
    return jnp.einsum("jgrc,gh->jgrhc", m, eye).reshape(nblk, gb * r, gb * c)


def _ssm_matrices(lam_re, lam_im, log_step, b_re, b_im, c_re, c_im):
    G = lam_re.shape[0]
    gb = SSM_GROUPS_PER_BLOCK
    nblk = G // gb
    abar_re, abar_im, bbar_re, bbar_im = _ssm_discretise(lam_re, lam_im, log_step, b_re, b_im)
    bt_re = bbar_re.reshape(nblk, gb, STATE_P, GROUP_CH).swapaxes(2, 3)
    bt_im = bbar_im.reshape(nblk, gb, STATE_P, GROUP_CH).swapaxes(2, 3)
    bd_b = jnp.concatenate([_block_diag(bt_re), _block_diag(bt_im)], axis=2).astype(BF16)
    ct_re = c_re.reshape(nblk, gb, GROUP_CH, STATE_P).swapaxes(2, 3)
    ct_im = c_im.reshape(nblk, gb, GROUP_CH, STATE_P).swapaxes(2, 3)
    bd_c = jnp.concatenate([_block_diag(ct_re), _block_diag(-ct_im)], axis=1).astype(BF16)
    a_re = abar_re.reshape(nblk, 1, gb * STATE_P)
    a_im = abar_im.reshape(nblk, 1, gb * STATE_P)
    return bd_b, a_re, a_im, bd_c


def _chunk_up_weights(w_up, fc):
    depth, d, two_dff = w_up.shape
    nchunk = two_dff // (2 * fc)
    w = w_up.astype(BF16).reshape(depth, d, 2, nchunk, fc)
    return w.transpose(0, 3, 1, 2, 4).reshape(depth, nchunk, d, 2 * fc)


def _chunk_state(state, fc):
    nbt, w, dff = state.shape
    return state.reshape(nbt, w, dff // fc, fc).transpose(2, 0, 1, 3)


def _unchunk_state(state):
    nchunk, nbt, w, fc = state.shape
    return state.transpose(1, 2, 0, 3).reshape(nbt, w, nchunk * fc)


TM_PROMPT = 512
TN_GLU = 1024
FFN_CHUNK = 512
SSM_TT = 128
SSM_CB = 4
SAMPLE_PC = 512
TQ = 512
TQ_DIAG = 256


def kernel(x_prompt, x_sample, c_prompt, c_sample, cache_k, cache_v, state_ssm_re, state_ssm_im, state_conv, w_ada, b_ada, ln_g, ln_b, w_up, w_dconv, b_dconv, w_down, w_ssm_in, ssm_lam_re, ssm_lam_im, ssm_log_step, ssm_b_re, ssm_b_im, ssm_c_re, ssm_c_im, ssm_d, w_glu, w_qkv, lam_q1, lam_k1, lam_q2, lam_k2, subln_g, w_o):
    depth = w_ada.shape[0]
    bp, L, d = x_prompt.shape
    bs, T, _ = x_sample.shape
    dff = w_down.shape[1]
    alpha = (2 * depth) ** 0.25
    rows_s = bs * T

    mods = _ada(jnp.concatenate([c_prompt, c_sample], axis=0), w_ada, b_ada)
    w_gv16, w_down16 = _chunk_up_weights(w_up, FFN_CHUNK), w_down.astype(BF16)
    w_in16, w_glu16 = w_ssm_in.astype(BF16), w_glu.astype(BF16)
    w_qkv16, w_o16 = w_qkv.astype(BF16), w_o.astype(BF16)

    xp = x_prompt
    xs = x_sample.reshape(1, rows_s, d)
    zeros_conv = jnp.zeros((bp, state_conv.shape[2], dff), F32)
    outs = {name: [] for name in ("kp", "vp", "srp", "sip", "cvp", "ks", "vs", "srs", "sis", "cvs")}

    for i in range(depth):
        mp = [m.reshape(bp, 1, d) for m in jnp.split(mods[i, :bp], 6, axis=-1)]
        ms = [jnp.repeat(m, T, axis=0).reshape(1, rows_s, d)
              for m in jnp.split(mods[i, bp:], 6, axis=-1)]
        shp1, scp1, gtp1, shp2, scp2, gtp2 = mp
        shs1, scs1, gts1, shs2, scs2, gts2 = ms
        lng = ln_g[i].reshape(2, 1, d)
        lnb = ln_b[i].reshape(2, 1, d)
        j = i // 2
        if i % 2 == 0:
            bd_b, a_re, a_im, bd_c = _ssm_matrices(
                ssm_lam_re[j], ssm_lam_im[j], ssm_log_step[j], ssm_b_re[j], ssm_b_im[j],
                ssm_c_re[j], ssm_c_im[j])
            d_skip = ssm_d[j].reshape(1, d)
            gp = ssm_lam_re.shape[1] * ssm_lam_re.shape[2]
            (up,) = _mod_matmul(xp, scp1, shp1, w_in16, j, 0, d, (F32,), TM_PROMPT, d, "ssm_in")
            zero_state = jnp.zeros((bp, gp), F32)
            zp, srp, sip = _ssm_scan(up, d_skip, bd_b, a_re, a_im, bd_c, zero_state, zero_state,
                                     SSM_TT, SSM_CB)
            xp = _glu_ln(zp, w_glu16, j, xp, gtp1, lng[0], lnb[0], TM_PROMPT, TN_GLU, alpha)
            (us,) = _mod_matmul(xs, scs1, shs1, w_in16, j, 0, d, (F32,), rows_s, d, "ssm_in")
            zs, srs, sis = _ssm_scan(us.reshape(bs, T, d), d_skip, bd_b, a_re, a_im, bd_c,
                                     state_ssm_re[j].reshape(bs, gp),
                                     state_ssm_im[j].reshape(bs, gp), T, SSM_CB)
            xs = _glu_ln(zs.reshape(1, rows_s, d), w_glu16, j, xs, gts1, lng[0], lnb[0],
                         rows_s, TN_GLU, alpha)
            st_shape = ssm_lam_re.shape[1:]
            outs["srp"].append(srp.reshape((bp,) + st_shape))
            outs["sip"].append(sip.reshape((bp,) + st_shape))
            outs["srs"].append(srs.reshape((bs,) + st_shape))
            outs["sis"].append(sis.reshape((bs,) + st_shape))
        else:
            lam_init = 0.8 - 0.6 * math.exp(-0.3 * i)
            lam = (jnp.exp(jnp.sum(lam_q1[j] * lam_k1[j])) - jnp.exp(jnp.sum(lam_q2[j] * lam_k2[j]))
                   + lam_init).reshape(1, 1)
            sg = subln_g[j].reshape(1, 2 * HEAD_DIM)
            nsub = cache_k.shape[3]
            nhead = cache_v.shape[3]
            qp, kp, vp, kp16, vp16 = _qkv(xp, scp1, shp1, w_qkv16, j, TM_PROMPT)
            op = _attn_prompt(lam, qp, kp16, vp16, sg, TQ, TQ_DIAG, lam_init)
            xp = _oproj_ln(op, w_o16, j, xp, gtp1, lng[0], lnb[0], TM_PROMPT, d, alpha)
            qs, ksn, vsn, ks16, vs16 = _qkv(xs, scs1, shs1, w_qkv16, j, rows_s)
            da = w_o.shape[1]
            osm = _attn_sample(lam, qs.reshape(bs, T, da), cache_k, cache_v, j,
                               ks16.reshape(bs, T, da), vs16.reshape(bs, T, da), sg,
                               SAMPLE_PC, lam_init)
            xs = _oproj_ln(osm.reshape(1, rows_s, da), w_o16, j, xs, gts1, lng[0], lnb[0],
                           rows_s, d, alpha)
            outs["kp"].append(kp.reshape(bp, L, nsub, HEAD_DIM))
            outs["vp"].append(vp.reshape(bp, L, nhead, 2 * HEAD_DIM))
            outs["ks"].append(ksn.reshape(bs, T, nsub, HEAD_DIM))
            outs["vs"].append(vsn.reshape(bs, T, nhead, 2 * HEAD_DIM))

        bconv = b_dconv[i].reshape(1, dff)
        xp, cvp = _ffn(xp, scp2, shp2, gtp2, w_gv16, w_dconv[i], bconv, w_down16, i,
                       _chunk_state(zeros_conv, FFN_CHUNK), lng[1], lnb[1],
                       1, TM_PROMPT, alpha)
        xs, cvs = _ffn(xs, scs2, shs2, gts2, w_gv16, w_dconv[i], bconv, w_down16, i,
                       _chunk_state(state_conv[i], FFN_CHUNK), lng[1], lnb[1],
                       bs, T, alpha)
        outs["cvp"].append(_unchunk_state(cvp))
        outs["cvs"].append(_unchunk_state(cvs))

    return (xp, xs.reshape(bs, T, d),
            jnp.stack(outs["kp"]), jnp.stack(outs["vp"]),
            jnp.stack(outs["srp"]), jnp.stack(outs["sip"]), jnp.stack(outs["cvp"]),
            jnp.stack(outs["ks"]), jnp.stack(outs["vs"]),
            jnp.stack(outs["srs"]), jnp.stack(outs["sis"]), jnp.stack(outs["cvs"]))
```

```python
import functools
import math

import jax
import jax.numpy as jnp
from jax import lax
from jax.experimental import pallas as pl
from jax.experimental.pallas import tpu as pltpu

F32 = jnp.float32
BF16 = jnp.bfloat16

CHUNK = 64
HEAD_DIM = 128
GROUP_CH = 16
STATE_P = 64
LN_EPS = 1e-5
MASK_VALUE = -1e30
Q_SCALE = math.log2(math.e) * HEAD_DIM ** -0.5

SUBLANES = 8
LANES = 128
VMEM_LIMIT = 56 * 1024 * 1024

SSM_GROUPS_PER_BLOCK = LANES // GROUP_CH
SSM_HS = SSM_GROUPS_PER_BLOCK * STATE_P


def _params(sem):
    return pltpu.CompilerParams(dimension_semantics=sem, vmem_limit_bytes=VMEM_LIMIT)


def _gelu(x):
    return 0.5 * x * (1.0 + lax.erf(x * math.sqrt(0.5)))


def _dot(a, b):
    return jnp.dot(a, b, preferred_element_type=F32)


def _dot_nt(a, b):
    return lax.dot_general(a, b, (((1,), (1,)), ((), ())), preferred_element_type=F32)


def _ada_kernel(c_ref, w_ref, b_ref, o_ref):
    c = c_ref[...]
    a = (c * jax.nn.sigmoid(c)).astype(BF16)
    o_ref[...] = _dot(a, w_ref[...].astype(BF16)) + b_ref[...]


def _ada(c_all, w_ada, b_ada, tn=1024):
    depth, d, n = w_ada.shape
    rows = c_all.shape[0]
    return pl.pallas_call(
        _ada_kernel,
        grid=(depth, n // tn),
        in_specs=[
            pl.BlockSpec((rows, d), lambda l, j: (0, 0)),
            pl.BlockSpec((None, d, tn), lambda l, j: (l, 0, j)),
            pl.BlockSpec((None, 1, tn), lambda l, j: (l, 0, j)),
        ],
        out_specs=pl.BlockSpec((None, rows, tn), lambda l, j: (l, 0, j)),
        out_shape=jax.ShapeDtypeStruct((depth, rows, n), F32),
        compiler_params=_params(("parallel", "parallel")),
        name="ada",
    )(c_all, w_ada, b_ada.reshape(depth, 1, n))


def _modulate(x_ref, sc_ref, sh_ref):
    return (x_ref[...] * (1.0 + sc_ref[...]) + sh_ref[...]).astype(BF16)


def _store_heads(o_ref, r, heads):
    nh, hd = heads
    rows = r.shape[0]
    for h in range(nh):
        part = r[:, h * hd:(h + 1) * hd]
        if hd == LANES:
            o_ref[pl.ds(h, rows, stride=nh), :] = part
        else:
            o_ref[:, h, :] = part


def _modmm_kernel(x_ref, sc_ref, sh_ref, w_ref, *rest, bf16_scale, f32_heads):
    *out_refs, h_scr = rest

    @pl.when(pl.program_id(2) == 0)
    def _():
        h_scr[...] = _modulate(x_ref, sc_ref, sh_ref)

    r = _dot(h_scr[...], w_ref[...])
    for o_ref in out_refs:
        if o_ref.dtype != F32:
            o_ref[...] = (r if bf16_scale == 1.0 else r * bf16_scale).astype(o_ref.dtype)
        elif f32_heads is None:
            o_ref[...] = r
        else:
            _store_heads(o_ref, r, f32_heads)


def _mod_spec(mod, tm):
    if mod.shape[1] == 1:
        return pl.BlockSpec((None, 1, mod.shape[2]), lambda b, i, n: (b, 0, 0))
    return pl.BlockSpec((None, tm, mod.shape[2]), lambda b, i, n: (b, i, 0))


def _mod_matmul(x, sc, sh, w, layer, col0, n, out_dtypes, tm, tn, name, bf16_scale=1.0,
                f32_heads=None):
    bsz, L, d = x.shape
    assert col0 % tn == 0 and n % tn == 0
    jb = col0 // tn
    nt = L // tm
    flat_spec = pl.BlockSpec((None, tm, tn), lambda b, i, j: (b, i, j))
    out_specs, out_shape = [], []
    for dt in out_dtypes:
        if dt != F32 or f32_heads is None:
            out_specs.append(flat_spec)
            out_shape.append(jax.ShapeDtypeStruct((bsz, L, n), dt))
            continue
        nh, hd = f32_heads
        assert tn == n == nh * hd
        if hd == LANES:
            out_specs.append(pl.BlockSpec((tm * nh, hd), lambda b, i, j: (b * nt + i, 0)))
            out_shape.append(jax.ShapeDtypeStruct((bsz * L * nh, hd), dt))
        else:
            out_specs.append(pl.BlockSpec((None, tm, nh, hd), lambda b, i, j: (b, i, 0, 0)))
            out_shape.append(jax.ShapeDtypeStruct((bsz, L, nh, hd), dt))
    return pl.pallas_call(
        functools.partial(_modmm_kernel, bf16_scale=bf16_scale, f32_heads=f32_heads),
        grid=(bsz, nt, n // tn),
        in_specs=[
            pl.BlockSpec((None, tm, d), lambda b, i, j: (b, i, 0)),
            _mod_spec(sc, tm),
            _mod_spec(sh, tm),
            pl.BlockSpec((None, d, tn), lambda b, i, j: (layer, 0, jb + j)),
        ],
        out_specs=out_specs,
        out_shape=out_shape,
        scratch_shapes=[pltpu.VMEM((tm, d), BF16)],
        compiler_params=_params(("parallel", "parallel", "arbitrary")),
        name=name,
    )(x, sc, sh, w)


def _ssm_scan_tile(u_ref, bdb_ref, are_ref, aim_ref, tb, x, h_scr, tt, cb):
    hs = SSM_HS
    for c in range(cb):
        lanes = slice(c * LANES, (c + 1) * LANES)
        for b in range(SUBLANES):
            tb[c, pl.ds(b, tt, stride=SUBLANES), :] = u_ref[b, :, lanes]
        x[c] = _dot(tb[c].astype(BF16), bdb_ref[c]).reshape(tt, SUBLANES, 2 * hs)
    for c in range(cb):
        ar = jnp.broadcast_to(are_ref[c], (SUBLANES, hs))
        ai = jnp.broadcast_to(aim_ref[c], (SUBLANES, hs))
        hr = h_scr[c, :, :hs]
        hi = h_scr[c, :, hs:]
        for t in range(tt):
            hr, hi = ((ar * hr - ai * hi) + x[c, t, :, :hs],
                      (ar * hi + ai * hr) + x[c, t, :, hs:])
            x[c, t, :, :hs] = hr
            x[c, t, :, hs:] = hi
        h_scr[c, :, :hs] = hr
        h_scr[c, :, hs:] = hi


def _ssm_out_tile(d_ref, bdc_ref, tb, x, z_ref, tt, cb):
    for c in range(cb):
        lanes = slice(c * LANES, (c + 1) * LANES)
        hb = x[c].reshape(tt * SUBLANES, 2 * SSM_HS).astype(BF16)
        y = _dot(hb, bdc_ref[c])
        tb[c] = _gelu(y + d_ref[:, lanes] * tb[c])
        for b in range(SUBLANES):
            z_ref[b, :, lanes] = tb[c, pl.ds(b, tt, stride=SUBLANES), :].astype(z_ref.dtype)


def _ssm_kernel(u_ref, d_ref, bdb_ref, are_ref, aim_ref, bdc_ref, sre_ref, sim_ref,
                z_ref, ore_ref, oim_ref, tb0, tb1, x0, x1, h_scr, *, tt, nt, cb):
    i = pl.program_id(2)
    hs = SSM_HS
    bufs = ((tb0, x0), (tb1, x1))
    scan = functools.partial(_ssm_scan_tile, u_ref, bdb_ref, are_ref, aim_ref)
    out = functools.partial(_ssm_out_tile, d_ref, bdc_ref)

    @pl.when(i == 0)
    def _():
        for c in range(cb):
            h_scr[c, :, :hs] = sre_ref[:, c * hs:(c + 1) * hs]
            h_scr[c, :, hs:] = sim_ref[:, c * hs:(c + 1) * hs]
        scan(*bufs[0], h_scr, tt, cb)

    for parity in range(2):
        @pl.when((i > 0) & (i < nt) & (i % 2 == parity))
        def _():
            scan(*bufs[parity], h_scr, tt, cb)
            out(*bufs[1 - parity], z_ref, tt, cb)

    @pl.when(i == nt)
    def _():
        out(*bufs[(nt - 1) % 2], z_ref, tt, cb)
        for c in range(cb):
            ore_ref[:, c * hs:(c + 1) * hs] = h_scr[c, :, :hs]
            oim_ref[:, c * hs:(c + 1) * hs] = h_scr[c, :, hs:]


def _ssm_scan(u, d_skip, bd_b, a_re, a_im, bd_c, s_re, s_im, tt, cb):
    bsz, L, d = u.shape
    nblk = d // LANES
    hs = SSM_HS
    nt = L // tt
    bw = cb * LANES
    st_spec = pl.BlockSpec((SUBLANES, cb * hs), lambda j, g, i: (g, j))
    in_spec = pl.BlockSpec((SUBLANES, tt, bw),
                           lambda j, g, i: (g, jnp.minimum(i, nt - 1), j))
    out_spec = pl.BlockSpec((SUBLANES, tt, bw),
                            lambda j, g, i: (g, jnp.maximum(i - 1, 0), j))
    return pl.pallas_call(
        functools.partial(_ssm_kernel, tt=tt, nt=nt, cb=cb),
        grid=(nblk // cb, bsz // SUBLANES, nt + 1),
        in_specs=[
            in_spec,
            pl.BlockSpec((1, bw), lambda j, g, i: (0, j)),
            pl.BlockSpec((cb, LANES, 2 * hs), lambda j, g, i: (j, 0, 0)),
            pl.BlockSpec((cb, 1, hs), lambda j, g, i: (j, 0, 0)),
            pl.BlockSpec((cb, 1, hs), lambda j, g, i: (j, 0, 0)),
            pl.BlockSpec((cb, 2 * hs, LANES), lambda j, g, i: (j, 0, 0)),
            st_spec,
            st_spec,
        ],
        out_specs=[out_spec, st_spec, st_spec],
        out_shape=[
            jax.ShapeDtypeStruct((bsz, L, d), BF16),
            jax.ShapeDtypeStruct(s_re.shape, F32),
            jax.ShapeDtypeStruct(s_im.shape, F32),
        ],
        scratch_shapes=[
            pltpu.VMEM((cb, tt * SUBLANES, LANES), F32),
            pltpu.VMEM((cb, tt * SUBLANES, LANES), F32),
            pltpu.VMEM((cb, tt, SUBLANES, 2 * hs), F32),
            pltpu.VMEM((cb, tt, SUBLANES, 2 * hs), F32),
            pltpu.VMEM((cb, SUBLANES, 2 * hs), F32),
        ],
        compiler_params=_params(("parallel", "parallel", "arbitrary")),
        name="ssm_scan",
    )(u, d_skip, bd_b, a_re, a_im, bd_c, s_re, s_im)


def _layer_norm_chunks(acc_scr, lng_ref, lnb_ref, o_ref, nchunk, tn):
    d = nchunk * tn
    parts = [acc_scr[c] for c in range(nchunk)]
    mu = sum(jnp.sum(p, axis=-1, keepdims=True) for p in parts) * (1.0 / d)
    cen = [p - mu for p in parts]
    var = sum(jnp.sum(q * q, axis=-1, keepdims=True) for q in cen) * (1.0 / d)
    inv = lax.rsqrt(var + LN_EPS)
    for c in range(nchunk):
        sl = slice(c * tn, (c + 1) * tn)
        o_ref[:, sl] = cen[c] * inv * lng_ref[:, sl] + lnb_ref[:, sl]


def _glu_ln_kernel(z_ref, wa_ref, wg_ref, x_ref, gt_ref, lng_ref, lnb_ref,
                   o_ref, acc_scr, *, nchunk, tn, alpha):
    c = pl.program_id(2)
    z = z_ref[...]
    a = _dot(z, wa_ref[...])
    g = _dot(z, wg_ref[...])
    m = a * jax.nn.sigmoid(g)
    acc_scr[c] = alpha * x_ref[...] + (1.0 + gt_ref[...]) * m

    @pl.when(c == nchunk - 1)
    def _():
        _layer_norm_chunks(acc_scr, lng_ref, lnb_ref, o_ref, nchunk, tn)


def _glu_ln(z, w_glu, layer, x, gate, ln_g, ln_b, tm, tn, alpha):
    bsz, L, d = x.shape
    nchunk = d // tn
    if gate.shape[1] == 1:
        gt_spec = pl.BlockSpec((None, 1, tn), lambda b, i, c: (b, 0, c))
    else:
        gt_spec = pl.BlockSpec((None, tm, tn), lambda b, i, c: (b, i, c))
    vec = pl.BlockSpec((1, d), lambda b, i, c: (0, 0))
    return pl.pallas_call(
        functools.partial(_glu_ln_kernel, nchunk=nchunk, tn=tn, alpha=alpha),
        grid=(bsz, L // tm, nchunk),
        in_specs=[
            pl.BlockSpec((None, tm, d), lambda b, i, c: (b, i, 0)),
            pl.BlockSpec((None, d, tn), lambda b, i, c: (layer, 0, c)),
            pl.BlockSpec((None, d, tn), lambda b, i, c: (layer, 0, nchunk + c)),
            pl.BlockSpec((None, tm, tn), lambda b, i, c: (b, i, c)),
            gt_spec,
            vec,
            vec,
        ],
        out_specs=pl.BlockSpec((None, tm, d), lambda b, i, c: (b, i, 0)),
        out_shape=jax.ShapeDtypeStruct((bsz, L, d), F32),
        scratch_shapes=[pltpu.VMEM((nchunk, tm, tn), F32)],
        compiler_params=_params(("parallel", "parallel", "arbitrary")),
        name="glu_ln",
    )(z, w_glu, w_glu, x, gate, ln_g, ln_b)


def _oproj_ln_kernel(o_in_ref, w_ref, x_ref, gt_ref, lng_ref, lnb_ref, o_ref, acc_scr,
                     *, nchunk, tn, alpha):
    c = pl.program_id(2)
    m = _dot(o_in_ref[...], w_ref[...])
    acc_scr[c] = alpha * x_ref[...] + (1.0 + gt_ref[...]) * m

    @pl.when(c == nchunk - 1)
    def _():
        _layer_norm_chunks(acc_scr, lng_ref, lnb_ref, o_ref, nchunk, tn)


def _oproj_ln(o_in, w_o, layer, x, gate, ln_g, ln_b, tm, tn, alpha):
    bsz, L, d = x.shape
    nchunk = d // tn
    if gate.shape[1] == 1:
        gt_spec = pl.BlockSpec((None, 1, tn), lambda b, i, c: (b, 0, c))
    else:
        gt_spec = pl.BlockSpec((None, tm, tn), lambda b, i, c: (b, i, c))
    vec = pl.BlockSpec((1, d), lambda b, i, c: (0, 0))
    return pl.pallas_call(
        functools.partial(_oproj_ln_kernel, nchunk=nchunk, tn=tn, alpha=alpha),
        grid=(bsz, L // tm, nchunk),
        in_specs=[
            pl.BlockSpec((None, tm, o_in.shape[2]), lambda b, i, c: (b, i, 0)),
            pl.BlockSpec((None, w_o.shape[1], tn), lambda b, i, c: (layer, 0, c)),
            pl.BlockSpec((None, tm, tn), lambda b, i, c: (b, i, c)),
            gt_spec,
            vec,
            vec,
        ],
        out_specs=pl.BlockSpec((None, tm, d), lambda b, i, c: (b, i, 0)),
        out_shape=jax.ShapeDtypeStruct((bsz, L, d), F32),
        scratch_shapes=[pltpu.VMEM((nchunk, tm, tn), F32)],
        compiler_params=_params(("parallel", "parallel", "arbitrary")),
        name="oproj_ln",
    )(o_in, w_o, x, gate, ln_g, ln_b)


def _ffn_kernel(x_ref, sc_ref, sh_ref, gt_ref, wg_ref, wv_ref, wc_ref, bc_ref, wd_ref,
                st_ref, lng_ref, lnb_ref, o_ref, cv_ref, h_scr, acc_scr, prev_scr,
                *, nb, T, nchunk, alpha):
    i = pl.program_id(1)
    c = pl.program_id(2)
    M = nb * T
    fc = wg_ref.shape[1]

    @pl.when(c == 0)
    def _():
        h_scr[...] = _modulate(x_ref, sc_ref, sh_ref)
        acc_scr[...] = jnp.zeros(acc_scr.shape, F32)

    @pl.when(i == 0)
    def _():
        prev_scr[...] = st_ref[c]

    @pl.when(i > 0)
    def _():
        prev_scr[...] = cv_ref[c]

    h = h_scr[...]
    g = _dot(h, wg_ref[...])
    v = _dot(h, wv_ref[...])

    row = lax.broadcasted_iota(jnp.int32, (M, 1), 0)
    if nb == 1:
        tpos = row
        p0 = prev_scr[0, 0:1, :]
        p1 = prev_scr[0, 1:2, :]
    else:
        tpos = lax.rem(row, T)
        p0 = jnp.broadcast_to(prev_scr[:, 0:1, :], (nb, T, fc)).reshape(M, fc)
        p1 = jnp.broadcast_to(prev_scr[:, 1:2, :], (nb, T, fc)).reshape(M, fc)
    s1 = jnp.where(tpos == 0, p1, pltpu.roll(g, 1, 0))
    s2 = jnp.where(tpos == 0, p0, jnp.where(tpos == 1, p1, pltpu.roll(g, 2, 0)))
    conv = bc_ref[...] + s2 * wc_ref[0:1, :] + s1 * wc_ref[1:2, :] + g * wc_ref[2:3, :]
    act = (_gelu(conv) * v).astype(BF16)
    acc_scr[...] += _dot(act, wd_ref[...])
    cv_ref[c] = g.reshape(nb, T, fc)[:, T - 2:, :]

    @pl.when(c == nchunk - 1)
    def _():
        r = alpha * x_ref[...] + (1.0 + gt_ref[...]) * acc_scr[...]
        mu = jnp.mean(r, axis=-1, keepdims=True)
        cen = r - mu
        var = jnp.mean(cen * cen, axis=-1, keepdims=True)
        o_ref[...] = cen * lax.rsqrt(var + LN_EPS) * lng_ref[...] + lnb_ref[...]


def _ffn(x, sc, sh, gate, w_up, w_conv, b_conv, w_down, layer, conv_state, ln_g, ln_b,
         nb, T, fc, alpha):
    S, R, d = x.shape
    dff = w_down.shape[1]
    nchunk = dff // fc
    tm = nb * T
    ntile = R // tm
    assert nb == 1 or ntile == 1

    def mod_spec(mod):
        if mod.shape[1] == 1:
            return pl.BlockSpec((None, 1, d), lambda s, i, c: (s, 0, 0))
        return pl.BlockSpec((None, tm, d), lambda s, i, c: (s, i, 0))

    vec = pl.BlockSpec((1, d), lambda s, i, c: (0, 0))
    st_spec = pl.BlockSpec((nchunk, nb, 2, fc), lambda s, i, c: (0, s, 0, 0))
    return pl.pallas_call(
        functools.partial(_ffn_kernel, nb=nb, T=T, nchunk=nchunk, alpha=alpha),
        grid=(S, ntile, nchunk),
        in_specs=[
            pl.BlockSpec((None, tm, d), lambda s, i, c: (s, i, 0)),
            mod_spec(sc),
            mod_spec(sh),
            mod_spec(gate),
            pl.BlockSpec((None, d, fc), lambda s, i, c: (layer, 0, c)),
            pl.BlockSpec((None, d, fc), lambda s, i, c: (layer, 0, nchunk + c)),
            pl.BlockSpec((w_conv.shape[0], fc), lambda s, i, c: (0, c)),
            pl.BlockSpec((1, fc), lambda s, i, c: (0, c)),
            pl.BlockSpec((None, fc, d), lambda s, i, c: (layer, c, 0)),
            st_spec,
            vec,
            vec,
        ],
        out_specs=[
            pl.BlockSpec((None, tm, d), lambda s, i, c: (s, i, 0)),
            st_spec,
        ],
        out_shape=[
            jax.ShapeDtypeStruct((S, R, d), F32),
            jax.ShapeDtypeStruct(conv_state.shape, F32),
        ],
        scratch_shapes=[
            pltpu.VMEM((tm, d), BF16),
            pltpu.VMEM((tm, d), F32),
            pltpu.VMEM((nb, 2, fc), F32),
        ],
        compiler_params=_params(("parallel", "arbitrary", "arbitrary")),
        name="conv_ffn",
    )(x, sc, sh, gate, w_up, w_up, w_conv, b_conv, w_down, conv_state, ln_g, ln_b)


def _qkv(x, sc, sh, w_qkv, layer, tm):
    da = w_qkv.shape[2] // 3
    (q,) = _mod_matmul(x, sc, sh, w_qkv, layer, 0, da, (BF16,), tm, da, "q_proj", Q_SCALE)
    k32, k16 = _mod_matmul(x, sc, sh, w_qkv, layer, da, da, (F32, BF16), tm, da, "k_proj",
                           f32_heads=(da // HEAD_DIM, HEAD_DIM))
    v32, v16 = _mod_matmul(x, sc, sh, w_qkv, layer, 2 * da, da, (F32, BF16), tm, da, "v_proj",
                           f32_heads=(da // (2 * HEAD_DIM), 2 * HEAD_DIM))
    return q, k32, v32, k16, v16


def _chunk_id(pos):
    assert CHUNK & (CHUNK - 1) == 0
    return lax.shift_right_logical(pos, CHUNK.bit_length() - 1)


def _sub_ln(o, g_ref, lam_init):
    o = o * lax.rsqrt(jnp.mean(o * o, axis=-1, keepdims=True) + LN_EPS)
    return o * g_ref[...] * (1.0 - lam_init)


def _lane_tile(x, width):
    if width < LANES:
        return x[:, :width]
    return jnp.tile(x, (1, width // LANES))


def _online_update(s, v, m_scr, l_scr, acc_scr, idx):
    m_old = m_scr[idx]
    m_new = jnp.maximum(m_old, jnp.max(s, axis=-1, keepdims=True))
    corr = jnp.exp2(m_old - m_new)
    p = jnp.exp2(s - _lane_tile(m_new, s.shape[1]))
    l_scr[idx] = corr * l_scr[idx] + jnp.sum(p, axis=-1, keepdims=True)
    acc_scr[idx] = _lane_tile(corr, v.shape[1]) * acc_scr[idx] + _dot(p.astype(BF16), v)
    m_scr[idx] = m_new


def _attn_prompt_kernel(lam_ref, q_ref, k_ref, v_ref, g_ref, o_ref, m_scr, l_scr, acc_scr,
                        *, tq, tk, sb, lam_init):
    i = pl.program_id(2)
    maps = (slice(0, HEAD_DIM), slice(HEAD_DIM, 2 * HEAD_DIM))
    base = pl.multiple_of(i * tq, tq)

    row_chunk = _chunk_id(lax.broadcasted_iota(jnp.int32, (sb, sb), 0))
    col_chunk = _chunk_id(lax.broadcasted_iota(jnp.int32, (sb, sb), 1))
    diag_keep = col_chunk <= row_chunk
    for r in range(tq // sb):
        rows = slice(r * sb, (r + 1) * sb)
        nk = (r + 1) * sb
        v = v_ref[pl.ds(base, nk), :]
        for idx, cols in enumerate(maps):
            s = _dot_nt(q_ref[rows, cols], k_ref[pl.ds(base, nk), cols])
            s_diag = jnp.where(diag_keep, s[:, r * sb:], MASK_VALUE)
            s = s_diag if r == 0 else jnp.concatenate([s[:, :r * sb], s_diag], axis=1)
            m = jnp.max(s, axis=-1, keepdims=True)
            p = jnp.exp2(s - m)
            m_scr[idx, rows] = jnp.broadcast_to(m, (sb, LANES))
            l_scr[idx, rows] = jnp.broadcast_to(jnp.sum(p, axis=-1, keepdims=True), (sb, LANES))
            acc_scr[idx, rows] = _dot(p.astype(BF16), v)

    def full_tiles(j0, count):
        starts = [pl.multiple_of((j0 + t) * tk, tk) for t in range(count)]
        s = [[_dot_nt(q_ref[:, cols], k_ref[pl.ds(st, tk), cols]) for cols in maps]
             for st in starts]
        for t, st in enumerate(starts):
            v = v_ref[pl.ds(st, tk), :]
            for idx in range(len(maps)):
                _online_update(s[t][idx], v, m_scr, l_scr, acc_scr, idx)

    def pair(jj, carry):
        full_tiles(2 * jj, 2)
        return carry

    nfull = i * (tq // tk)
    lax.fori_loop(0, nfull // 2, pair, 0)

    @pl.when(nfull % 2 == 1)
    def _():
        full_tiles(nfull - 1, 1)

    lam = lam_ref[0, 0]
    hw = acc_scr.shape[2]
    o = (acc_scr[0] * _lane_tile(1.0 / l_scr[0], hw)
         - acc_scr[1] * _lane_tile(lam / l_scr[1], hw))
    o_ref[...] = _sub_ln(o, g_ref, lam_init).astype(o_ref.dtype)


def _attn_prompt(lam, q, k, v, subln_g, tq, tk, sb, lam_init):
    bsz, L, da = q.shape
    hw = 2 * HEAD_DIM
    nh = da // hw
    assert tq % tk == 0 and tq % sb == 0 and sb % CHUNK == 0
    return pl.pallas_call(
        functools.partial(_attn_prompt_kernel, tq=tq, tk=tk, sb=sb, lam_init=lam_init),
        grid=(bsz, nh, L // tq),
        in_specs=[
            pl.BlockSpec(memory_space=pltpu.SMEM),
            pl.BlockSpec((None, tq, hw), lambda b, h, i: (b, i, h)),
            pl.BlockSpec((None, L, hw), lambda b, h, i: (b, 0, h)),
            pl.BlockSpec((None, L, hw), lambda b, h, i: (b, 0, h)),
            pl.BlockSpec((1, hw), lambda b, h, i: (0, 0)),
        ],
        out_specs=pl.BlockSpec((None, tq, hw), lambda b, h, i: (b, i, h)),
        out_shape=jax.ShapeDtypeStruct((bsz, L, da), BF16),
        scratch_shapes=[
            pltpu.VMEM((2, tq, LANES), F32),
            pltpu.VMEM((2, tq, LANES), F32),
            pltpu.VMEM((2, tq, hw), F32),
        ],
        compiler_params=_params(("parallel", "parallel", "arbitrary")),
        name="attn_prompt",
    )(lam, q, k, v, subln_g)


def _attn_sample_kernel(lam_ref, q_ref, ck_ref, cv_ref, kn_ref, vn_ref, g_ref, o_ref,
                        m_scr, l_scr, acc_scr, vh_scr, *, P, T, pc, nsub, lam_init):
    j = pl.program_id(1)
    hw = 2 * HEAD_DIM
    qc = _chunk_id(P + lax.broadcasted_iota(jnp.int32, (T, 1), 0))

    @pl.when(j == 0)
    def _():
        m_scr[...] = jnp.full(m_scr.shape, MASK_VALUE, F32)
        l_scr[...] = jnp.zeros(l_scr.shape, F32)
        acc_scr[...] = jnp.zeros(acc_scr.shape, F32)

    keep_c = _chunk_id(j * pc + lax.broadcasted_iota(jnp.int32, (1, pc), 1)) <= qc
    nh = nsub // 2
    for half in range(2):
        vh_scr[half] = cv_ref[:, half * LANES:(half + 1) * LANES]
    scores = []
    for s in range(nsub):
        k = ck_ref[pl.ds(s, pc, stride=nsub), :].astype(BF16)
        q = q_ref[:, s * HEAD_DIM:(s + 1) * HEAD_DIM]
        scores.append(jnp.where(keep_c, _dot_nt(q, k), MASK_VALUE))
    for h in range(nh):
        v = jnp.concatenate([vh_scr[half, pl.ds(h, pc, stride=nh), :] for half in range(2)],
                            axis=1).astype(BF16)
        for s in (2 * h, 2 * h + 1):
            _online_update(scores[s], v, m_scr, l_scr, acc_scr, s)

    @pl.when(j == pl.num_programs(1) - 1)
    def _():
        keep_n = _chunk_id(P + lax.broadcasted_iota(jnp.int32, (1, T), 1)) <= qc
        lam = lam_ref[0, 0]
        new_scores = []
        for s in range(nsub):
            cols = slice(s * HEAD_DIM, (s + 1) * HEAD_DIM)
            new_scores.append(
                jnp.where(keep_n, _dot_nt(q_ref[:, cols], kn_ref[:, cols]), MASK_VALUE))
        for h in range(nh):
            vcols = slice(h * hw, (h + 1) * hw)
            vn = vn_ref[:, vcols]
            for s in (2 * h, 2 * h + 1):
                _online_update(new_scores[s], vn, m_scr, l_scr, acc_scr, s)
        for h in range(nh):
            vcols = slice(h * hw, (h + 1) * hw)
            o = (acc_scr[2 * h] * _lane_tile(1.0 / l_scr[2 * h], hw)
                 - acc_scr[2 * h + 1] * _lane_tile(lam / l_scr[2 * h + 1], hw))
            o_ref[:, vcols] = _sub_ln(o, g_ref, lam_init).astype(o_ref.dtype)


def _attn_sample(lam, q, cache_k, cache_v, layer, k_new, v_new, subln_g, pc, lam_init):
    bsz, T, da = q.shape
    na, _, P, nsub, _ = cache_k.shape
    hw = 2 * HEAD_DIM
    new_spec = pl.BlockSpec((None, T, da), lambda b, j: (b, 0, 0))
    return pl.pallas_call(
        functools.partial(_attn_sample_kernel, P=P, T=T, pc=pc, nsub=nsub, lam_init=lam_init),
        grid=(bsz, P // pc),
        in_specs=[
            pl.BlockSpec(memory_space=pltpu.SMEM),
            new_spec,
            pl.BlockSpec((None, None, pc * nsub, HEAD_DIM), lambda b, j: (layer, b, j, 0)),
            pl.BlockSpec((None, None, pc * (nsub // 2), hw), lambda b, j: (layer, b, j, 0)),
            new_spec,
            new_spec,
            pl.BlockSpec((1, hw), lambda b, j: (0, 0)),
        ],
        out_specs=new_spec,
        out_shape=jax.ShapeDtypeStruct((bsz, T, da), BF16),
        scratch_shapes=[
            pltpu.VMEM((nsub, T, LANES), F32),
            pltpu.VMEM((nsub, T, LANES), F32),
            pltpu.VMEM((nsub, T, hw), F32),
            pltpu.VMEM((2, pc * (nsub // 2), LANES), F32),
        ],
        compiler_params=_params(("parallel", "arbitrary")),
        name="attn_sample",
    )(lam, q, cache_k.reshape(na, bsz, P * nsub, HEAD_DIM),
      cache_v.reshape(na, bsz, P * (nsub // 2), hw), k_new, v_new, subln_g)


def _ssm_discretise(lam_re, lam_im, log_step, b_re, b_im):
    lr = jnp.minimum(lam_re, -1e-4)
    li = lam_im
    dt = jnp.exp(log_step)[:, None]
    mag = jnp.exp(lr * dt)
    abar_re = mag * jnp.cos(li * dt)
    abar_im = mag * jnp.sin(li * dt)
    nr = abar_re - 1.0
    ni = abar_im
    den = lr * lr + li * li
    kr = (nr * lr + ni * li) / den
    ki = (ni * lr - nr * li) / den
    bbar_re = kr[..., None] * b_re - ki[..., None] * b_im
    bbar_im = kr[..., None] * b_im + ki[..., None] * b_re
    return abar_re, abar_im, bbar_re, bbar_im


def _block_diag(m):
    nblk, gb, r, c = m.shape
    eye = jnp.eye(gb, dtype=m.dtype)
    return jnp.einsum("jgrc,gh->jgrhc", m, eye).reshape(nblk, gb * r, gb * c)


def _ssm_matrices(lam_re, lam_im, log_step, b_re, b_im, c_re, c_im):
    G = lam_re.shape[0]
    gb = SSM_GROUPS_PER_BLOCK
    nblk = G // gb
    abar_re, abar_im, bbar_re, bbar_im = _ssm_discretise(lam_re, lam_im, log_step, b_re, b_im)
    bt_re = bbar_re.reshape(nblk, gb, STATE_P, GROUP_CH).swapaxes(2, 3)
    bt_im = bbar_im.reshape(nblk, gb, STATE_P, GROUP_CH).swapaxes(2, 3)
    bd_b = jnp.concatenate([_block_diag(bt_re), _block_diag(bt_im)], axis=2).astype(BF16)
    ct_re = c_re.reshape(nblk, gb, GROUP_CH, STATE_P).swapaxes(2, 3)
    ct_im = c_im.reshape(nblk, gb, GROUP_CH, STATE_P).swapaxes(2, 3)
    bd_c = jnp.concatenate([_block_diag(ct_re), _block_diag(-ct_im)], axis=1).astype(BF16)
    a_re = abar_re.reshape(nblk, 1, gb * STATE_P)
    a_im = abar_im.reshape(nblk, 1, gb * STATE_P)
    return bd_b, a_re, a_im, bd_c


def _chunk_state(state, fc):
    nbt, w, dff = state.shape
    return state.reshape(nbt, w, dff // fc, fc).transpose(2, 0, 1, 3)


def _unchunk_state(state):
    nchunk, nbt, w, fc = state.shape
    return state.transpose(1, 2, 0, 3).reshape(nbt, w, nchunk * fc)


TM_PROMPT = 512
TN_GLU = 1024
FFN_CHUNK = 512
SSM_TT = 128
SSM_CB = 4
SAMPLE_PC = 512
TQ = 1024
TK = 512
TQ_DIAG = 256


def kernel(x_prompt, x_sample, c_prompt, c_sample, cache_k, cache_v, state_ssm_re, state_ssm_im, state_conv, w_ada, b_ada, ln_g, ln_b, w_up, w_dconv, b_dconv, w_down, w_ssm_in, ssm_lam_re, ssm_lam_im, ssm_log_step, ssm_b_re, ssm_b_im, ssm_c_re, ssm_c_im, ssm_d, w_glu, w_qkv, lam_q1, lam_k1, lam_q2, lam_k2, subln_g, w_o):
    depth = w_ada.shape[0]
    bp, L, d = x_prompt.shape
    bs, T, _ = x_sample.shape
    dff = w_down.shape[1]
    alpha = (2 * depth) ** 0.25
    rows_s = bs * T

    mods = _ada(jnp.concatenate([c_prompt, c_sample], axis=0), w_ada, b_ada)
    w_up16, w_down16 = w_up.astype(BF16), w_down.astype(BF16)
    w_in16, w_glu16 = w_ssm_in.astype(BF16), w_glu.astype(BF16)
    w_qkv16, w_o16 = w_qkv.astype(BF16), w_o.astype(BF16)

    xp = x_prompt
    xs = x_sample.reshape(1, rows_s, d)
    zeros_conv = jnp.zeros((bp, state_conv.shape[2], dff), F32)
    outs = {name: [] for name in ("kp", "vp", "srp", "sip", "cvp", "ks", "vs", "srs", "sis", "cvs")}

    for i in range(depth):
        mp = [m.reshape(bp, 1, d) for m in jnp.split(mods[i, :bp], 6, axis=-1)]
        ms = [jnp.repeat(m, T, axis=0).reshape(1, rows_s, d)
              for m in jnp.split(mods[i, bp:], 6, axis=-1)]
        shp1, scp1, gtp1, shp2, scp2, gtp2 = mp
        shs1, scs1, gts1, shs2, scs2, gts2 = ms
        lng = ln_g[i].reshape(2, 1, d)
        lnb = ln_b[i].reshape(2, 1, d)
        j = i // 2
        if i % 2 == 0:
            bd_b, a_re, a_im, bd_c = _ssm_matrices(
                ssm_lam_re[j], ssm_lam_im[j], ssm_log_step[j], ssm_b_re[j], ssm_b_im[j],
                ssm_c_re[j], ssm_c_im[j])
            d_skip = ssm_d[j].reshape(1, d)
            gp = ssm_lam_re.shape[1] * ssm_lam_re.shape[2]
            (up,) = _mod_matmul(xp, scp1, shp1, w_in16, j, 0, d, (F32,), TM_PROMPT, d, "ssm_in")
            zero_state = jnp.zeros((bp, gp), F32)
            zp, srp, sip = _ssm_scan(up, d_skip, bd_b, a_re, a_im, bd_c, zero_state, zero_state,
                                     SSM_TT, SSM_CB)
            xp = _glu_ln(zp, w_glu16, j, xp, gtp1, lng[0], lnb[0], TM_PROMPT, TN_GLU, alpha)
            (us,) = _mod_matmul(xs, scs1, shs1, w_in16, j, 0, d, (F32,), rows_s, d, "ssm_in")
            zs, srs, sis = _ssm_scan(us.reshape(bs, T, d), d_skip, bd_b, a_re, a_im, bd_c,
                                     state_ssm_re[j].reshape(bs, gp),
                                     state_ssm_im[j].reshape(bs, gp), T, SSM_CB)
            xs = _glu_ln(zs.reshape(1, rows_s, d), w_glu16, j, xs, gts1, lng[0], lnb[0],
                         rows_s, TN_GLU, alpha)
            st_shape = ssm_lam_re.shape[1:]
            outs["srp"].append(srp.reshape((bp,) + st_shape))
            outs["sip"].append(sip.reshape((bp,) + st_shape))
            outs["srs"].append(srs.reshape((bs,) + st_shape))
            outs["sis"].append(sis.reshape((bs,) + st_shape))
        else:
            lam_init = 0.8 - 0.6 * math.exp(-0.3 * i)
            lam = (jnp.exp(jnp.sum(lam_q1[j] * lam_k1[j])) - jnp.exp(jnp.sum(lam_q2[j] * lam_k2[j]))
                   + lam_init).reshape(1, 1)
            sg = subln_g[j].reshape(1, 2 * HEAD_DIM)
            nsub = cache_k.shape[3]
            nhead = cache_v.shape[3]
            qp, kp, vp, kp16, vp16 = _qkv(xp, scp1, shp1, w_qkv16, j, TM_PROMPT)
            op = _attn_prompt(lam, qp, kp16, vp16, sg, TQ, TK, TQ_DIAG, lam_init)
            xp = _oproj_ln(op, w_o16, j, xp, gtp1, lng[0], lnb[0], TM_PROMPT, d, alpha)
            qs, ksn, vsn, ks16, vs16 = _qkv(xs, scs1, shs1, w_qkv16, j, rows_s)
            da = w_o.shape[1]
            osm = _attn_sample(lam, qs.reshape(bs, T, da), cache_k, cache_v, j,
                               ks16.reshape(bs, T, da), vs16.reshape(bs, T, da), sg,
                               SAMPLE_PC, lam_init)
            xs = _oproj_ln(osm.reshape(1, rows_s, da), w_o16, j, xs, gts1, lng[0], lnb[0],
                           rows_s, d, alpha)
            outs["kp"].append(kp.reshape(bp, L, nsub, HEAD_DIM))
            outs["vp"].append(vp.reshape(bp, L, nhead, 2 * HEAD_DIM))
            outs["ks"].append(ksn.reshape(bs, T, nsub, HEAD_DIM))
            outs["vs"].append(vsn.reshape(bs, T, nhead, 2 * HEAD_DIM))

        bconv = b_dconv[i].reshape(1, dff)
        xp, cvp = _ffn(xp, scp2, shp2, gtp2, w_up16, w_dconv[i], bconv, w_down16, i,
                       _chunk_state(zeros_conv, FFN_CHUNK), lng[1], lnb[1],
                       1, TM_PROMPT, FFN_CHUNK, alpha)
        xs, cvs = _ffn(xs, scs2, shs2, gts2, w_up16, w_dconv[i], bconv, w_down16, i,
                       _chunk_state(state_conv[i], FFN_CHUNK), lng[1], lnb[1],
                       bs, T, FFN_CHUNK, alpha)
        outs["cvp"].append(_unchunk_state(cvp))
        outs["cvs"].append(_unchunk_state(cvs))

    return (xp, xs.reshape(bs, T, d),
            jnp.stack(outs["kp"]), jnp.stack(outs["vp"]),
            jnp.stack(outs["srp"]), jnp.stack(outs["sip"]), jnp.stack(outs["cvp"]),
            jnp.stack(outs["ks"]), jnp.stack(outs["vs"]),
            jnp.stack(outs["srs"]), jnp.stack(outs["sis"]), jnp.stack(outs["cvs"]))
```

```python
import functools
import math

import jax
import jax.numpy as jnp
from jax import lax
from jax.experimental import pallas as pl
from jax.experimental.pallas import tpu as pltpu

F32 = jnp.float32
BF16 = jnp.bfloat16

CHUNK = 64
HEAD_DIM = 128
GROUP_CH = 16
STATE_P = 64
LN_EPS = 1e-5
MASK_VALUE = -1e30
Q_SCALE = math.log2(math.e) * HEAD_DIM ** -0.5

SUBLANES = 8
LANES = 128
VMEM_LIMIT = 56 * 1024 * 1024

SSM_GROUPS_PER_BLOCK = LANES // GROUP_CH
SSM_HS = SSM_GROUPS_PER_BLOCK * STATE_P


def _params(sem):
    return pltpu.CompilerParams(dimension_semantics=sem, vmem_limit_bytes=VMEM_LIMIT)


def _gelu(x):
    return 0.5 * x * (1.0 + lax.erf(x * math.sqrt(0.5)))


def _dot(a, b):
    return jnp.dot(a, b, preferred_element_type=F32)


def _dot_nt(a, b):
    return lax.dot_general(a, b, (((1,), (1,)), ((), ())), preferred_element_type=F32)


def _ada_kernel(c_ref, w_ref, b_ref, o_ref):
    c = c_ref[...]
    a = (c * jax.nn.sigmoid(c)).astype(BF16)
    o_ref[...] = _dot(a, w_ref[...].astype(BF16)) + b_ref[...]


def _ada(c_all, w_ada, b_ada, tn=1024):
    depth, d, n = w_ada.shape
    rows = c_all.shape[0]
    return pl.pallas_call(
        _ada_kernel,
        grid=(depth, n // tn),
        in_specs=[
            pl.BlockSpec((rows, d), lambda l, j: (0, 0)),
            pl.BlockSpec((None, d, tn), lambda l, j: (l, 0, j)),
            pl.BlockSpec((None, 1, tn), lambda l, j: (l, 0, j)),
        ],
        out_specs=pl.BlockSpec((None, rows, tn), lambda l, j: (l, 0, j)),
        out_shape=jax.ShapeDtypeStruct((depth, rows, n), F32),
        compiler_params=_params(("parallel", "parallel")),
        name="ada",
    )(c_all, w_ada, b_ada.reshape(depth, 1, n))


def _modulate(x_ref, sc_ref, sh_ref):
    return (x_ref[...] * (1.0 + sc_ref[...]) + sh_ref[...]).astype(BF16)


def _store_heads(o_ref, r, heads):
    nh, hd = heads
    rows = r.shape[0]
    for h in range(nh):
        part = r[:, h * hd:(h + 1) * hd]
        if hd == LANES:
            o_ref[pl.ds(h, rows, stride=nh), :] = part
        else:
            o_ref[:, h, :] = part


def _modmm_kernel(x_ref, sc_ref, sh_ref, w_ref, *rest, bf16_scale, f32_heads):
    *out_refs, h_scr = rest

    @pl.when(pl.program_id(2) == 0)
    def _():
        h_scr[...] = _modulate(x_ref, sc_ref, sh_ref)

    r = _dot(h_scr[...], w_ref[...])
    for o_ref in out_refs:
        if o_ref.dtype != F32:
            o_ref[...] = (r if bf16_scale == 1.0 else r * bf16_scale).astype(o_ref.dtype)
        elif f32_heads is None:
            o_ref[...] = r
        else:
            _store_heads(o_ref, r, f32_heads)


def _mod_spec(mod, tm):
    if mod.shape[1] == 1:
        return pl.BlockSpec((None, 1, mod.shape[2]), lambda b, i, n: (b, 0, 0))
    return pl.BlockSpec((None, tm, mod.shape[2]), lambda b, i, n: (b, i, 0))


def _mod_matmul(x, sc, sh, w, layer, col0, n, out_dtypes, tm, tn, name, bf16_scale=1.0,
                f32_heads=None):
    bsz, L, d = x.shape
    assert col0 % tn == 0 and n % tn == 0
    jb = col0 // tn
    nt = L // tm
    flat_spec = pl.BlockSpec((None, tm, tn), lambda b, i, j: (b, i, j))
    out_specs, out_shape = [], []
    for dt in out_dtypes:
        if dt != F32 or f32_heads is None:
            out_specs.append(flat_spec)
            out_shape.append(jax.ShapeDtypeStruct((bsz, L, n), dt))
            continue
        nh, hd = f32_heads
        assert tn == n == nh * hd
        if hd == LANES:
            out_specs.append(pl.BlockSpec((tm * nh, hd), lambda b, i, j: (b * nt + i, 0)))
            out_shape.append(jax.ShapeDtypeStruct((bsz * L * nh, hd), dt))
        else:
            out_specs.append(pl.BlockSpec((None, tm, nh, hd), lambda b, i, j: (b, i, 0, 0)))
            out_shape.append(jax.ShapeDtypeStruct((bsz, L, nh, hd), dt))
    return pl.pallas_call(
        functools.partial(_modmm_kernel, bf16_scale=bf16_scale, f32_heads=f32_heads),
        grid=(bsz, nt, n // tn),
        in_specs=[
            pl.BlockSpec((None, tm, d), lambda b, i, j: (b, i, 0)),
            _mod_spec(sc, tm),
            _mod_spec(sh, tm),
            pl.BlockSpec((None, d, tn), lambda b, i, j: (layer, 0, jb + j)),
        ],
        out_specs=out_specs,
        out_shape=out_shape,
        scratch_shapes=[pltpu.VMEM((tm, d), BF16)],
        compiler_params=_params(("parallel", "parallel", "arbitrary")),
        name=name,
    )(x, sc, sh, w)


def _ssm_scan_tile(u_ref, bdb_ref, are_ref, aim_ref, tb, x, h_scr, tt, blocks):
    hs = SSM_HS
    for c in blocks:
        lanes = slice(c * LANES, (c + 1) * LANES)
        for b in range(SUBLANES):
            tb[c, pl.ds(b, tt, stride=SUBLANES), :] = u_ref[b, :, lanes]
        x[c] = _dot(tb[c].astype(BF16), bdb_ref[c]).reshape(tt, SUBLANES, 2 * hs)
    for c in blocks:
        ar = jnp.broadcast_to(are_ref[c], (SUBLANES, hs))
        ai = jnp.broadcast_to(aim_ref[c], (SUBLANES, hs))
        hr = h_scr[c, :, :hs]
        hi = h_scr[c, :, hs:]
        for t in range(tt):
            hr, hi = ((ar * hr - ai * hi) + x[c, t, :, :hs],
                      (ar * hi + ai * hr) + x[c, t, :, hs:])
            x[c, t, :, :hs] = hr
            x[c, t, :, hs:] = hi
        h_scr[c, :, :hs] = hr
        h_scr[c, :, hs:] = hi


def _ssm_out_tile(d_ref, bdc_ref, tb, x, z_ref, tt, blocks):
    for c in blocks:
        lanes = slice(c * LANES, (c + 1) * LANES)
        hb = x[c].reshape(tt * SUBLANES, 2 * SSM_HS).astype(BF16)
        y = _dot(hb, bdc_ref[c])
        tb[c] = _gelu(y + d_ref[:, lanes] * tb[c])
        for b in range(SUBLANES):
            z_ref[b, :, lanes] = tb[c, pl.ds(b, tt, stride=SUBLANES), :].astype(z_ref.dtype)


def _ssm_kernel(u_ref, d_ref, bdb_ref, are_ref, aim_ref, bdc_ref, sre_ref, sim_ref,
                z_ref, ore_ref, oim_ref, tb0, tb1, x0, x1, h_scr, *, tt, nt, cb):
    i = pl.program_id(2)
    hs = SSM_HS
    bufs = ((tb0, x0), (tb1, x1))
    scan = functools.partial(_ssm_scan_tile, u_ref, bdb_ref, are_ref, aim_ref)
    out = functools.partial(_ssm_out_tile, d_ref, bdc_ref)

    @pl.when(i == 0)
    def _():
        for c in range(cb):
            h_scr[c, :, :hs] = sre_ref[:, c * hs:(c + 1) * hs]
            h_scr[c, :, hs:] = sim_ref[:, c * hs:(c + 1) * hs]
        scan(*bufs[0], h_scr, tt, range(cb))

    for parity in range(2):
        @pl.when((i > 0) & (i < nt) & (i % 2 == parity))
        def _():
            for c in range(cb):
                scan(*bufs[parity], h_scr, tt, (c,))
                out(*bufs[1 - parity], z_ref, tt, (c,))

    @pl.when(i == nt)
    def _():
        out(*bufs[(nt - 1) % 2], z_ref, tt, range(cb))
        for c in range(cb):
            ore_ref[:, c * hs:(c + 1) * hs] = h_scr[c, :, :hs]
            oim_ref[:, c * hs:(c + 1) * hs] = h_scr[c, :, hs:]


def _ssm_scan(u, d_skip, bd_b, a_re, a_im, bd_c, s_re, s_im, tt, cb):
    bsz, L, d = u.shape
    nblk = d // LANES
    hs = SSM_HS
    nt = L // tt
    bw = cb * LANES
    st_spec = pl.BlockSpec((SUBLANES, cb * hs), lambda j, g, i: (g, j))
    in_spec = pl.BlockSpec((SUBLANES, tt, bw),
                           lambda j, g, i: (g, jnp.minimum(i, nt - 1), j))
    out_spec = pl.BlockSpec((SUBLANES, tt, bw),
                            lambda j, g, i: (g, jnp.maximum(i - 1, 0), j))
    return pl.pallas_call(
        functools.partial(_ssm_kernel, tt=tt, nt=nt, cb=cb),
        grid=(nblk // cb, bsz // SUBLANES, nt + 1),
        in_specs=[
            in_spec,
            pl.BlockSpec((1, bw), lambda j, g, i: (0, j)),
            pl.BlockSpec((cb, LANES, 2 * hs), lambda j, g, i: (j, 0, 0)),
            pl.BlockSpec((cb, 1, hs), lambda j, g, i: (j, 0, 0)),
            pl.BlockSpec((cb, 1, hs), lambda j, g, i: (j, 0, 0)),
            pl.BlockSpec((cb, 2 * hs, LANES), lambda j, g, i: (j, 0, 0)),
            st_spec,
            st_spec,
        ],
        out_specs=[out_spec, st_spec, st_spec],
        out_shape=[
            jax.ShapeDtypeStruct((bsz, L, d), BF16),
            jax.ShapeDtypeStruct(s_re.shape, F32),
            jax.ShapeDtypeStruct(s_im.shape, F32),
        ],
        scratch_shapes=[
            pltpu.VMEM((cb, tt * SUBLANES, LANES), F32),
            pltpu.VMEM((cb, tt * SUBLANES, LANES), F32),
            pltpu.VMEM((cb, tt, SUBLANES, 2 * hs), F32),
            pltpu.VMEM((cb, tt, SUBLANES, 2 * hs), F32),
            pltpu.VMEM((cb, SUBLANES, 2 * hs), F32),
        ],
        compiler_params=_params(("parallel", "parallel", "arbitrary")),
        name="ssm_scan",
    )(u, d_skip, bd_b, a_re, a_im, bd_c, s_re, s_im)


def _layer_norm_chunks(acc_scr, lng_ref, lnb_ref, o_ref, nchunk, tn):
    d = nchunk * tn
    parts = [acc_scr[c] for c in range(nchunk)]
    mu = sum(jnp.sum(p, axis=-1, keepdims=True) for p in parts) * (1.0 / d)
    cen = [p - mu for p in parts]
    var = sum(jnp.sum(q * q, axis=-1, keepdims=True) for q in cen) * (1.0 / d)
    inv = lax.rsqrt(var + LN_EPS)
    for c in range(nchunk):
        sl = slice(c * tn, (c + 1) * tn)
        o_ref[:, sl] = cen[c] * inv * lng_ref[:, sl] + lnb_ref[:, sl]


def _glu_ln_kernel(z_ref, wa_ref, wg_ref, x_ref, gt_ref, lng_ref, lnb_ref,
                   o_ref, acc_scr, *, nchunk, tn, alpha):
    c = pl.program_id(2)
    z = z_ref[...]
    a = _dot(z, wa_ref[...])
    g = _dot(z, wg_ref[...])
    m = a * jax.nn.sigmoid(g)
    acc_scr[c] = alpha * x_ref[...] + (1.0 + gt_ref[...]) * m

    @pl.when(c == nchunk - 1)
    def _():
        _layer_norm_chunks(acc_scr, lng_ref, lnb_ref, o_ref, nchunk, tn)


def _glu_ln(z, w_glu, layer, x, gate, ln_g, ln_b, tm, tn, alpha):
    bsz, L, d = x.shape
    nchunk = d // tn
    if gate.shape[1] == 1:
        gt_spec = pl.BlockSpec((None, 1, tn), lambda b, i, c: (b, 0, c))
    else:
        gt_spec = pl.BlockSpec((None, tm, tn), lambda b, i, c: (b, i, c))
    vec = pl.BlockSpec((1, d), lambda b, i, c: (0, 0))
    return pl.pallas_call(
        functools.partial(_glu_ln_kernel, nchunk=nchunk, tn=tn, alpha=alpha),
        grid=(bsz, L // tm, nchunk),
        in_specs=[
            pl.BlockSpec((None, tm, d), lambda b, i, c: (b, i, 0)),
            pl.BlockSpec((None, d, tn), lambda b, i, c: (layer, 0, c)),
            pl.BlockSpec((None, d, tn), lambda b, i, c: (layer, 0, nchunk + c)),
            pl.BlockSpec((None, tm, tn), lambda b, i, c: (b, i, c)),
            gt_spec,
            vec,
            vec,
        ],
        out_specs=pl.BlockSpec((None, tm, d), lambda b, i, c: (b, i, 0)),
        out_shape=jax.ShapeDtypeStruct((bsz, L, d), F32),
        scratch_shapes=[pltpu.VMEM((nchunk, tm, tn), F32)],
        compiler_params=_params(("parallel", "parallel", "arbitrary")),
        name="glu_ln",
    )(z, w_glu, w_glu, x, gate, ln_g, ln_b)


def _oproj_ln_kernel(o_in_ref, w_ref, x_ref, gt_ref, lng_ref, lnb_ref, o_ref, acc_scr,
                     *, nchunk, tn, alpha):
    c = pl.program_id(2)
    m = _dot(o_in_ref[...], w_ref[...])
    acc_scr[c] = alpha * x_ref[...] + (1.0 + gt_ref[...]) * m

    @pl.when(c == nchunk - 1)
    def _():
        _layer_norm_chunks(acc_scr, lng_ref, lnb_ref, o_ref, nchunk, tn)


def _oproj_ln(o_in, w_o, layer, x, gate, ln_g, ln_b, tm, tn, alpha):
    bsz, L, d = x.shape
    nchunk = d // tn
    if gate.shape[1] == 1:
        gt_spec = pl.BlockSpec((None, 1, tn), lambda b, i, c: (b, 0, c))
    else:
        gt_spec = pl.BlockSpec((None, tm, tn), lambda b, i, c: (b, i, c))
    vec = pl.BlockSpec((1, d), lambda b, i, c: (0, 0))
    return pl.pallas_call(
        functools.partial(_oproj_ln_kernel, nchunk=nchunk, tn=tn, alpha=alpha),
        grid=(bsz, L // tm, nchunk),
        in_specs=[
            pl.BlockSpec((None, tm, o_in.shape[2]), lambda b, i, c: (b, i, 0)),
            pl.BlockSpec((None, w_o.shape[1], tn), lambda b, i, c: (layer, 0, c)),
            pl.BlockSpec((None, tm, tn), lambda b, i, c: (b, i, c)),
            gt_spec,
            vec,
            vec,
        ],
        out_specs=pl.BlockSpec((None, tm, d), lambda b, i, c: (b, i, 0)),
        out_shape=jax.ShapeDtypeStruct((bsz, L, d), F32),
        scratch_shapes=[pltpu.VMEM((nchunk, tm, tn), F32)],
        compiler_params=_params(("parallel", "parallel", "arbitrary")),
        name="oproj_ln",
    )(o_in, w_o, x, gate, ln_g, ln_b)


def _ffn_kernel(x_ref, sc_ref, sh_ref, gt_ref, wg_ref, wv_ref, wc_ref, bc_ref, wd_ref,
                st_ref, lng_ref, lnb_ref, o_ref, cv_ref, h_scr, acc_scr, prev_scr,
                *, nb, T, nchunk, alpha):
    i = pl.program_id(1)
    c = pl.program_id(2)
    M = nb * T
    fc = wg_ref.shape[1]

    @pl.when(c == 0)
    def _():
        h_scr[...] = _modulate(x_ref, sc_ref, sh_ref)
        acc_scr[...] = jnp.zeros(acc_scr.shape, F32)

    @pl.when(i == 0)
    def _():
        prev_scr[...] = st_ref[c]

    @pl.when(i > 0)
    def _():
        prev_scr[...] = cv_ref[c]

    h = h_scr[...]
    g = _dot(h, wg_ref[...])
    v = _dot(h, wv_ref[...])

    row = lax.broadcasted_iota(jnp.int32, (M, 1), 0)
    if nb == 1:
        tpos = row
        p0 = prev_scr[0, 0:1, :]
        p1 = prev_scr[0, 1:2, :]
    else:
        tpos = lax.rem(row, T)
        p0 = jnp.broadcast_to(prev_scr[:, 0:1, :], (nb, T, fc)).reshape(M, fc)
        p1 = jnp.broadcast_to(prev_scr[:, 1:2, :], (nb, T, fc)).reshape(M, fc)
    s1 = jnp.where(tpos == 0, p1, pltpu.roll(g, 1, 0))
    s2 = jnp.where(tpos == 0, p0, jnp.where(tpos == 1, p1, pltpu.roll(g, 2, 0)))
    conv = bc_ref[...] + s2 * wc_ref[0:1, :] + s1 * wc_ref[1:2, :] + g * wc_ref[2:3, :]
    act = (_gelu(conv) * v).astype(BF16)
    acc_scr[...] += _dot(act, wd_ref[...])
    cv_ref[c] = g.reshape(nb, T, fc)[:, T - 2:, :]

    @pl.when(c == nchunk - 1)
    def _():
        r = alpha * x_ref[...] + (1.0 + gt_ref[...]) * acc_scr[...]
        mu = jnp.mean(r, axis=-1, keepdims=True)
        cen = r - mu
        var = jnp.mean(cen * cen, axis=-1, keepdims=True)
        o_ref[...] = cen * lax.rsqrt(var + LN_EPS) * lng_ref[...] + lnb_ref[...]


def _ffn(x, sc, sh, gate, w_up, w_conv, b_conv, w_down, layer, conv_state, ln_g, ln_b,
         nb, T, fc, alpha):
    S, R, d = x.shape
    dff = w_down.shape[1]
    nchunk = dff // fc
    tm = nb * T
    ntile = R // tm
    assert nb == 1 or ntile == 1

    def mod_spec(mod):
        if mod.shape[1] == 1:
            return pl.BlockSpec((None, 1, d), lambda s, i, c: (s, 0, 0))
        return pl.BlockSpec((None, tm, d), lambda s, i, c: (s, i, 0))

    vec = pl.BlockSpec((1, d), lambda s, i, c: (0, 0))
    st_spec = pl.BlockSpec((nchunk, nb, 2, fc), lambda s, i, c: (0, s, 0, 0))
    return pl.pallas_call(
        functools.partial(_ffn_kernel, nb=nb, T=T, nchunk=nchunk, alpha=alpha),
        grid=(S, ntile, nchunk),
        in_specs=[
            pl.BlockSpec((None, tm, d), lambda s, i, c: (s, i, 0)),
            mod_spec(sc),
            mod_spec(sh),
            mod_spec(gate),
            pl.BlockSpec((None, d, fc), lambda s, i, c: (layer, 0, c)),
            pl.BlockSpec((None, d, fc), lambda s, i, c: (layer, 0, nchunk + c)),
            pl.BlockSpec((w_conv.shape[0], fc), lambda s, i, c: (0, c)),
            pl.BlockSpec((1, fc), lambda s, i, c: (0, c)),
            pl.BlockSpec((None, fc, d), lambda s, i, c: (layer, c, 0)),
            st_spec,
            vec,
            vec,
        ],
        out_specs=[
            pl.BlockSpec((None, tm, d), lambda s, i, c: (s, i, 0)),
            st_spec,
        ],
        out_shape=[
            jax.ShapeDtypeStruct((S, R, d), F32),
            jax.ShapeDtypeStruct(conv_state.shape, F32),
        ],
        scratch_shapes=[
            pltpu.VMEM((tm, d), BF16),
            pltpu.VMEM((tm, d), F32),
            pltpu.VMEM((nb, 2, fc), F32),
        ],
        compiler_params=_params(("parallel", "arbitrary", "arbitrary")),
        name="conv_ffn",
    )(x, sc, sh, gate, w_up, w_up, w_conv, b_conv, w_down, conv_state, ln_g, ln_b)


def _qkv(x, sc, sh, w_qkv, layer, tm):
    da = w_qkv.shape[2] // 3
    (q,) = _mod_matmul(x, sc, sh, w_qkv, layer, 0, da, (BF16,), tm, da, "q_proj", Q_SCALE)
    k32, k16 = _mod_matmul(x, sc, sh, w_qkv, layer, da, da, (F32, BF16), tm, da, "k_proj",
                           f32_heads=(da // HEAD_DIM, HEAD_DIM))
    v32, v16 = _mod_matmul(x, sc, sh, w_qkv, layer, 2 * da, da, (F32, BF16), tm, da, "v_proj",
                           f32_heads=(da // (2 * HEAD_DIM), 2 * HEAD_DIM))
    return q, k32, v32, k16, v16


def _chunk_id(pos):
    assert CHUNK & (CHUNK - 1) == 0
    return lax.shift_right_logical(pos, CHUNK.bit_length() - 1)


def _sub_ln(o, g_ref, lam_init):
    o = o * lax.rsqrt(jnp.mean(o * o, axis=-1, keepdims=True) + LN_EPS)
    return o * g_ref[...] * (1.0 - lam_init)


def _lane_tile(x, width):
    if width < LANES:
        return x[:, :width]
    return jnp.tile(x, (1, width // LANES))


def _online_update(s, v, m_scr, l_scr, acc_scr, idx):
    m_old = m_scr[idx]
    m_new = jnp.maximum(m_old, jnp.max(s, axis=-1, keepdims=True))
    corr = jnp.exp2(m_old - m_new)
    p = jnp.exp2(s - _lane_tile(m_new, s.shape[1]))
    l_scr[idx] = corr * l_scr[idx] + jnp.sum(p, axis=-1, keepdims=True)
    acc_scr[idx] = _lane_tile(corr, v.shape[1]) * acc_scr[idx] + _dot(p.astype(BF16), v)
    m_scr[idx] = m_new


def _attn_prompt_kernel(lam_ref, q_ref, k_ref, v_ref, g_ref, o_ref, m_scr, l_scr, acc_scr,
                        *, tq, tk, sb, lam_init):
    i = pl.program_id(2)
    maps = (slice(0, HEAD_DIM), slice(HEAD_DIM, 2 * HEAD_DIM))
    base = pl.multiple_of(i * tq, tq)

    row_chunk = _chunk_id(lax.broadcasted_iota(jnp.int32, (sb, sb), 0))
    col_chunk = _chunk_id(lax.broadcasted_iota(jnp.int32, (sb, sb), 1))
    diag_keep = col_chunk <= row_chunk
    for r in range(tq // sb):
        rows = slice(r * sb, (r + 1) * sb)
        nk = (r + 1) * sb
        v = v_ref[pl.ds(base, nk), :]
        for idx, cols in enumerate(maps):
            s = _dot_nt(q_ref[rows, cols], k_ref[pl.ds(base, nk), cols])
            s_diag = jnp.where(diag_keep, s[:, r * sb:], MASK_VALUE)
            s = s_diag if r == 0 else jnp.concatenate([s[:, :r * sb], s_diag], axis=1)
            m = jnp.max(s, axis=-1, keepdims=True)
            p = jnp.exp2(s - m)
            m_scr[idx, rows] = jnp.broadcast_to(m, (sb, LANES))
            l_scr[idx, rows] = jnp.broadcast_to(jnp.sum(p, axis=-1, keepdims=True), (sb, LANES))
            acc_scr[idx, rows] = _dot(p.astype(BF16), v)

    def full_tiles(j0, count):
        starts = [pl.multiple_of((j0 + t) * tk, tk) for t in range(count)]
        s = [[_dot_nt(q_ref[:, cols], k_ref[pl.ds(st, tk), cols]) for cols in maps]
             for st in starts]
        for t, st in enumerate(starts):
            v = v_ref[pl.ds(st, tk), :]
            for idx in range(len(maps)):
                _online_update(s[t][idx], v, m_scr, l_scr, acc_scr, idx)

    def pair(jj, carry):
        full_tiles(2 * jj, 2)
        return carry

    nfull = i * (tq // tk)
    lax.fori_loop(0, nfull // 2, pair, 0)

    @pl.when(nfull % 2 == 1)
    def _():
        full_tiles(nfull - 1, 1)

    lam = lam_ref[0, 0]
    hw = acc_scr.shape[2]
    o = (acc_scr[0] * _lane_tile(1.0 / l_scr[0], hw)
         - acc_scr[1] * _lane_tile(lam / l_scr[1], hw))
    o_ref[...] = _sub_ln(o, g_ref, lam_init).astype(o_ref.dtype)


def _attn_prompt(lam, q, k, v, subln_g, tq, tk, sb, lam_init):
    bsz, L, da = q.shape
    hw = 2 * HEAD_DIM
    nh = da // hw
    assert tq % tk == 0 and tq % sb == 0 and sb % CHUNK == 0
    return pl.pallas_call(
        functools.partial(_attn_prompt_kernel, tq=tq, tk=tk, sb=sb, lam_init=lam_init),
        grid=(bsz, nh, L // tq),
        in_specs=[
            pl.BlockSpec(memory_space=pltpu.SMEM),
            pl.BlockSpec((None, tq, hw), lambda b, h, i: (b, i, h)),
            pl.BlockSpec((None, L, hw), lambda b, h, i: (b, 0, h)),
            pl.BlockSpec((None, L, hw), lambda b, h, i: (b, 0, h)),
            pl.BlockSpec((1, hw), lambda b, h, i: (0, 0)),
        ],
        out_specs=pl.BlockSpec((None, tq, hw), lambda b, h, i: (b, i, h)),
        out_shape=jax.ShapeDtypeStruct((bsz, L, da), BF16),
        scratch_shapes=[
            pltpu.VMEM((2, tq, LANES), F32),
            pltpu.VMEM((2, tq, LANES), F32),
            pltpu.VMEM((2, tq, hw), F32),
        ],
        compiler_params=_params(("parallel", "parallel", "arbitrary")),
        name="attn_prompt",
    )(lam, q, k, v, subln_g)


def _attn_sample_kernel(lam_ref, q_ref, ck_ref, cv_ref, kn_ref, vn_ref, g_ref, o_ref,
                        m_scr, l_scr, acc_scr, vh_scr, *, P, T, pc, nsub, lam_init):
    j = pl.program_id(1)
    hw = 2 * HEAD_DIM
    qc = _chunk_id(P + lax.broadcasted_iota(jnp.int32, (T, 1), 0))

    @pl.when(j == 0)
    def _():
        m_scr[...] = jnp.full(m_scr.shape, MASK_VALUE, F32)
        l_scr[...] = jnp.zeros(l_scr.shape, F32)
        acc_scr[...] = jnp.zeros(acc_scr.shape, F32)

    keep_c = _chunk_id(j * pc + lax.broadcasted_iota(jnp.int32, (1, pc), 1)) <= qc
    nh = nsub // 2
    for half in range(2):
        vh_scr[half] = cv_ref[:, half * LANES:(half + 1) * LANES]
    scores = []
    for s in range(nsub):
        k = ck_ref[pl.ds(s, pc, stride=nsub), :].astype(BF16)
        q = q_ref[:, s * HEAD_DIM:(s + 1) * HEAD_DIM]
        scores.append(jnp.where(keep_c, _dot_nt(q, k), MASK_VALUE))
    for h in range(nh):
        v = jnp.concatenate([vh_scr[half, pl.ds(h, pc, stride=nh), :] for half in range(2)],
                            axis=1).astype(BF16)
        for s in (2 * h, 2 * h + 1):
            _online_update(scores[s], v, m_scr, l_scr, acc_scr, s)

    @pl.when(j == pl.num_programs(1) - 1)
    def _():
        keep_n = _chunk_id(P + lax.broadcasted_iota(jnp.int32, (1, T), 1)) <= qc
        lam = lam_ref[0, 0]
        new_scores = []
        for s in range(nsub):
            cols = slice(s * HEAD_DIM, (s + 1) * HEAD_DIM)
            new_scores.append(
                jnp.where(keep_n, _dot_nt(q_ref[:, cols], kn_ref[:, cols]), MASK_VALUE))
        for h in range(nh):
            vcols = slice(h * hw, (h + 1) * hw)
            vn = vn_ref[:, vcols]
            for s in (2 * h, 2 * h + 1):
                _online_update(new_scores[s], vn, m_scr, l_scr, acc_scr, s)
        for h in range(nh):
            vcols = slice(h * hw, (h + 1) * hw)
            o = (acc_scr[2 * h] * _lane_tile(1.0 / l_scr[2 * h], hw)
                 - acc_scr[2 * h + 1] * _lane_tile(lam / l_scr[2 * h + 1], hw))
            o_ref[:, vcols] = _sub_ln(o, g_ref, lam_init).astype(o_ref.dtype)


def _attn_sample(lam, q, cache_k, cache_v, layer, k_new, v_new, subln_g, pc, lam_init):
    bsz, T, da = q.shape
    na, _, P, nsub, _ = cache_k.shape
    hw = 2 * HEAD_DIM
    new_spec = pl.BlockSpec((None, T, da), lambda b, j: (b, 0, 0))
    return pl.pallas_call(
        functools.partial(_attn_sample_kernel, P=P, T=T, pc=pc, nsub=nsub, lam_init=lam_init),
        grid=(bsz, P // pc),
        in_specs=[
            pl.BlockSpec(memory_space=pltpu.SMEM),
            new_spec,
            pl.BlockSpec((None, None, pc * nsub, HEAD_DIM), lambda b, j: (layer, b, j, 0)),
            pl.BlockSpec((None, None, pc * (nsub // 2), hw), lambda b, j: (layer, b, j, 0)),
            new_spec,
            new_spec,
            pl.BlockSpec((1, hw), lambda b, j: (0, 0)),
        ],
        out_specs=new_spec,
        out_shape=jax.ShapeDtypeStruct((bsz, T, da), BF16),
        scratch_shapes=[
            pltpu.VMEM((nsub, T, LANES), F32),
            pltpu.VMEM((nsub, T, LANES), F32),
            pltpu.VMEM((nsub, T, hw), F32),
            pltpu.VMEM((2, pc * (nsub // 2), LANES), F32),
        ],
        compiler_params=_params(("parallel", "arbitrary")),
        name="attn_sample",
    )(lam, q, cache_k.reshape(na, bsz, P * nsub, HEAD_DIM),
      cache_v.reshape(na, bsz, P * (nsub // 2), hw), k_new, v_new, subln_g)


def _ssm_discretise(lam_re, lam_im, log_step, b_re, b_im):
    lr = jnp.minimum(lam_re, -1e-4)
    li = lam_im
    dt = jnp.exp(log_step)[:, None]
    mag = jnp.exp(lr * dt)
    abar_re = mag * jnp.cos(li * dt)
    abar_im = mag * jnp.sin(li * dt)
    nr = abar_re - 1.0
    ni = abar_im
    den = lr * lr + li * li
    kr = (nr * lr + ni * li) / den
    ki = (ni * lr - nr * li) / den
    bbar_re = kr[..., None] * b_re - ki[..., None] * b_im
    bbar_im = kr[..., None] * b_im + ki[..., None] * b_re
    return abar_re, abar_im, bbar_re, bbar_im


def _block_diag(m):
    nblk, gb, r, c = m.shape
    eye = jnp.eye(gb, dtype=m.dtype)
    return jnp.einsum("jgrc,gh->jgrhc", m, eye).reshape(nblk, gb * r, gb * c)


def _ssm_matrices(lam_re, lam_im, log_step, b_re, b_im, c_re, c_im):
    G = lam_re.shape[0]
    gb = SSM_GROUPS_PER_BLOCK
    nblk = G // gb
    abar_re, abar_im, bbar_re, bbar_im = _ssm_discretise(lam_re, lam_im, log_step, b_re, b_im)
    bt_re = bbar_re.reshape(nblk, gb, STATE_P, GROUP_CH).swapaxes(2, 3)
    bt_im = bbar_im.reshape(nblk, gb, STATE_P, GROUP_CH).swapaxes(2, 3)
    bd_b = jnp.concatenate([_block_diag(bt_re), _block_diag(bt_im)], axis=2).astype(BF16)
    ct_re = c_re.reshape(nblk, gb, GROUP_CH, STATE_P).swapaxes(2, 3)
    ct_im = c_im.reshape(nblk, gb, GROUP_CH, STATE_P).swapaxes(2, 3)
    bd_c = jnp.concatenate([_block_diag(ct_re), _block_diag(-ct_im)], axis=1).astype(BF16)
    a_re = abar_re.reshape(nblk, 1, gb * STATE_P)
    a_im = abar_im.reshape(nblk, 1, gb * STATE_P)
    return bd_b, a_re, a_im, bd_c


def _chunk_state(state, fc):
    nbt, w, dff = state.shape
    return state.reshape(nbt, w, dff // fc, fc).transpose(2, 0, 1, 3)


def _unchunk_state(state):
    nchunk, nbt, w, fc = state.shape
    return state.transpose(1, 2, 0, 3).reshape(nbt, w, nchunk * fc)


TM_PROMPT = 512
TN_GLU = 1024
FFN_CHUNK = 512
SSM_TT = 128
SSM_CB = 4
SAMPLE_PC = 512
TQ = 1024
TK = 512
TQ_DIAG = 256


def kernel(x_prompt, x_sample, c_prompt, c_sample, cache_k, cache_v, state_ssm_re, state_ssm_im, state_conv, w_ada, b_ada, ln_g, ln_b, w_up, w_dconv, b_dconv, w_down, w_ssm_in, ssm_lam_re, ssm_lam_im, ssm_log_step, ssm_b_re, ssm_b_im, ssm_c_re, ssm_c_im, ssm_d, w_glu, w_qkv, lam_q1, lam_k1, lam_q2, lam_k2, subln_g, w_o):
    depth = w_ada.shape[0]
    bp, L, d = x_prompt.shape
    bs, T, _ = x_sample.shape
    dff = w_down.shape[1]
    alpha = (2 * depth) ** 0.25
    rows_s = bs * T

    mods = _ada(jnp.concatenate([c_prompt, c_sample], axis=0), w_ada, b_ada)
    w_up16, w_down16 = w_up.astype(BF16), w_down.astype(BF16)
    w_in16, w_glu16 = w_ssm_in.astype(BF16), w_glu.astype(BF16)
    w_qkv16, w_o16 = w_qkv.astype(BF16), w_o.astype(BF16)

    xp = x_prompt
    xs = x_sample.reshape(1, rows_s, d)
    zeros_conv = jnp.zeros((bp, state_conv.shape[2], dff), F32)
    outs = {name: [] for name in ("kp", "vp", "srp", "sip", "cvp", "ks", "vs", "srs", "sis", "cvs")}

    for i in range(depth):
        mp = [m.reshape(bp, 1, d) for m in jnp.split(mods[i, :bp], 6, axis=-1)]
        ms = [jnp.repeat(m, T, axis=0).reshape(1, rows_s, d)
              for m in jnp.split(mods[i, bp:], 6, axis=-1)]
        shp1, scp1, gtp1, shp2, scp2, gtp2 = mp
        shs1, scs1, gts1, shs2, scs2, gts2 = ms
        lng = ln_g[i].reshape(2, 1, d)
        lnb = ln_b[i].reshape(2, 1, d)
        j = i // 2
        if i % 2 == 0:
            bd_b, a_re, a_im, bd_c = _ssm_matrices(
                ssm_lam_re[j], ssm_lam_im[j], ssm_log_step[j], ssm_b_re[j], ssm_b_im[j],
                ssm_c_re[j], ssm_c_im[j])
            d_skip = ssm_d[j].reshape(1, d)
            gp = ssm_lam_re.shape[1] * ssm_lam_re.shape[2]
            (up,) = _mod_matmul(xp, scp1, shp1, w_in16, j, 0, d, (F32,), TM_PROMPT, d, "ssm_in")
            zero_state = jnp.zeros((bp, gp), F32)
            zp, srp, sip = _ssm_scan(up, d_skip, bd_b, a_re, a_im, bd_c, zero_state, zero_state,
                                     SSM_TT, SSM_CB)
            xp = _glu_ln(zp, w_glu16, j, xp, gtp1, lng[0], lnb[0], TM_PROMPT, TN_GLU, alpha)
            (us,) = _mod_matmul(xs, scs1, shs1, w_in16, j, 0, d, (F32,), rows_s, d, "ssm_in")
            zs, srs, sis = _ssm_scan(us.reshape(bs, T, d), d_skip, bd_b, a_re, a_im, bd_c,
                                     state_ssm_re[j].reshape(bs, gp),
                                     state_ssm_im[j].reshape(bs, gp), T, SSM_CB)
            xs = _glu_ln(zs.reshape(1, rows_s, d), w_glu16, j, xs, gts1, lng[0], lnb[0],
                         rows_s, TN_GLU, alpha)
            st_shape = ssm_lam_re.shape[1:]
            outs["srp"].append(srp.reshape((bp,) + st_shape))
            outs["sip"].append(sip.reshape((bp,) + st_shape))
            outs["srs"].append(srs.reshape((bs,) + st_shape))
            outs["sis"].append(sis.reshape((bs,) + st_shape))
        else:
            lam_init = 0.8 - 0.6 * math.exp(-0.3 * i)
            lam = (jnp.exp(jnp.sum(lam_q1[j] * lam_k1[j])) - jnp.exp(jnp.sum(lam_q2[j] * lam_k2[j]))
                   + lam_init).reshape(1, 1)
            sg = subln_g[j].reshape(1, 2 * HEAD_DIM)
            nsub = cache_k.shape[3]
            nhead = cache_v.shape[3]
            qp, kp, vp, kp16, vp16 = _qkv(xp, scp1, shp1, w_qkv16, j, TM_PROMPT)
            op = _attn_prompt(lam, qp, kp16, vp16, sg, TQ, TK, TQ_DIAG, lam_init)
            xp = _oproj_ln(op, w_o16, j, xp, gtp1, lng[0], lnb[0], TM_PROMPT, d, alpha)
            qs, ksn, vsn, ks16, vs16 = _qkv(xs, scs1, shs1, w_qkv16, j, rows_s)
            da = w_o.shape[1]
            osm = _attn_sample(lam, qs.reshape(bs, T, da), cache_k, cache_v, j,
                               ks16.reshape(bs, T, da), vs16.reshape(bs, T, da), sg,
                               SAMPLE_PC, lam_init)
            xs = _oproj_ln(osm.reshape(1, rows_s, da), w_o16, j, xs, gts1, lng[0], lnb[0],
                           rows_s, d, alpha)
            outs["kp"].append(kp.reshape(bp, L, nsub, HEAD_DIM))
            outs["vp"].append(vp.reshape(bp, L, nhead, 2 * HEAD_DIM))
            outs["ks"].append(ksn.reshape(bs, T, nsub, HEAD_DIM))
            outs["vs"].append(vsn.reshape(bs, T, nhead, 2 * HEAD_DIM))

        bconv = b_dconv[i].reshape(1, dff)
        xp, cvp = _ffn(xp, scp2, shp2, gtp2, w_up16, w_dconv[i], bconv, w_down16, i,
                       _chunk_state(zeros_conv, FFN_CHUNK), lng[1], lnb[1],
                       1, TM_PROMPT, FFN_CHUNK, alpha)
        xs, cvs = _ffn(xs, scs2, shs2, gts2, w_up16, w_dconv[i], bconv, w_down16, i,
                       _chunk_state(state_conv[i], FFN_CHUNK), lng[1], lnb[1],
                       bs, T, FFN_CHUNK, alpha)
        outs["cvp"].append(_unchunk_state(cvp))
        outs["cvs"].append(_unchunk_state(cvs))

    return (xp, xs.reshape(bs, T, d),
            jnp.stack(outs["kp"]), jnp.stack(outs["vp"]),
            jnp.stack(outs["srp"]), jnp.stack(outs["sip"]), jnp.stack(outs["cvp"]),
            jnp.stack(outs["ks"]), jnp.stack(outs["vs"]),
            jnp.stack(outs["srs"]), jnp.stack(outs["sis"]), jnp.stack(outs["cvs"]))
```

```python
import functools
import math

import jax
import jax.numpy as jnp
from jax import lax
from jax.experimental import pallas as pl
from jax.experimental.pallas import tpu as pltpu

F32 = jnp.float32
BF16 = jnp.bfloat16

CHUNK = 64
HEAD_DIM = 128
GROUP_CH = 16
STATE_P = 64
LN_EPS = 1e-5
MASK_VALUE = -1e30
Q_SCALE = math.log2(math.e) * HEAD_DIM ** -0.5

SUBLANES = 8
LANES = 128
VMEM_LIMIT = 56 * 1024 * 1024

SSM_GROUPS_PER_BLOCK = LANES // GROUP_CH
SSM_HS = SSM_GROUPS_PER_BLOCK * STATE_P


def _params(sem):
    return pltpu.CompilerParams(dimension_semantics=sem, vmem_limit_bytes=VMEM_LIMIT)


def _gelu(x):
    return 0.5 * x * (1.0 + lax.erf(x * math.sqrt(0.5)))


def _dot(a, b):
    return jnp.dot(a, b, preferred_element_type=F32)


def _dot_nt(a, b):
    return lax.dot_general(a, b, (((1,), (1,)), ((), ())), preferred_element_type=F32)


def _ada_kernel(c_ref, w_ref, b_ref, o_ref):
    c = c_ref[...]
    a = (c * jax.nn.sigmoid(c)).astype(BF16)
    o_ref[...] = _dot(a, w_ref[...].astype(BF16)) + b_ref[...]


def _ada(c_all, w_ada, b_ada, tn=1024):
    depth, d, n = w_ada.shape
    rows = c_all.shape[0]
    return pl.pallas_call(
        _ada_kernel,
        grid=(depth, n // tn),
        in_specs=[
            pl.BlockSpec((rows, d), lambda l, j: (0, 0)),
            pl.BlockSpec((None, d, tn), lambda l, j: (l, 0, j)),
            pl.BlockSpec((None, 1, tn), lambda l, j: (l, 0, j)),
        ],
        out_specs=pl.BlockSpec((None, rows, tn), lambda l, j: (l, 0, j)),
        out_shape=jax.ShapeDtypeStruct((depth, rows, n), F32),
        compiler_params=_params(("parallel", "parallel")),
        name="ada",
    )(c_all, w_ada, b_ada.reshape(depth, 1, n))


def _modulate(x_ref, sc_ref, sh_ref):
    return (x_ref[...] * (1.0 + sc_ref[...]) + sh_ref[...]).astype(BF16)


def _store_heads(o_ref, r, heads):
    nh, hd = heads
    rows = r.shape[0]
    for h in range(nh):
        part = r[:, h * hd:(h + 1) * hd]
        if hd == LANES:
            o_ref[pl.ds(h, rows, stride=nh), :] = part
        else:
            o_ref[:, h, :] = part


def _modmm_kernel(x_ref, sc_ref, sh_ref, w_ref, *rest, bf16_scale, f32_heads):
    *out_refs, h_scr = rest

    @pl.when(pl.program_id(2) == 0)
    def _():
        h_scr[...] = _modulate(x_ref, sc_ref, sh_ref)

    r = _dot(h_scr[...], w_ref[...])
    for o_ref in out_refs:
        if o_ref.dtype != F32:
            o_ref[...] = (r if bf16_scale == 1.0 else r * bf16_scale).astype(o_ref.dtype)
        elif f32_heads is None:
            o_ref[...] = r
        else:
            _store_heads(o_ref, r, f32_heads)


def _mod_spec(mod, tm):
    if mod.shape[1] == 1:
        return pl.BlockSpec((None, 1, mod.shape[2]), lambda b, i, n: (b, 0, 0))
    return pl.BlockSpec((None, tm, mod.shape[2]), lambda b, i, n: (b, i, 0))


def _mod_matmul(x, sc, sh, w, layer, col0, n, out_dtypes, tm, tn, name, bf16_scale=1.0,
                f32_heads=None):
    bsz, L, d = x.shape
    assert col0 % tn == 0 and n % tn == 0
    jb = col0 // tn
    nt = L // tm
    flat_spec = pl.BlockSpec((None, tm, tn), lambda b, i, j: (b, i, j))
    out_specs, out_shape = [], []
    for dt in out_dtypes:
        if dt != F32 or f32_heads is None:
            out_specs.append(flat_spec)
            out_shape.append(jax.ShapeDtypeStruct((bsz, L, n), dt))
            continue
        nh, hd = f32_heads
        assert tn == n == nh * hd
        if hd == LANES:
            out_specs.append(pl.BlockSpec((tm * nh, hd), lambda b, i, j: (b * nt + i, 0)))
            out_shape.append(jax.ShapeDtypeStruct((bsz * L * nh, hd), dt))
        else:
            out_specs.append(pl.BlockSpec((None, tm, nh, hd), lambda b, i, j: (b, i, 0, 0)))
            out_shape.append(jax.ShapeDtypeStruct((bsz, L, nh, hd), dt))
    return pl.pallas_call(
        functools.partial(_modmm_kernel, bf16_scale=bf16_scale, f32_heads=f32_heads),
        grid=(bsz, nt, n // tn),
        in_specs=[
            pl.BlockSpec((None, tm, d), lambda b, i, j: (b, i, 0)),
            _mod_spec(sc, tm),
            _mod_spec(sh, tm),
            pl.BlockSpec((None, d, tn), lambda b, i, j: (layer, 0, jb + j)),
        ],
        out_specs=out_specs,
        out_shape=out_shape,
        scratch_shapes=[pltpu.VMEM((tm, d), BF16)],
        compiler_params=_params(("parallel", "parallel", "arbitrary")),
        name=name,
    )(x, sc, sh, w)


def _ssm_scan_tile(u_ref, bdb_ref, are_ref, aim_ref, tb, x, h_scr, tt, blocks):
    hs = SSM_HS
    for c in blocks:
        lanes = slice(c * LANES, (c + 1) * LANES)
        for b in range(SUBLANES):
            tb[c, pl.ds(b, tt, stride=SUBLANES), :] = u_ref[b, :, lanes]
        x[c] = _dot(tb[c].astype(BF16), bdb_ref[c]).reshape(tt, SUBLANES, 2 * hs)
    for c in blocks:
        ar = jnp.broadcast_to(are_ref[c], (SUBLANES, hs))
        ai = jnp.broadcast_to(aim_ref[c], (SUBLANES, hs))
        hr = h_scr[c, :, :hs]
        hi = h_scr[c, :, hs:]
        for t in range(tt):
            hr, hi = ((ar * hr - ai * hi) + x[c, t, :, :hs],
                      (ar * hi + ai * hr) + x[c, t, :, hs:])
            x[c, t, :, :hs] = hr
            x[c, t, :, hs:] = hi
        h_scr[c, :, :hs] = hr
        h_scr[c, :, hs:] = hi


def _ssm_out_tile(d_ref, bdc_ref, tb, x, z_ref, tt, blocks):
    for c in blocks:
        lanes = slice(c * LANES, (c + 1) * LANES)
        hb = x[c].reshape(tt * SUBLANES, 2 * SSM_HS).astype(BF16)
        y = _dot(hb, bdc_ref[c])
        tb[c] = _gelu(y + d_ref[:, lanes] * tb[c])
        for b in range(SUBLANES):
            z_ref[b, :, lanes] = tb[c, pl.ds(b, tt, stride=SUBLANES), :].astype(z_ref.dtype)


def _ssm_kernel(u_ref, d_ref, bdb_ref, are_ref, aim_ref, bdc_ref, sre_ref, sim_ref,
                z_ref, ore_ref, oim_ref, tb0, tb1, x0, x1, h_scr, *, tt, nt, cb):
    i = pl.program_id(2)
    hs = SSM_HS
    bufs = ((tb0, x0), (tb1, x1))
    scan = functools.partial(_ssm_scan_tile, u_ref, bdb_ref, are_ref, aim_ref)
    out = functools.partial(_ssm_out_tile, d_ref, bdc_ref)

    @pl.when(i == 0)
    def _():
        for c in range(cb):
            h_scr[c, :, :hs] = sre_ref[:, c * hs:(c + 1) * hs]
            h_scr[c, :, hs:] = sim_ref[:, c * hs:(c + 1) * hs]
        scan(*bufs[0], h_scr, tt, range(cb))

    for parity in range(2):
        @pl.when((i > 0) & (i < nt) & (i % 2 == parity))
        def _():
            for c in range(cb):
                scan(*bufs[parity], h_scr, tt, (c,))
                out(*bufs[1 - parity], z_ref, tt, (c,))

    @pl.when(i == nt)
    def _():
        out(*bufs[(nt - 1) % 2], z_ref, tt, range(cb))
        for c in range(cb):
            ore_ref[:, c * hs:(c + 1) * hs] = h_scr[c, :, :hs]
            oim_ref[:, c * hs:(c + 1) * hs] = h_scr[c, :, hs:]


def _ssm_scan(u, d_skip, bd_b, a_re, a_im, bd_c, s_re, s_im, tt, cb):
    bsz, L, d = u.shape
    nblk = d // LANES
    hs = SSM_HS
    nt = L // tt
    bw = cb * LANES
    st_spec = pl.BlockSpec((SUBLANES, cb * hs), lambda j, g, i: (g, j))
    in_spec = pl.BlockSpec((SUBLANES, tt, bw),
                           lambda j, g, i: (g, jnp.minimum(i, nt - 1), j))
    out_spec = pl.BlockSpec((SUBLANES, tt, bw),
                            lambda j, g, i: (g, jnp.maximum(i - 1, 0), j))
    return pl.pallas_call(
        functools.partial(_ssm_kernel, tt=tt, nt=nt, cb=cb),
        grid=(nblk // cb, bsz // SUBLANES, nt + 1),
        in_specs=[
            in_spec,
            pl.BlockSpec((1, bw), lambda j, g, i: (0, j)),
            pl.BlockSpec((cb, LANES, 2 * hs), lambda j, g, i: (j, 0, 0)),
            pl.BlockSpec((cb, 1, hs), lambda j, g, i: (j, 0, 0)),
            pl.BlockSpec((cb, 1, hs), lambda j, g, i: (j, 0, 0)),
            pl.BlockSpec((cb, 2 * hs, LANES), lambda j, g, i: (j, 0, 0)),
            st_spec,
            st_spec,
        ],
        out_specs=[out_spec, st_spec, st_spec],
        out_shape=[
            jax.ShapeDtypeStruct((bsz, L, d), BF16),
            jax.ShapeDtypeStruct(s_re.shape, F32),
            jax.ShapeDtypeStruct(s_im.shape, F32),
        ],
        scratch_shapes=[
            pltpu.VMEM((cb, tt * SUBLANES, LANES), F32),
            pltpu.VMEM((cb, tt * SUBLANES, LANES), F32),
            pltpu.VMEM((cb, tt, SUBLANES, 2 * hs), F32),
            pltpu.VMEM((cb, tt, SUBLANES, 2 * hs), F32),
            pltpu.VMEM((cb, SUBLANES, 2 * hs), F32),
        ],
        compiler_params=_params(("parallel", "parallel", "arbitrary")),
        name="ssm_scan",
    )(u, d_skip, bd_b, a_re, a_im, bd_c, s_re, s_im)


def _layer_norm_chunks(acc_scr, lng_ref, lnb_ref, o_ref, nchunk, tn):
    d = nchunk * tn
    parts = [acc_scr[c] for c in range(nchunk)]
    mu = sum(jnp.sum(p, axis=-1, keepdims=True) for p in parts) * (1.0 / d)
    cen = [p - mu for p in parts]
    var = sum(jnp.sum(q * q, axis=-1, keepdims=True) for q in cen) * (1.0 / d)
    inv = lax.rsqrt(var + LN_EPS)
    for c in range(nchunk):
        sl = slice(c * tn, (c + 1) * tn)
        o_ref[:, sl] = cen[c] * inv * lng_ref[:, sl] + lnb_ref[:, sl]


def _glu_ln_kernel(z_ref, wa_ref, wg_ref, x_ref, gt_ref, lng_ref, lnb_ref,
                   o_ref, acc_scr, *, nchunk, tn, alpha):
    c = pl.program_id(2)
    z = z_ref[...]
    a = _dot(z, wa_ref[...])
    g = _dot(z, wg_ref[...])
    m = a * jax.nn.sigmoid(g)
    acc_scr[c] = alpha * x_ref[...] + (1.0 + gt_ref[...]) * m

    @pl.when(c == nchunk - 1)
    def _():
        _layer_norm_chunks(acc_scr, lng_ref, lnb_ref, o_ref, nchunk, tn)


def _glu_ln(z, w_glu, layer, x, gate, ln_g, ln_b, tm, tn, alpha):
    bsz, L, d = x.shape
    nchunk = d // tn
    if gate.shape[1] == 1:
        gt_spec = pl.BlockSpec((None, 1, tn), lambda b, i, c: (b, 0, c))
    else:
        gt_spec = pl.BlockSpec((None, tm, tn), lambda b, i, c: (b, i, c))
    vec = pl.BlockSpec((1, d), lambda b, i, c: (0, 0))
    return pl.pallas_call(
        functools.partial(_glu_ln_kernel, nchunk=nchunk, tn=tn, alpha=alpha),
        grid=(bsz, L // tm, nchunk),
        in_specs=[
            pl.BlockSpec((None, tm, d), lambda b, i, c: (b, i, 0)),
            pl.BlockSpec((None, d, tn), lambda b, i, c: (layer, 0, c)),
            pl.BlockSpec((None, d, tn), lambda b, i, c: (layer, 0, nchunk + c)),
            pl.BlockSpec((None, tm, tn), lambda b, i, c: (b, i, c)),
            gt_spec,
            vec,
            vec,
        ],
        out_specs=pl.BlockSpec((None, tm, d), lambda b, i, c: (b, i, 0)),
        out_shape=jax.ShapeDtypeStruct((bsz, L, d), F32),
        scratch_shapes=[pltpu.VMEM((nchunk, tm, tn), F32)],
        compiler_params=_params(("parallel", "parallel", "arbitrary")),
        name="glu_ln",
    )(z, w_glu, w_glu, x, gate, ln_g, ln_b)


def _oproj_ln_kernel(o_in_ref, w_ref, x_ref, gt_ref, lng_ref, lnb_ref, o_ref, acc_scr,
                     *, nchunk, tn, alpha):
    c = pl.program_id(2)
    m = _dot(o_in_ref[...], w_ref[...])
    acc_scr[c] = alpha * x_ref[...] + (1.0 + gt_ref[...]) * m

    @pl.when(c == nchunk - 1)
    def _():
        _layer_norm_chunks(acc_scr, lng_ref, lnb_ref, o_ref, nchunk, tn)


def _oproj_ln(o_in, w_o, layer, x, gate, ln_g, ln_b, tm, tn, alpha):
    bsz, L, d = x.shape
    nchunk = d // tn
    if gate.shape[1] == 1:
        gt_spec = pl.BlockSpec((None, 1, tn), lambda b, i, c: (b, 0, c))
    else:
        gt_spec = pl.BlockSpec((None, tm, tn), lambda b, i, c: (b, i, c))
    vec = pl.BlockSpec((1, d), lambda b, i, c: (0, 0))
    return pl.pallas_call(
        functools.partial(_oproj_ln_kernel, nchunk=nchunk, tn=tn, alpha=alpha),
        grid=(bsz, L // tm, nchunk),
        in_specs=[
            pl.BlockSpec((None, tm, o_in.shape[2]), lambda b, i, c: (b, i, 0)),
            pl.BlockSpec((None, w_o.shape[1], tn), lambda b, i, c: (layer, 0, c)),
            pl.BlockSpec((None, tm, tn), lambda b, i, c: (b, i, c)),
            gt_spec,
            vec,
            vec,
        ],
        out_specs=pl.BlockSpec((None, tm, d), lambda b, i, c: (b, i, 0)),
        out_shape=jax.ShapeDtypeStruct((bsz, L, d), F32),
        scratch_shapes=[pltpu.VMEM((nchunk, tm, tn), F32)],
        compiler_params=_params(("parallel", "parallel", "arbitrary")),
        name="oproj_ln",
    )(o_in, w_o, x, gate, ln_g, ln_b)


def _ffn_kernel(x_ref, sc_ref, sh_ref, gt_ref, wg_ref, wv_ref, wc_ref, bc_ref, wd_ref,
                st_ref, lng_ref, lnb_ref, o_ref, cv_ref, h_scr, acc_scr, prev_scr,
                *, nb, T, nchunk, alpha):
    i = pl.program_id(1)
    c = pl.program_id(2)
    M = nb * T
    fc = wg_ref.shape[1]

    @pl.when(c == 0)
    def _():
        h_scr[...] = _modulate(x_ref, sc_ref, sh_ref)
        acc_scr[...] = jnp.zeros(acc_scr.shape, F32)

    @pl.when(i == 0)
    def _():
        prev_scr[...] = st_ref[c]

    @pl.when(i > 0)
    def _():
        prev_scr[...] = cv_ref[c]

    h = h_scr[...]
    g = _dot(h, wg_ref[...])
    v = _dot(h, wv_ref[...])

    row = lax.broadcasted_iota(jnp.int32, (M, 1), 0)
    if nb == 1:
        tpos = row
        p0 = prev_scr[0, 0:1, :]
        p1 = prev_scr[0, 1:2, :]
    else:
        tpos = lax.rem(row, T)
        p0 = jnp.broadcast_to(prev_scr[:, 0:1, :], (nb, T, fc)).reshape(M, fc)
        p1 = jnp.broadcast_to(prev_scr[:, 1:2, :], (nb, T, fc)).reshape(M, fc)
    s1 = jnp.where(tpos == 0, p1, pltpu.roll(g, 1, 0))
    s2 = jnp.where(tpos == 0, p0, jnp.where(tpos == 1, p1, pltpu.roll(g, 2, 0)))
    conv = bc_ref[...] + s2 * wc_ref[0:1, :] + s1 * wc_ref[1:2, :] + g * wc_ref[2:3, :]
    act = (_gelu(conv) * v).astype(BF16)
    acc_scr[...] += _dot(act, wd_ref[...])
    cv_ref[c] = g.reshape(nb, T, fc)[:, T - 2:, :]

    @pl.when(c == nchunk - 1)
    def _():
        r = alpha * x_ref[...] + (1.0 + gt_ref[...]) * acc_scr[...]
        mu = jnp.mean(r, axis=-1, keepdims=True)
        cen = r - mu
        var = jnp.mean(cen * cen, axis=-1, keepdims=True)
        o_ref[...] = cen * lax.rsqrt(var + LN_EPS) * lng_ref[...] + lnb_ref[...]


def _ffn(x, sc, sh, gate, w_up, w_conv, b_conv, w_down, layer, conv_state, ln_g, ln_b,
         nb, T, fc, alpha):
    S, R, d = x.shape
    dff = w_down.shape[1]
    nchunk = dff // fc
    tm = nb * T
    ntile = R // tm
    assert nb == 1 or ntile == 1

    def mod_spec(mod):
        if mod.shape[1] == 1:
            return pl.BlockSpec((None, 1, d), lambda s, i, c: (s, 0, 0))
        return pl.BlockSpec((None, tm, d), lambda s, i, c: (s, i, 0))

    vec = pl.BlockSpec((1, d), lambda s, i, c: (0, 0))
    st_spec = pl.BlockSpec((nchunk, nb, 2, fc), lambda s, i, c: (0, s, 0, 0))
    return pl.pallas_call(
        functools.partial(_ffn_kernel, nb=nb, T=T, nchunk=nchunk, alpha=alpha),
        grid=(S, ntile, nchunk),
        in_specs=[
            pl.BlockSpec((None, tm, d), lambda s, i, c: (s, i, 0)),
            mod_spec(sc),
            mod_spec(sh),
            mod_spec(gate),
            pl.BlockSpec((None, d, fc), lambda s, i, c: (layer, 0, c)),
            pl.BlockSpec((None, d, fc), lambda s, i, c: (layer, 0, nchunk + c)),
            pl.BlockSpec((w_conv.shape[0], fc), lambda s, i, c: (0, c)),
            pl.BlockSpec((1, fc), lambda s, i, c: (0, c)),
            pl.BlockSpec((None, fc, d), lambda s, i, c: (layer, c, 0)),
            st_spec,
            vec,
            vec,
        ],
        out_specs=[
            pl.BlockSpec((None, tm, d), lambda s, i, c: (s, i, 0)),
            st_spec,
        ],
        out_shape=[
            jax.ShapeDtypeStruct((S, R, d), F32),
            jax.ShapeDtypeStruct(conv_state.shape, F32),
        ],
        scratch_shapes=[
            pltpu.VMEM((tm, d), BF16),
            pltpu.VMEM((tm, d), F32),
            pltpu.VMEM((nb, 2, fc), F32),
        ],
        compiler_params=_params(("parallel", "arbitrary", "arbitrary")),
        name="conv_ffn",
    )(x, sc, sh, gate, w_up, w_up, w_conv, b_conv, w_down, conv_state, ln_g, ln_b)


def _qkv(x, sc, sh, w_qkv, layer, tm):
    da = w_qkv.shape[2] // 3
    (q,) = _mod_matmul(x, sc, sh, w_qkv, layer, 0, da, (BF16,), tm, da, "q_proj", Q_SCALE)
    k32, k16 = _mod_matmul(x, sc, sh, w_qkv, layer, da, da, (F32, BF16), tm, da, "k_proj",
                           f32_heads=(da // HEAD_DIM, HEAD_DIM))
    v32, v16 = _mod_matmul(x, sc, sh, w_qkv, layer, 2 * da, da, (F32, BF16), tm, da, "v_proj",
                           f32_heads=(da // (2 * HEAD_DIM), 2 * HEAD_DIM))
    return q, k32, v32, k16, v16


def _chunk_id(pos):
    assert CHUNK & (CHUNK - 1) == 0
    return lax.shift_right_logical(pos, CHUNK.bit_length() - 1)


def _sub_ln(o, g_ref, lam_init):
    o = o * lax.rsqrt(jnp.mean(o * o, axis=-1, keepdims=True) + LN_EPS)
    return o * g_ref[...] * (1.0 - lam_init)


def _lane_tile(x, width):
    if width < LANES:
        return x[:, :width]
    return jnp.tile(x, (1, width // LANES))


def _online_update(s, v, m_scr, l_scr, acc_scr, idx):
    m_old = m_scr[idx]
    m_new = jnp.maximum(m_old, jnp.max(s, axis=-1, keepdims=True))
    corr = jnp.exp2(m_old - m_new)
    p = jnp.exp2(s - _lane_tile(m_new, s.shape[1]))
    l_scr[idx] = corr * l_scr[idx] + jnp.sum(p, axis=-1, keepdims=True)
    acc_scr[idx] = _lane_tile(corr, v.shape[1]) * acc_scr[idx] + _dot(p.astype(BF16), v)
    m_scr[idx] = m_new


def _attn_prompt_kernel(lam_ref, q_ref, k_ref, v_ref, g_ref, o_ref, m_scr, l_scr, acc_scr,
                        *, tq, tk, sb, lam_init):
    i = pl.program_id(2)
    maps = (slice(0, HEAD_DIM), slice(HEAD_DIM, 2 * HEAD_DIM))
    base = pl.multiple_of(i * tq, tq)

    row_chunk = _chunk_id(lax.broadcasted_iota(jnp.int32, (sb, sb), 0))
    col_chunk = _chunk_id(lax.broadcasted_iota(jnp.int32, (sb, sb), 1))
    diag_keep = col_chunk <= row_chunk
    nrb = tq // sb
    scores = [[_dot_nt(q_ref[r * sb:(r + 1) * sb, cols], k_ref[pl.ds(base, (r + 1) * sb), cols])
               for cols in maps] for r in range(nrb)]
    for r in range(nrb):
        rows = slice(r * sb, (r + 1) * sb)
        v = v_ref[pl.ds(base, (r + 1) * sb), :]
        for idx in range(len(maps)):
            s = scores[r][idx]
            s_diag = jnp.where(diag_keep, s[:, r * sb:], MASK_VALUE)
            s = s_diag if r == 0 else jnp.concatenate([s[:, :r * sb], s_diag], axis=1)
            m = jnp.max(s, axis=-1, keepdims=True)
            p = jnp.exp2(s - m)
            m_scr[idx, rows] = jnp.broadcast_to(m, (sb, LANES))
            l_scr[idx, rows] = jnp.broadcast_to(jnp.sum(p, axis=-1, keepdims=True), (sb, LANES))
            acc_scr[idx, rows] = _dot(p.astype(BF16), v)

    def full_tiles(j0, count):
        starts = [pl.multiple_of((j0 + t) * tk, tk) for t in range(count)]
        s = [[_dot_nt(q_ref[:, cols], k_ref[pl.ds(st, tk), cols]) for cols in maps]
             for st in starts]
        for t, st in enumerate(starts):
            v = v_ref[pl.ds(st, tk), :]
            for idx in range(len(maps)):
                _online_update(s[t][idx], v, m_scr, l_scr, acc_scr, idx)

    def pair(jj, carry):
        full_tiles(2 * jj, 2)
        return carry

    nfull = i * (tq // tk)
    lax.fori_loop(0, nfull // 2, pair, 0)

    @pl.when(nfull % 2 == 1)
    def _():
        full_tiles(nfull - 1, 1)

    lam = lam_ref[0, 0]
    hw = acc_scr.shape[2]
    o = (acc_scr[0] * _lane_tile(1.0 / l_scr[0], hw)
         - acc_scr[1] * _lane_tile(lam / l_scr[1], hw))
    o_ref[...] = _sub_ln(o, g_ref, lam_init).astype(o_ref.dtype)


def _attn_prompt(lam, q, k, v, subln_g, tq, tk, sb, lam_init):
    bsz, L, da = q.shape
    hw = 2 * HEAD_DIM
    nh = da // hw
    assert tq % tk == 0 and tq % sb == 0 and sb % CHUNK == 0
    return pl.pallas_call(
        functools.partial(_attn_prompt_kernel, tq=tq, tk=tk, sb=sb, lam_init=lam_init),
        grid=(bsz, nh, L // tq),
        in_specs=[
            pl.BlockSpec(memory_space=pltpu.SMEM),
            pl.BlockSpec((None, tq, hw), lambda b, h, i: (b, i, h)),
            pl.BlockSpec((None, L, hw), lambda b, h, i: (b, 0, h)),
            pl.BlockSpec((None, L, hw), lambda b, h, i: (b, 0, h)),
            pl.BlockSpec((1, hw), lambda b, h, i: (0, 0)),
        ],
        out_specs=pl.BlockSpec((None, tq, hw), lambda b, h, i: (b, i, h)),
        out_shape=jax.ShapeDtypeStruct((bsz, L, da), BF16),
        scratch_shapes=[
            pltpu.VMEM((2, tq, LANES), F32),
            pltpu.VMEM((2, tq, LANES), F32),
            pltpu.VMEM((2, tq, hw), F32),
        ],
        compiler_params=_params(("parallel", "parallel", "arbitrary")),
        name="attn_prompt",
    )(lam, q, k, v, subln_g)


def _attn_sample_kernel(lam_ref, q_ref, ck_ref, cv_ref, kn_ref, vn_ref, g_ref, o_ref,
                        m_scr, l_scr, acc_scr, vh_scr, *, P, T, pc, nsub, lam_init):
    j = pl.program_id(1)
    hw = 2 * HEAD_DIM
    qc = _chunk_id(P + lax.broadcasted_iota(jnp.int32, (T, 1), 0))

    @pl.when(j == 0)
    def _():
        m_scr[...] = jnp.full(m_scr.shape, MASK_VALUE, F32)
        l_scr[...] = jnp.zeros(l_scr.shape, F32)
        acc_scr[...] = jnp.zeros(acc_scr.shape, F32)

    keep_c = _chunk_id(j * pc + lax.broadcasted_iota(jnp.int32, (1, pc), 1)) <= qc
    nh = nsub // 2
    for half in range(2):
        vh_scr[half] = cv_ref[:, half * LANES:(half + 1) * LANES]
    scores = []
    for s in range(nsub):
        k = ck_ref[pl.ds(s, pc, stride=nsub), :].astype(BF16)
        q = q_ref[:, s * HEAD_DIM:(s + 1) * HEAD_DIM]
        scores.append(jnp.where(keep_c, _dot_nt(q, k), MASK_VALUE))
    for h in range(nh):
        v = jnp.concatenate([vh_scr[half, pl.ds(h, pc, stride=nh), :] for half in range(2)],
                            axis=1).astype(BF16)
        for s in (2 * h, 2 * h + 1):
            _online_update(scores[s], v, m_scr, l_scr, acc_scr, s)

    @pl.when(j == pl.num_programs(1) - 1)
    def _():
        keep_n = _chunk_id(P + lax.broadcasted_iota(jnp.int32, (1, T), 1)) <= qc
        lam = lam_ref[0, 0]
        new_scores = []
        for s in range(nsub):
            cols = slice(s * HEAD_DIM, (s + 1) * HEAD_DIM)
            new_scores.append(
                jnp.where(keep_n, _dot_nt(q_ref[:, cols], kn_ref[:, cols]), MASK_VALUE))
        for h in range(nh):
            vcols = slice(h * hw, (h + 1) * hw)
            vn = vn_ref[:, vcols]
            for s in (2 * h, 2 * h + 1):
                _online_update(new_scores[s], vn, m_scr, l_scr, acc_scr, s)
        for h in range(nh):
            vcols = slice(h * hw, (h + 1) * hw)
            o = (acc_scr[2 * h] * _lane_tile(1.0 / l_scr[2 * h], hw)
                 - acc_scr[2 * h + 1] * _lane_tile(lam / l_scr[2 * h + 1], hw))
            o_ref[:, vcols] = _sub_ln(o, g_ref, lam_init).astype(o_ref.dtype)


def _attn_sample(lam, q, cache_k, cache_v, layer, k_new, v_new, subln_g, pc, lam_init):
    bsz, T, da = q.shape
    na, _, P, nsub, _ = cache_k.shape
    hw = 2 * HEAD_DIM
    new_spec = pl.BlockSpec((None, T, da), lambda b, j: (b, 0, 0))
    return pl.pallas_call(
        functools.partial(_attn_sample_kernel, P=P, T=T, pc=pc, nsub=nsub, lam_init=lam_init),
        grid=(bsz, P // pc),
        in_specs=[
            pl.BlockSpec(memory_space=pltpu.SMEM),
            new_spec,
            pl.BlockSpec((None, None, pc * nsub, HEAD_DIM), lambda b, j: (layer, b, j, 0)),
            pl.BlockSpec((None, None, pc * (nsub // 2), hw), lambda b, j: (layer, b, j, 0)),
            new_spec,
            new_spec,
            pl.BlockSpec((1, hw), lambda b, j: (0, 0)),
        ],
        out_specs=new_spec,
        out_shape=jax.ShapeDtypeStruct((bsz, T, da), BF16),
        scratch_shapes=[
            pltpu.VMEM((nsub, T, LANES), F32),
            pltpu.VMEM((nsub, T, LANES), F32),
            pltpu.VMEM((nsub, T, hw), F32),
            pltpu.VMEM((2, pc * (nsub // 2), LANES), F32),
        ],
        compiler_params=_params(("parallel", "arbitrary")),
        name="attn_sample",
    )(lam, q, cache_k.reshape(na, bsz, P * nsub, HEAD_DIM),
      cache_v.reshape(na, bsz, P * (nsub // 2), hw), k_new, v_new, subln_g)


def _ssm_discretise(lam_re, lam_im, log_step, b_re, b_im):
    lr = jnp.minimum(lam_re, -1e-4)
    li = lam_im
    dt = jnp.exp(log_step)[:, None]
    mag = jnp.exp(lr * dt)
    abar_re = mag * jnp.cos(li * dt)
    abar_im = mag * jnp.sin(li * dt)
    nr = abar_re - 1.0
    ni = abar_im
    den = lr * lr + li * li
    kr = (nr * lr + ni * li) / den
    ki = (ni * lr - nr * li) / den
    bbar_re = kr[..., None] * b_re - ki[..., None] * b_im
    bbar_im = kr[..., None] * b_im + ki[..., None] * b_re
    return abar_re, abar_im, bbar_re, bbar_im


def _block_diag(m):
    nblk, gb, r, c = m.shape
    eye = jnp.eye(gb, dtype=m.dtype)
    return jnp.einsum("jgrc,gh->jgrhc", m, eye).reshape(nblk, gb * r, gb * c)


def _ssm_matrices(lam_re, lam_im, log_step, b_re, b_im, c_re, c_im):
    G = lam_re.shape[0]
    gb = SSM_GROUPS_PER_BLOCK
    nblk = G // gb
    abar_re, abar_im, bbar_re, bbar_im = _ssm_discretise(lam_re, lam_im, log_step, b_re, b_im)
    bt_re = bbar_re.reshape(nblk, gb, STATE_P, GROUP_CH).swapaxes(2, 3)
    bt_im = bbar_im.reshape(nblk, gb, STATE_P, GROUP_CH).swapaxes(2, 3)
    bd_b = jnp.concatenate([_block_diag(bt_re), _block_diag(bt_im)], axis=2).astype(BF16)
    ct_re = c_re.reshape(nblk, gb, GROUP_CH, STATE_P).swapaxes(2, 3)
    ct_im = c_im.reshape(nblk, gb, GROUP_CH, STATE_P).swapaxes(2, 3)
    bd_c = jnp.concatenate([_block_diag(ct_re), _block_diag(-ct_im)], axis=1).astype(BF16)
    a_re = abar_re.reshape(nblk, 1, gb * STATE_P)
    a_im = abar_im.reshape(nblk, 1, gb * STATE_P)
    return bd_b, a_re, a_im, bd_c


def _chunk_state(state, fc):
    nbt, w, dff = state.shape
    return state.reshape(nbt, w, dff // fc, fc).transpose(2, 0, 1, 3)


def _unchunk_state(state):
    nchunk, nbt, w, fc = state.shape
    return state.transpose(1, 2, 0, 3).reshape(nbt, w, nchunk * fc)


TM_PROMPT = 512
TN_GLU = 1024
FFN_CHUNK = 512
SSM_TT = 128
SSM_CB = 4
SAMPLE_PC = 512
TQ = 1024
TK = 512
TQ_DIAG = 256


def kernel(x_prompt, x_sample, c_prompt, c_sample, cache_k, cache_v, state_ssm_re, state_ssm_im, state_conv, w_ada, b_ada, ln_g, ln_b, w_up, w_dconv, b_dconv, w_down, w_ssm_in, ssm_lam_re, ssm_lam_im, ssm_log_step, ssm_b_re, ssm_b_im, ssm_c_re, ssm_c_im, ssm_d, w_glu, w_qkv, lam_q1, lam_k1, lam_q2, lam_k2, subln_g, w_o):
    depth = w_ada.shape[0]
    bp, L, d = x_prompt.shape
    bs, T, _ = x_sample.shape
    dff = w_down.shape[1]
    alpha = (2 * depth) ** 0.25
    rows_s = bs * T

    mods = _ada(jnp.concatenate([c_prompt, c_sample], axis=0), w_ada, b_ada)
    w_up16, w_down16 = w_up.astype(BF16), w_down.astype(BF16)
    w_in16, w_glu16 = w_ssm_in.astype(BF16), w_glu.astype(BF16)
    w_qkv16, w_o16 = w_qkv.astype(BF16), w_o.astype(BF16)

    xp = x_prompt
    xs = x_sample.reshape(1, rows_s, d)
    zeros_conv = jnp.zeros((bp, state_conv.shape[2], dff), F32)
    outs = {name: [] for name in ("kp", "vp", "srp", "sip", "cvp", "ks", "vs", "srs", "sis", "cvs")}

    for i in range(depth):
        mp = [m.reshape(bp, 1, d) for m in jnp.split(mods[i, :bp], 6, axis=-1)]
        ms = [jnp.repeat(m, T, axis=0).reshape(1, rows_s, d)
              for m in jnp.split(mods[i, bp:], 6, axis=-1)]
        shp1, scp1, gtp1, shp2, scp2, gtp2 = mp
        shs1, scs1, gts1, shs2, scs2, gts2 = ms
        lng = ln_g[i].reshape(2, 1, d)
        lnb = ln_b[i].reshape(2, 1, d)
        j = i // 2
        if i % 2 == 0:
            bd_b, a_re, a_im, bd_c = _ssm_matrices(
                ssm_lam_re[j], ssm_lam_im[j], ssm_log_step[j], ssm_b_re[j], ssm_b_im[j],
                ssm_c_re[j], ssm_c_im[j])
            d_skip = ssm_d[j].reshape(1, d)
            gp = ssm_lam_re.shape[1] * ssm_lam_re.shape[2]
            (up,) = _mod_matmul(xp, scp1, shp1, w_in16, j, 0, d, (F32,), TM_PROMPT, d, "ssm_in")
            zero_state = jnp.zeros((bp, gp), F32)
            zp, srp, sip = _ssm_scan(up, d_skip, bd_b, a_re, a_im, bd_c, zero_state, zero_state,
                                     SSM_TT, SSM_CB)
            xp = _glu_ln(zp, w_glu16, j, xp, gtp1, lng[0], lnb[0], TM_PROMPT, TN_GLU, alpha)
            (us,) = _mod_matmul(xs, scs1, shs1, w_in16, j, 0, d, (F32,), rows_s, d, "ssm_in")
            zs, srs, sis = _ssm_scan(us.reshape(bs, T, d), d_skip, bd_b, a_re, a_im, bd_c,
                                     state_ssm_re[j].reshape(bs, gp),
                                     state_ssm_im[j].reshape(bs, gp), T, SSM_CB)
            xs = _glu_ln(zs.reshape(1, rows_s, d), w_glu16, j, xs, gts1, lng[0], lnb[0],
                         rows_s, TN_GLU, alpha)
            st_shape = ssm_lam_re.shape[1:]
            outs["srp"].append(srp.reshape((bp,) + st_shape))
            outs["sip"].append(sip.reshape((bp,) + st_shape))
            outs["srs"].append(srs.reshape((bs,) + st_shape))
            outs["sis"].append(sis.reshape((bs,) + st_shape))
        else:
            lam_init = 0.8 - 0.6 * math.exp(-0.3 * i)
            lam = (jnp.exp(jnp.sum(lam_q1[j] * lam_k1[j])) - jnp.exp(jnp.sum(lam_q2[j] * lam_k2[j]))
                   + lam_init).reshape(1, 1)
            sg = subln_g[j].reshape(1, 2 * HEAD_DIM)
            nsub = cache_k.shape[3]
            nhead = cache_v.shape[3]
            qp, kp, vp, kp16, vp16 = _qkv(xp, scp1, shp1, w_qkv16, j, TM_PROMPT)
            op = _attn_prompt(lam, qp, kp16, vp16, sg, TQ, TK, TQ_DIAG, lam_init)
            xp = _oproj_ln(op, w_o16, j, xp, gtp1, lng[0], lnb[0], TM_PROMPT, d, alpha)
            qs, ksn, vsn, ks16, vs16 = _qkv(xs, scs1, shs1, w_qkv16, j, rows_s)
            da = w_o.shape[1]
            osm = _attn_sample(lam, qs.reshape(bs, T, da), cache_k, cache_v, j,
                               ks16.reshape(bs, T, da), vs16.reshape(bs, T, da), sg,
                               SAMPLE_PC, lam_init)
            xs = _oproj_ln(osm.reshape(1, rows_s, da), w_o16, j, xs, gts1, lng[0], lnb[0],
                           rows_s, d, alpha)
            outs["kp"].append(kp.reshape(bp, L, nsub, HEAD_DIM))
            outs["vp"].append(vp.reshape(bp, L, nhead, 2 * HEAD_DIM))
            outs["ks"].append(ksn.reshape(bs, T, nsub, HEAD_DIM))
            outs["vs"].append(vsn.reshape(bs, T, nhead, 2 * HEAD_DIM))

        bconv = b_dconv[i].reshape(1, dff)
        xp, cvp = _ffn(xp, scp2, shp2, gtp2, w_up16, w_dconv[i], bconv, w_down16, i,
                       _chunk_state(zeros_conv, FFN_CHUNK), lng[1], lnb[1],
                       1, TM_PROMPT, FFN_CHUNK, alpha)
        xs, cvs = _ffn(xs, scs2, shs2, gts2, w_up16, w_dconv[i], bconv, w_down16, i,
                       _chunk_state(state_conv[i], FFN_CHUNK), lng[1], lnb[1],
                       bs, T, FFN_CHUNK, alpha)
        outs["cvp"].append(_unchunk_state(cvp))
        outs["cvs"].append(_unchunk_state(cvs))

    return (xp, xs.reshape(bs, T, d),
            jnp.stack(outs["kp"]), jnp.stack(outs["vp"]),
            jnp.stack(outs["srp"]), jnp.stack(outs["sip"]), jnp.stack(outs["cvp"]),
            jnp.stack(outs["ks"]), jnp.stack(outs["vs"]),
            jnp.stack(outs["srs"]), jnp.stack(outs["sis"]), jnp.stack(outs["cvs"]))
```

```python
import functools
import math

import jax
import jax.numpy as jnp
from jax import lax
from jax.experimental import pallas as pl
from jax.experimental.pallas import tpu as pltpu

F32 = jnp.float32
BF16 = jnp.bfloat16

CHUNK = 64
HEAD_DIM = 128
GROUP_CH = 16
STATE_P = 64
LN_EPS = 1e-5
MASK_VALUE = -1e30
Q_SCALE = math.log2(math.e) * HEAD_DIM ** -0.5

SUBLANES = 8
LANES = 128
VMEM_LIMIT = 56 * 1024 * 1024

SSM_GROUPS_PER_BLOCK = LANES // GROUP_CH
SSM_HS = SSM_GROUPS_PER_BLOCK * STATE_P


def _params(sem):
    return pltpu.CompilerParams(dimension_semantics=sem, vmem_limit_bytes=VMEM_LIMIT)


def _gelu(x):
    return 0.5 * x * (1.0 + lax.erf(x * math.sqrt(0.5)))


def _dot(a, b):
    return jnp.dot(a, b, preferred_element_type=F32)


def _dot_nt(a, b):
    return lax.dot_general(a, b, (((1,), (1,)), ((), ())), preferred_element_type=F32)


def _ada_kernel(c_ref, w_ref, b_ref, o_ref):
    c = c_ref[...]
    a = (c * jax.nn.sigmoid(c)).astype(BF16)
    o_ref[...] = _dot(a, w_ref[...].astype(BF16)) + b_ref[...]


def _ada(c_all, w_ada, b_ada, tn=1024):
    depth, d, n = w_ada.shape
    rows = c_all.shape[0]
    return pl.pallas_call(
        _ada_kernel,
        grid=(depth, n // tn),
        in_specs=[
            pl.BlockSpec((rows, d), lambda l, j: (0, 0)),
            pl.BlockSpec((None, d, tn), lambda l, j: (l, 0, j)),
            pl.BlockSpec((None, 1, tn), lambda l, j: (l, 0, j)),
        ],
        out_specs=pl.BlockSpec((None, rows, tn), lambda l, j: (l, 0, j)),
        out_shape=jax.ShapeDtypeStruct((depth, rows, n), F32),
        compiler_params=_params(("parallel", "parallel")),
        name="ada",
    )(c_all, w_ada, b_ada.reshape(depth, 1, n))


def _modulate(x_ref, sc_ref, sh_ref):
    return (x_ref[...] * (1.0 + sc_ref[...]) + sh_ref[...]).astype(BF16)


def _store_heads(o_ref, r, heads):
    nh, hd = heads
    rows = r.shape[0]
    for h in range(nh):
        part = r[:, h * hd:(h + 1) * hd]
        if hd == LANES:
            o_ref[pl.ds(h, rows, stride=nh), :] = part
        else:
            o_ref[:, h, :] = part


def _modmm_kernel(x_ref, sc_ref, sh_ref, w_ref, *rest, bf16_scale, f32_heads):
    *out_refs, h_scr = rest

    @pl.when(pl.program_id(2) == 0)
    def _():
        h_scr[...] = _modulate(x_ref, sc_ref, sh_ref)

    r = _dot(h_scr[...], w_ref[...])
    for o_ref in out_refs:
        if o_ref.dtype != F32:
            o_ref[...] = (r if bf16_scale == 1.0 else r * bf16_scale).astype(o_ref.dtype)
        elif f32_heads is None:
            o_ref[...] = r
        else:
            _store_heads(o_ref, r, f32_heads)


def _mod_spec(mod, tm):
    if mod.shape[1] == 1:
        return pl.BlockSpec((None, 1, mod.shape[2]), lambda b, i, n: (b, 0, 0))
    return pl.BlockSpec((None, tm, mod.shape[2]), lambda b, i, n: (b, i, 0))


def _mod_matmul(x, sc, sh, w, layer, col0, n, out_dtypes, tm, tn, name, bf16_scale=1.0,
                f32_heads=None):
    bsz, L, d = x.shape
    assert col0 % tn == 0 and n % tn == 0
    jb = col0 // tn
    nt = L // tm
    flat_spec = pl.BlockSpec((None, tm, tn), lambda b, i, j: (b, i, j))
    out_specs, out_shape = [], []
    for dt in out_dtypes:
        if dt != F32 or f32_heads is None:
            out_specs.append(flat_spec)
            out_shape.append(jax.ShapeDtypeStruct((bsz, L, n), dt))
            continue
        nh, hd = f32_heads
        assert tn == n == nh * hd
        if hd == LANES:
            out_specs.append(pl.BlockSpec((tm * nh, hd), lambda b, i, j: (b * nt + i, 0)))
            out_shape.append(jax.ShapeDtypeStruct((bsz * L * nh, hd), dt))
        else:
            out_specs.append(pl.BlockSpec((None, tm, nh, hd), lambda b, i, j: (b, i, 0, 0)))
            out_shape.append(jax.ShapeDtypeStruct((bsz, L, nh, hd), dt))
    return pl.pallas_call(
        functools.partial(_modmm_kernel, bf16_scale=bf16_scale, f32_heads=f32_heads),
        grid=(bsz, nt, n // tn),
        in_specs=[
            pl.BlockSpec((None, tm, d), lambda b, i, j: (b, i, 0)),
            _mod_spec(sc, tm),
            _mod_spec(sh, tm),
            pl.BlockSpec((None, d, tn), lambda b, i, j: (layer, 0, jb + j)),
        ],
        out_specs=out_specs,
        out_shape=out_shape,
        scratch_shapes=[pltpu.VMEM((tm, d), BF16)],
        compiler_params=_params(("parallel", "parallel", "arbitrary")),
        name=name,
    )(x, sc, sh, w)


def _ssm_scan_tile(u_ref, bdb_ref, are_ref, aim_ref, tb, x, h_scr, tt, blocks):
    hs = SSM_HS
    for c in blocks:
        lanes = slice(c * LANES, (c + 1) * LANES)
        for b in range(SUBLANES):
            tb[c, pl.ds(b, tt, stride=SUBLANES), :] = u_ref[b, :, lanes]
        x[c] = _dot(tb[c].astype(BF16), bdb_ref[c]).reshape(tt, SUBLANES, 2 * hs)
    for c in blocks:
        ar = jnp.broadcast_to(are_ref[c], (SUBLANES, hs))
        ai = jnp.broadcast_to(aim_ref[c], (SUBLANES, hs))
        hr = h_scr[c, :, :hs]
        hi = h_scr[c, :, hs:]
        for t in range(tt):
            hr, hi = ((ar * hr - ai * hi) + x[c, t, :, :hs],
                      (ar * hi + ai * hr) + x[c, t, :, hs:])
            x[c, t, :, :hs] = hr
            x[c, t, :, hs:] = hi
        h_scr[c, :, :hs] = hr
        h_scr[c, :, hs:] = hi


def _ssm_out_tile(d_ref, bdc_ref, tb, x, z_ref, tt, blocks):
    for c in blocks:
        lanes = slice(c * LANES, (c + 1) * LANES)
        hb = x[c].reshape(tt * SUBLANES, 2 * SSM_HS).astype(BF16)
        y = _dot(hb, bdc_ref[c])
        tb[c] = _gelu(y + d_ref[:, lanes] * tb[c])
        for b in range(SUBLANES):
            z_ref[b, :, lanes] = tb[c, pl.ds(b, tt, stride=SUBLANES), :].astype(z_ref.dtype)


def _ssm_kernel(u_ref, d_ref, bdb_ref, are_ref, aim_ref, bdc_ref, sre_ref, sim_ref,
                z_ref, ore_ref, oim_ref, tb0, tb1, x0, x1, h_scr, *, tt, nt, cb):
    i = pl.program_id(2)
    hs = SSM_HS
    bufs = ((tb0, x0), (tb1, x1))
    scan = functools.partial(_ssm_scan_tile, u_ref, bdb_ref, are_ref, aim_ref)
    out = functools.partial(_ssm_out_tile, d_ref, bdc_ref)

    @pl.when(i == 0)
    def _():
        for c in range(cb):
            h_scr[c, :, :hs] = sre_ref[:, c * hs:(c + 1) * hs]
            h_scr[c, :, hs:] = sim_ref[:, c * hs:(c + 1) * hs]
        scan(*bufs[0], h_scr, tt, range(cb))

    for parity in range(2):
        @pl.when((i > 0) & (i < nt) & (i % 2 == parity))
        def _():
            for c in range(cb):
                scan(*bufs[parity], h_scr, tt, (c,))
                out(*bufs[1 - parity], z_ref, tt, (c,))

    @pl.when(i == nt)
    def _():
        out(*bufs[(nt - 1) % 2], z_ref, tt, range(cb))
        for c in range(cb):
            ore_ref[:, c * hs:(c + 1) * hs] = h_scr[c, :, :hs]
            oim_ref[:, c * hs:(c + 1) * hs] = h_scr[c, :, hs:]


def _ssm_scan(u, d_skip, bd_b, a_re, a_im, bd_c, s_re, s_im, tt, cb):
    bsz, L, d = u.shape
    nblk = d // LANES
    hs = SSM_HS
    nt = L // tt
    bw = cb * LANES
    st_spec = pl.BlockSpec((SUBLANES, cb * hs), lambda j, g, i: (g, j))
    in_spec = pl.BlockSpec((SUBLANES, tt, bw),
                           lambda j, g, i: (g, jnp.minimum(i, nt - 1), j))
    out_spec = pl.BlockSpec((SUBLANES, tt, bw),
                            lambda j, g, i: (g, jnp.maximum(i - 1, 0), j))
    return pl.pallas_call(
        functools.partial(_ssm_kernel, tt=tt, nt=nt, cb=cb),
        grid=(nblk // cb, bsz // SUBLANES, nt + 1),
        in_specs=[
            in_spec,
            pl.BlockSpec((1, bw), lambda j, g, i: (0, j)),
            pl.BlockSpec((cb, LANES, 2 * hs), lambda j, g, i: (j, 0, 0)),
            pl.BlockSpec((cb, 1, hs), lambda j, g, i: (j, 0, 0)),
            pl.BlockSpec((cb, 1, hs), lambda j, g, i: (j, 0, 0)),
            pl.BlockSpec((cb, 2 * hs, LANES), lambda j, g, i: (j, 0, 0)),
            st_spec,
            st_spec,
        ],
        out_specs=[out_spec, st_spec, st_spec],
        out_shape=[
            jax.ShapeDtypeStruct((bsz, L, d), BF16),
            jax.ShapeDtypeStruct(s_re.shape, F32),
            jax.ShapeDtypeStruct(s_im.shape, F32),
        ],
        scratch_shapes=[
            pltpu.VMEM((cb, tt * SUBLANES, LANES), F32),
            pltpu.VMEM((cb, tt * SUBLANES, LANES), F32),
            pltpu.VMEM((cb, tt, SUBLANES, 2 * hs), F32),
            pltpu.VMEM((cb, tt, SUBLANES, 2 * hs), F32),
            pltpu.VMEM((cb, SUBLANES, 2 * hs), F32),
        ],
        compiler_params=_params(("parallel", "parallel", "arbitrary")),
        name="ssm_scan",
    )(u, d_skip, bd_b, a_re, a_im, bd_c, s_re, s_im)


def _layer_norm_chunks(acc_scr, lng_ref, lnb_ref, o_ref, nchunk, tn):
    d = nchunk * tn
    parts = [acc_scr[c] for c in range(nchunk)]
    mu = sum(jnp.sum(p, axis=-1, keepdims=True) for p in parts) * (1.0 / d)
    cen = [p - mu for p in parts]
    var = sum(jnp.sum(q * q, axis=-1, keepdims=True) for q in cen) * (1.0 / d)
    inv = lax.rsqrt(var + LN_EPS)
    for c in range(nchunk):
        sl = slice(c * tn, (c + 1) * tn)
        o_ref[:, sl] = cen[c] * inv * lng_ref[:, sl] + lnb_ref[:, sl]


def _glu_ln_kernel(z_ref, wa_ref, wg_ref, x_ref, gt_ref, lng_ref, lnb_ref,
                   o_ref, acc_scr, *, nchunk, tn, alpha):
    c = pl.program_id(2)
    z = z_ref[...]
    a = _dot(z, wa_ref[...])
    g = _dot(z, wg_ref[...])
    m = a * jax.nn.sigmoid(g)
    acc_scr[c] = alpha * x_ref[...] + (1.0 + gt_ref[...]) * m

    @pl.when(c == nchunk - 1)
    def _():
        _layer_norm_chunks(acc_scr, lng_ref, lnb_ref, o_ref, nchunk, tn)


def _glu_ln(z, w_glu, layer, x, gate, ln_g, ln_b, tm, tn, alpha):
    bsz, L, d = x.shape
    nchunk = d // tn
    if gate.shape[1] == 1:
        gt_spec = pl.BlockSpec((None, 1, tn), lambda b, i, c: (b, 0, c))
    else:
        gt_spec = pl.BlockSpec((None, tm, tn), lambda b, i, c: (b, i, c))
    vec = pl.BlockSpec((1, d), lambda b, i, c: (0, 0))
    return pl.pallas_call(
        functools.partial(_glu_ln_kernel, nchunk=nchunk, tn=tn, alpha=alpha),
        grid=(bsz, L // tm, nchunk),
        in_specs=[
            pl.BlockSpec((None, tm, d), lambda b, i, c: (b, i, 0)),
            pl.BlockSpec((None, d, tn), lambda b, i, c: (layer, 0, c)),
            pl.BlockSpec((None, d, tn), lambda b, i, c: (layer, 0, nchunk + c)),
            pl.BlockSpec((None, tm, tn), lambda b, i, c: (b, i, c)),
            gt_spec,
            vec,
            vec,
        ],
        out_specs=pl.BlockSpec((None, tm, d), lambda b, i, c: (b, i, 0)),
        out_shape=jax.ShapeDtypeStruct((bsz, L, d), F32),
        scratch_shapes=[pltpu.VMEM((nchunk, tm, tn), F32)],
        compiler_params=_params(("parallel", "parallel", "arbitrary")),
        name="glu_ln",
    )(z, w_glu, w_glu, x, gate, ln_g, ln_b)


def _oproj_ln_kernel(o_in_ref, w_ref, x_ref, gt_ref, lng_ref, lnb_ref, o_ref, acc_scr,
                     *, nchunk, tn, alpha):
    c = pl.program_id(2)
    m = _dot(o_in_ref[...], w_ref[...])
    acc_scr[c] = alpha * x_ref[...] + (1.0 + gt_ref[...]) * m

    @pl.when(c == nchunk - 1)
    def _():
        _layer_norm_chunks(acc_scr, lng_ref, lnb_ref, o_ref, nchunk, tn)


def _oproj_ln(o_in, w_o, layer, x, gate, ln_g, ln_b, tm, tn, alpha):
    bsz, L, d = x.shape
    nchunk = d // tn
    if gate.shape[1] == 1:
        gt_spec = pl.BlockSpec((None, 1, tn), lambda b, i, c: (b, 0, c))
    else:
        gt_spec = pl.BlockSpec((None, tm, tn), lambda b, i, c: (b, i, c))
    vec = pl.BlockSpec((1, d), lambda b, i, c: (0, 0))
    return pl.pallas_call(
        functools.partial(_oproj_ln_kernel, nchunk=nchunk, tn=tn, alpha=alpha),
        grid=(bsz, L // tm, nchunk),
        in_specs=[
            pl.BlockSpec((None, tm, o_in.shape[2]), lambda b, i, c: (b, i, 0)),
            pl.BlockSpec((None, w_o.shape[1], tn), lambda b, i, c: (layer, 0, c)),
            pl.BlockSpec((None, tm, tn), lambda b, i, c: (b, i, c)),
            gt_spec,
            vec,
            vec,
        ],
        out_specs=pl.BlockSpec((None, tm, d), lambda b, i, c: (b, i, 0)),
        out_shape=jax.ShapeDtypeStruct((bsz, L, d), F32),
        scratch_shapes=[pltpu.VMEM((nchunk, tm, tn), F32)],
        compiler_params=_params(("parallel", "parallel", "arbitrary")),
        name="oproj_ln",
    )(o_in, w_o, x, gate, ln_g, ln_b)


def _ffn_kernel(x_ref, sc_ref, sh_ref, gt_ref, wg_ref, wv_ref, wc_ref, bc_ref, wd_ref,
                st_ref, lng_ref, lnb_ref, o_ref, cv_ref, h_scr, acc_scr, prev_scr,
                *, nb, T, nchunk, alpha):
    i = pl.program_id(1)
    c = pl.program_id(2)
    M = nb * T
    fc = wg_ref.shape[1]

    @pl.when(c == 0)
    def _():
        h_scr[...] = _modulate(x_ref, sc_ref, sh_ref)
        acc_scr[...] = jnp.zeros(acc_scr.shape, F32)

    @pl.when(i == 0)
    def _():
        prev_scr[...] = st_ref[c]

    @pl.when(i > 0)
    def _():
        prev_scr[...] = cv_ref[c]

    h = h_scr[...]
    g = _dot(h, wg_ref[...])
    v = _dot(h, wv_ref[...])

    row = lax.broadcasted_iota(jnp.int32, (M, 1), 0)
    if nb == 1:
        tpos = row
        p0 = prev_scr[0, 0:1, :]
        p1 = prev_scr[0, 1:2, :]
    else:
        tpos = lax.rem(row, T)
        p0 = jnp.broadcast_to(prev_scr[:, 0:1, :], (nb, T, fc)).reshape(M, fc)
        p1 = jnp.broadcast_to(prev_scr[:, 1:2, :], (nb, T, fc)).reshape(M, fc)
    s1 = jnp.where(tpos == 0, p1, pltpu.roll(g, 1, 0))
    s2 = jnp.where(tpos == 0, p0, jnp.where(tpos == 1, p1, pltpu.roll(g, 2, 0)))
    conv = bc_ref[c] + s2 * wc_ref[c, 0:1, :] + s1 * wc_ref[c, 1:2, :] + g * wc_ref[c, 2:3, :]
    act = (_gelu(conv) * v).astype(BF16)
    acc_scr[...] += _dot(act, wd_ref[...])
    cv_ref[c] = g.reshape(nb, T, fc)[:, T - 2:, :]

    @pl.when(c == nchunk - 1)
    def _():
        r = alpha * x_ref[...] + (1.0 + gt_ref[...]) * acc_scr[...]
        mu = jnp.mean(r, axis=-1, keepdims=True)
        cen = r - mu
        var = jnp.mean(cen * cen, axis=-1, keepdims=True)
        o_ref[...] = cen * lax.rsqrt(var + LN_EPS) * lng_ref[...] + lnb_ref[...]


def _ffn(x, sc, sh, gate, w_up, w_conv, b_conv, w_down, layer, conv_state, ln_g, ln_b,
         nb, T, fc, alpha):
    S, R, d = x.shape
    dff = w_down.shape[1]
    nchunk = dff // fc
    tm = nb * T
    ntile = R // tm
    assert nb == 1 or ntile == 1

    def mod_spec(mod):
        if mod.shape[1] == 1:
            return pl.BlockSpec((None, 1, d), lambda s, i, c: (s, 0, 0))
        return pl.BlockSpec((None, tm, d), lambda s, i, c: (s, i, 0))

    vec = pl.BlockSpec((1, d), lambda s, i, c: (0, 0))
    st_spec = pl.BlockSpec((nchunk, nb, 2, fc), lambda s, i, c: (0, s, 0, 0))
    return pl.pallas_call(
        functools.partial(_ffn_kernel, nb=nb, T=T, nchunk=nchunk, alpha=alpha),
        grid=(S, ntile, nchunk),
        in_specs=[
            pl.BlockSpec((None, tm, d), lambda s, i, c: (s, i, 0)),
            mod_spec(sc),
            mod_spec(sh),
            mod_spec(gate),
            pl.BlockSpec((None, d, fc), lambda s, i, c: (layer, 0, c)),
            pl.BlockSpec((None, d, fc), lambda s, i, c: (layer, 0, nchunk + c)),
            pl.BlockSpec((nchunk, w_conv.shape[0], fc), lambda s, i, c: (0, 0, 0)),
            pl.BlockSpec((nchunk, 1, fc), lambda s, i, c: (0, 0, 0)),
            pl.BlockSpec((None, fc, d), lambda s, i, c: (layer, c, 0)),
            st_spec,
            vec,
            vec,
        ],
        out_specs=[
            pl.BlockSpec((None, tm, d), lambda s, i, c: (s, i, 0)),
            st_spec,
        ],
        out_shape=[
            jax.ShapeDtypeStruct((S, R, d), F32),
            jax.ShapeDtypeStruct(conv_state.shape, F32),
        ],
        scratch_shapes=[
            pltpu.VMEM((tm, d), BF16),
            pltpu.VMEM((tm, d), F32),
            pltpu.VMEM((nb, 2, fc), F32),
        ],
        compiler_params=_params(("parallel", "arbitrary", "arbitrary")),
        name="conv_ffn",
    )(x, sc, sh, gate, w_up, w_up,
      w_conv.reshape(w_conv.shape[0], nchunk, fc).swapaxes(0, 1),
      b_conv.reshape(nchunk, 1, fc), w_down, conv_state, ln_g, ln_b)


def _qkv(x, sc, sh, w_qkv, layer, tm):
    da = w_qkv.shape[2] // 3
    (q,) = _mod_matmul(x, sc, sh, w_qkv, layer, 0, da, (BF16,), tm, da, "q_proj", Q_SCALE)
    k32, k16 = _mod_matmul(x, sc, sh, w_qkv, layer, da, da, (F32, BF16), tm, da, "k_proj",
                           f32_heads=(da // HEAD_DIM, HEAD_DIM))
    v32, v16 = _mod_matmul(x, sc, sh, w_qkv, layer, 2 * da, da, (F32, BF16), tm, da, "v_proj",
                           f32_heads=(da // (2 * HEAD_DIM), 2 * HEAD_DIM))
    return q, k32, v32, k16, v16


def _chunk_id(pos):
    assert CHUNK & (CHUNK - 1) == 0
    return lax.shift_right_logical(pos, CHUNK.bit_length() - 1)


def _sub_ln(o, g_ref, lam_init):
    o = o * lax.rsqrt(jnp.mean(o * o, axis=-1, keepdims=True) + LN_EPS)
    return o * g_ref[...] * (1.0 - lam_init)


def _lane_tile(x, width):
    if width < LANES:
        return x[:, :width]
    return jnp.tile(x, (1, width // LANES))


def _online_update(s, v, m_scr, l_scr, acc_scr, idx):
    m_old = m_scr[idx]
    m_new = jnp.maximum(m_old, jnp.max(s, axis=-1, keepdims=True))
    corr = jnp.exp2(m_old - m_new)
    p = jnp.exp2(s - _lane_tile(m_new, s.shape[1]))
    l_scr[idx] = corr * l_scr[idx] + jnp.sum(p, axis=-1, keepdims=True)
    acc_scr[idx] = _lane_tile(corr, v.shape[1]) * acc_scr[idx] + _dot(p.astype(BF16), v)
    m_scr[idx] = m_new


def _attn_prompt_kernel(lam_ref, q_ref, k_ref, v_ref, g_ref, o_ref, m_scr, l_scr, acc_scr,
                        *, tq, tk, sb, lam_init):
    i = pl.program_id(2)
    maps = (slice(0, HEAD_DIM), slice(HEAD_DIM, 2 * HEAD_DIM))
    base = pl.multiple_of(i * tq, tq)

    row_chunk = _chunk_id(lax.broadcasted_iota(jnp.int32, (sb, sb), 0))
    col_chunk = _chunk_id(lax.broadcasted_iota(jnp.int32, (sb, sb), 1))
    diag_keep = col_chunk <= row_chunk
    nrb = tq // sb
    scores = [[_dot_nt(q_ref[r * sb:(r + 1) * sb, cols], k_ref[pl.ds(base, (r + 1) * sb), cols])
               for cols in maps] for r in range(nrb)]
    for r in range(nrb):
        rows = slice(r * sb, (r + 1) * sb)
        v = v_ref[pl.ds(base, (r + 1) * sb), :]
        for idx in range(len(maps)):
            s = scores[r][idx]
            s_diag = jnp.where(diag_keep, s[:, r * sb:], MASK_VALUE)
            s = s_diag if r == 0 else jnp.concatenate([s[:, :r * sb], s_diag], axis=1)
            m = jnp.max(s, axis=-1, keepdims=True)
            p = jnp.exp2(s - m)
            m_scr[idx, rows] = jnp.broadcast_to(m, (sb, LANES))
            l_scr[idx, rows] = jnp.broadcast_to(jnp.sum(p, axis=-1, keepdims=True), (sb, LANES))
            acc_scr[idx, rows] = _dot(p.astype(BF16), v)

    def full_tiles(j0, count):
        starts = [pl.multiple_of((j0 + t) * tk, tk) for t in range(count)]
        s = [[_dot_nt(q_ref[:, cols], k_ref[pl.ds(st, tk), cols]) for cols in maps]
             for st in starts]
        for t, st in enumerate(starts):
            v = v_ref[pl.ds(st, tk), :]
            for idx in range(len(maps)):
                _online_update(s[t][idx], v, m_scr, l_scr, acc_scr, idx)

    def pair(jj, carry):
        full_tiles(2 * jj, 2)
        return carry

    nfull = i * (tq // tk)
    lax.fori_loop(0, nfull // 2, pair, 0)

    @pl.when(nfull % 2 == 1)
    def _():
        full_tiles(nfull - 1, 1)

    lam = lam_ref[0, 0]
    hw = acc_scr.shape[2]
    o = (acc_scr[0] * _lane_tile(1.0 / l_scr[0], hw)
         - acc_scr[1] * _lane_tile(lam / l_scr[1], hw))
    o_ref[...] = _sub_ln(o, g_ref, lam_init).astype(o_ref.dtype)


def _attn_prompt(lam, q, k, v, subln_g, tq, tk, sb, lam_init):
    bsz, L, da = q.shape
    hw = 2 * HEAD_DIM
    nh = da // hw
    assert tq % tk == 0 and tq % sb == 0 and sb % CHUNK == 0
    return pl.pallas_call(
        functools.partial(_attn_prompt_kernel, tq=tq, tk=tk, sb=sb, lam_init=lam_init),
        grid=(bsz, nh, L // tq),
        in_specs=[
            pl.BlockSpec(memory_space=pltpu.SMEM),
            pl.BlockSpec((None, tq, hw), lambda b, h, i: (b, i, h)),
            pl.BlockSpec((None, L, hw), lambda b, h, i: (b, 0, h)),
            pl.BlockSpec((None, L, hw), lambda b, h, i: (b, 0, h)),
            pl.BlockSpec((1, hw), lambda b, h, i: (0, 0)),
        ],
        out_specs=pl.BlockSpec((None, tq, hw), lambda b, h, i: (b, i, h)),
        out_shape=jax.ShapeDtypeStruct((bsz, L, da), BF16),
        scratch_shapes=[
            pltpu.VMEM((2, tq, LANES), F32),
            pltpu.VMEM((2, tq, LANES), F32),
            pltpu.VMEM((2, tq, hw), F32),
        ],
        compiler_params=_params(("parallel", "parallel", "arbitrary")),
        name="attn_prompt",
    )(lam, q, k, v, subln_g)


def _attn_sample_kernel(lam_ref, q_ref, ck_ref, cv_ref, kn_ref, vn_ref, g_ref, o_ref,
                        m_scr, l_scr, acc_scr, vh_scr, *, P, T, pc, nsub, lam_init):
    j = pl.program_id(1)
    hw = 2 * HEAD_DIM
    qc = _chunk_id(P + lax.broadcasted_iota(jnp.int32, (T, 1), 0))

    @pl.when(j == 0)
    def _():
        m_scr[...] = jnp.full(m_scr.shape, MASK_VALUE, F32)
        l_scr[...] = jnp.zeros(l_scr.shape, F32)
        acc_scr[...] = jnp.zeros(acc_scr.shape, F32)

    keep_c = _chunk_id(j * pc + lax.broadcasted_iota(jnp.int32, (1, pc), 1)) <= qc
    nh = nsub // 2
    for half in range(2):
        vh_scr[half] = cv_ref[:, half * LANES:(half + 1) * LANES]
    scores = []
    for s in range(nsub):
        k = ck_ref[pl.ds(s, pc, stride=nsub), :].astype(BF16)
        q = q_ref[:, s * HEAD_DIM:(s + 1) * HEAD_DIM]
        scores.append(jnp.where(keep_c, _dot_nt(q, k), MASK_VALUE))
    for h in range(nh):
        v = jnp.concatenate([vh_scr[half, pl.ds(h, pc, stride=nh), :] for half in range(2)],
                            axis=1).astype(BF16)
        for s in (2 * h, 2 * h + 1):
            _online_update(scores[s], v, m_scr, l_scr, acc_scr, s)

    @pl.when(j == pl.num_programs(1) - 1)
    def _():
        keep_n = _chunk_id(P + lax.broadcasted_iota(jnp.int32, (1, T), 1)) <= qc
        lam = lam_ref[0, 0]
        new_scores = []
        for s in range(nsub):
            cols = slice(s * HEAD_DIM, (s + 1) * HEAD_DIM)
            new_scores.append(
                jnp.where(keep_n, _dot_nt(q_ref[:, cols], kn_ref[:, cols]), MASK_VALUE))
        for h in range(nh):
            vcols = slice(h * hw, (h + 1) * hw)
            vn = vn_ref[:, vcols]
            for s in (2 * h, 2 * h + 1):
                _online_update(new_scores[s], vn, m_scr, l_scr, acc_scr, s)
        for h in range(nh):
            vcols = slice(h * hw, (h + 1) * hw)
            o = (acc_scr[2 * h] * _lane_tile(1.0 / l_scr[2 * h], hw)
                 - acc_scr[2 * h + 1] * _lane_tile(lam / l_scr[2 * h + 1], hw))
            o_ref[:, vcols] = _sub_ln(o, g_ref, lam_init).astype(o_ref.dtype)


def _attn_sample(lam, q, cache_k, cache_v, layer, k_new, v_new, subln_g, pc, lam_init):
    bsz, T, da = q.shape
    na, _, P, nsub, _ = cache_k.shape
    hw = 2 * HEAD_DIM
    new_spec = pl.BlockSpec((None, T, da), lambda b, j: (b, 0, 0))
    return pl.pallas_call(
        functools.partial(_attn_sample_kernel, P=P, T=T, pc=pc, nsub=nsub, lam_init=lam_init),
        grid=(bsz, P // pc),
        in_specs=[
            pl.BlockSpec(memory_space=pltpu.SMEM),
            new_spec,
            pl.BlockSpec((None, None, pc * nsub, HEAD_DIM), lambda b, j: (layer, b, j, 0)),
            pl.BlockSpec((None, None, pc * (nsub // 2), hw), lambda b, j: (layer, b, j, 0)),
            new_spec,
            new_spec,
            pl.BlockSpec((1, hw), lambda b, j: (0, 0)),
        ],
        out_specs=new_spec,
        out_shape=jax.ShapeDtypeStruct((bsz, T, da), BF16),
        scratch_shapes=[
            pltpu.VMEM((nsub, T, LANES), F32),
            pltpu.VMEM((nsub, T, LANES), F32),
            pltpu.VMEM((nsub, T, hw), F32),
            pltpu.VMEM((2, pc * (nsub // 2), LANES), F32),
        ],
        compiler_params=_params(("parallel", "arbitrary")),
        name="attn_sample",
    )(lam, q, cache_k.reshape(na, bsz, P * nsub, HEAD_DIM),
      cache_v.reshape(na, bsz, P * (nsub // 2), hw), k_new, v_new, subln_g)


def _ssm_discretise(lam_re, lam_im, log_step, b_re, b_im):
    lr = jnp.minimum(lam_re, -1e-4)
    li = lam_im
    dt = jnp.exp(log_step)[:, None]
    mag = jnp.exp(lr * dt)
    abar_re = mag * jnp.cos(li * dt)
    abar_im = mag * jnp.sin(li * dt)
    nr = abar_re - 1.0
    ni = abar_im
    den = lr * lr + li * li
    kr = (nr * lr + ni * li) / den
    ki = (ni * lr - nr * li) / den
    bbar_re = kr[..., None] * b_re - ki[..., None] * b_im
    bbar_im = kr[..., None] * b_im + ki[..., None] * b_re
    return abar_re, abar_im, bbar_re, bbar_im


def _block_diag(m):
    nblk, gb, r, c = m.shape
    eye = jnp.eye(gb, dtype=m.dtype)
    return jnp.einsum("jgrc,gh->jgrhc", m, eye).reshape(nblk, gb * r, gb * c)


def _ssm_matrices(lam_re, lam_im, log_step, b_re, b_im, c_re, c_im):
    G = lam_re.shape[0]
    gb = SSM_GROUPS_PER_BLOCK
    nblk = G // gb
    abar_re, abar_im, bbar_re, bbar_im = _ssm_discretise(lam_re, lam_im, log_step, b_re, b_im)
    bt_re = bbar_re.reshape(nblk, gb, STATE_P, GROUP_CH).swapaxes(2, 3)
    bt_im = bbar_im.reshape(nblk, gb, STATE_P, GROUP_CH).swapaxes(2, 3)
    bd_b = jnp.concatenate([_block_diag(bt_re), _block_diag(bt_im)], axis=2).astype(BF16)
    ct_re = c_re.reshape(nblk, gb, GROUP_CH, STATE_P).swapaxes(2, 3)
    ct_im = c_im.reshape(nblk, gb, GROUP_CH, STATE_P).swapaxes(2, 3)
    bd_c = jnp.concatenate([_block_diag(ct_re), _block_diag(-ct_im)], axis=1).astype(BF16)
    a_re = abar_re.reshape(nblk, 1, gb * STATE_P)
    a_im = abar_im.reshape(nblk, 1, gb * STATE_P)
    return bd_b, a_re, a_im, bd_c


def _chunk_state(state, fc):
    nbt, w, dff = state.shape
    return state.reshape(nbt, w, dff // fc, fc).transpose(2, 0, 1, 3)


def _unchunk_state(state):
    nchunk, nbt, w, fc = state.shape
    return state.transpose(1, 2, 0, 3).reshape(nbt, w, nchunk * fc)


TM_PROMPT = 512
TN_GLU = 1024
FFN_CHUNK = 512
SSM_TT = 128
SSM_CB = 4
SAMPLE_PC = 512
TQ = 1024
TK = 512
TQ_DIAG = 256


def kernel(x_prompt, x_sample, c_prompt, c_sample, cache_k, cache_v, state_ssm_re, state_ssm_im, state_conv, w_ada, b_ada, ln_g, ln_b, w_up, w_dconv, b_dconv, w_down, w_ssm_in, ssm_lam_re, ssm_lam_im, ssm_log_step, ssm_b_re, ssm_b_im, ssm_c_re, ssm_c_im, ssm_d, w_glu, w_qkv, lam_q1, lam_k1, lam_q2, lam_k2, subln_g, w_o):
    depth = w_ada.shape[0]
    bp, L, d = x_prompt.shape
    bs, T, _ = x_sample.shape
    dff = w_down.shape[1]
    alpha = (2 * depth) ** 0.25
    rows_s = bs * T

    mods = _ada(jnp.concatenate([c_prompt, c_sample], axis=0), w_ada, b_ada)
    w_up16, w_down16 = w_up.astype(BF16), w_down.astype(BF16)
    w_in16, w_glu16 = w_ssm_in.astype(BF16), w_glu.astype(BF16)
    w_qkv16, w_o16 = w_qkv.astype(BF16), w_o.astype(BF16)

    xp = x_prompt
    xs = x_sample.reshape(1, rows_s, d)
    zeros_conv = jnp.zeros((bp, state_conv.shape[2], dff), F32)
    outs = {name: [] for name in ("kp", "vp", "srp", "sip", "cvp", "ks", "vs", "srs", "sis", "cvs")}

    for i in range(depth):
        mp = [m.reshape(bp, 1, d) for m in jnp.split(mods[i, :bp], 6, axis=-1)]
        ms = [jnp.repeat(m, T, axis=0).reshape(1, rows_s, d)
              for m in jnp.split(mods[i, bp:], 6, axis=-1)]
        shp1, scp1, gtp1, shp2, scp2, gtp2 = mp
        shs1, scs1, gts1, shs2, scs2, gts2 = ms
        lng = ln_g[i].reshape(2, 1, d)
        lnb = ln_b[i].reshape(2, 1, d)
        j = i // 2
        if i % 2 == 0:
            bd_b, a_re, a_im, bd_c = _ssm_matrices(
                ssm_lam_re[j], ssm_lam_im[j], ssm_log_step[j], ssm_b_re[j], ssm_b_im[j],
                ssm_c_re[j], ssm_c_im[j])
            d_skip = ssm_d[j].reshape(1, d)
            gp = ssm_lam_re.shape[1] * ssm_lam_re.shape[2]
            (up,) = _mod_matmul(xp, scp1, shp1, w_in16, j, 0, d, (F32,), TM_PROMPT, d, "ssm_in")
            zero_state = jnp.zeros((bp, gp), F32)
            zp, srp, sip = _ssm_scan(up, d_skip, bd_b, a_re, a_im, bd_c, zero_state, zero_state,
                                     SSM_TT, SSM_CB)
            xp = _glu_ln(zp, w_glu16, j, xp, gtp1, lng[0], lnb[0], TM_PROMPT, TN_GLU, alpha)
            (us,) = _mod_matmul(xs, scs1, shs1, w_in16, j, 0, d, (F32,), rows_s, d, "ssm_in")
            zs, srs, sis = _ssm_scan(us.reshape(bs, T, d), d_skip, bd_b, a_re, a_im, bd_c,
                                     state_ssm_re[j].reshape(bs, gp),
                                     state_ssm_im[j].reshape(bs, gp), T, SSM_CB)
            xs = _glu_ln(zs.reshape(1, rows_s, d), w_glu16, j, xs, gts1, lng[0], lnb[0],
                         rows_s, TN_GLU, alpha)
            st_shape = ssm_lam_re.shape[1:]
            outs["srp"].append(srp.reshape((bp,) + st_shape))
            outs["sip"].append(sip.reshape((bp,) + st_shape))
            outs["srs"].append(srs.reshape((bs,) + st_shape))
            outs["sis"].append(sis.reshape((bs,) + st_shape))
        else:
            lam_init = 0.8 - 0.6 * math.exp(-0.3 * i)
            lam = (jnp.exp(jnp.sum(lam_q1[j] * lam_k1[j])) - jnp.exp(jnp.sum(lam_q2[j] * lam_k2[j]))
                   + lam_init).reshape(1, 1)
            sg = subln_g[j].reshape(1, 2 * HEAD_DIM)
            nsub = cache_k.shape[3]
            nhead = cache_v.shape[3]
            qp, kp, vp, kp16, vp16 = _qkv(xp, scp1, shp1, w_qkv16, j, TM_PROMPT)
            op = _attn_prompt(lam, qp, kp16, vp16, sg, TQ, TK, TQ_DIAG, lam_init)
            xp = _oproj_ln(op, w_o16, j, xp, gtp1, lng[0], lnb[0], TM_PROMPT, d, alpha)
            qs, ksn, vsn, ks16, vs16 = _qkv(xs, scs1, shs1, w_qkv16, j, rows_s)
            da = w_o.shape[1]
            osm = _attn_sample(lam, qs.reshape(bs, T, da), cache_k, cache_v, j,
                               ks16.reshape(bs, T, da), vs16.reshape(bs, T, da), sg,
                               SAMPLE_PC, lam_init)
            xs = _oproj_ln(osm.reshape(1, rows_s, da), w_o16, j, xs, gts1, lng[0], lnb[0],
                           rows_s, d, alpha)
            outs["kp"].append(kp.reshape(bp, L, nsub, HEAD_DIM))
            outs["vp"].append(vp.reshape(bp, L, nhead, 2 * HEAD_DIM))
            outs["ks"].append(ksn.reshape(bs, T, nsub, HEAD_DIM))
            outs["vs"].append(vsn.reshape(bs, T, nhead, 2 * HEAD_DIM))

        bconv = b_dconv[i].reshape(1, dff)
        xp, cvp = _ffn(xp, scp2, shp2, gtp2, w_up16, w_dconv[i], bconv, w_down16, i,
                       _chunk_state(zeros_conv, FFN_CHUNK), lng[1], lnb[1],
                       1, TM_PROMPT, FFN_CHUNK, alpha)
        xs, cvs = _ffn(xs, scs2, shs2, gts2, w_up16, w_dconv[i], bconv, w_down16, i,
                       _chunk_state(state_conv[i], FFN_CHUNK), lng[1], lnb[1],
                       bs, T, FFN_CHUNK, alpha)
        outs["cvp"].append(_unchunk_state(cvp))
        outs["cvs"].append(_unchunk_state(cvs))

    return (xp, xs.reshape(bs, T, d),
            jnp.stack(outs["kp"]), jnp.stack(outs["vp"]),
            jnp.stack(outs["srp"]), jnp.stack(outs["sip"]), jnp.stack(outs["cvp"]),
            jnp.stack(outs["ks"]), jnp.stack(outs["vs"]),
            jnp.stack(outs["srs"]), jnp.stack(outs["sis"]), jnp.stack(outs["cvs"]))
```
